```python
import math
import numpy as np
import jax
import jax.numpy as jnp
from jax import lax

D_MODEL = 1024
BATCH = 8
SEQ = 2048
DEPTH = 4

PLE_DIM = 256
D_FF = 2816
EPS = 1e-6
NEG_INF = -1e30
FORCE_SCORE = 1e4

NSA_HEADS = 8
NSA_KV_GROUPS = 2
HEAD_DIM = 64
NSA_WIDTH = NSA_HEADS * HEAD_DIM
KV_WIDTH = NSA_KV_GROUPS * HEAD_DIM
CMP_BLOCK = 32
CMP_STRIDE = 16
CMP_HIDDEN = 256
SEL_BLOCK = 64
SEL_TOPK = 8
WINDOW = 512
Q_BLOCK = 128

REL_BUCKETS = 32
REL_MAX_DIST = 128

SSM_HEADS = 16
SSM_HEAD_DIM = 64
SSM_INNER = SSM_HEADS * SSM_HEAD_DIM
SSM_GROUPS = 2
SSM_STATE = 128
CONV_WIDTH = 4
SSD_CHUNK = 128
CONV_CH = SSM_INNER + 2 * SSM_GROUPS * SSM_STATE

MIX_WIDTH = NSA_WIDTH + SSM_INNER
IN_SIZES = (NSA_WIDTH, KV_WIDTH, KV_WIDTH, KV_WIDTH, KV_WIDTH, KV_WIDTH, KV_WIDTH,
            3 * NSA_HEADS, SSM_INNER, CONV_CH, SSM_HEADS)
IN_WIDTH = sum(IN_SIZES)

kernel_name = "hybrid_nsa_mamba2_macaron_block"


def rmsnorm(x, g):
    xf = x.astype(jnp.float32)
    y = xf * lax.rsqrt(jnp.mean(xf * xf, axis=-1, keepdims=True) + EPS)
    return (y * g.astype(jnp.float32)).astype(x.dtype)


def swiglu(x, w_in, w_out):
    gate, up = jnp.split(x @ w_in, 2, axis=-1)
    return (jax.nn.silu(gate) * up) @ w_out


def t5_bucket(dist):
    n = jnp.maximum(dist, 0)
    exact = REL_BUCKETS // 2
    nf = jnp.maximum(n, exact).astype(jnp.float32)
    large = exact + (jnp.log(nf / exact) / math.log(REL_MAX_DIST / exact)
                     * (REL_BUCKETS - exact)).astype(jnp.int32)
    return jnp.where(n < exact, n, jnp.minimum(large, REL_BUCKETS - 1))


def masked_softmax(logits, mask, axis=-1):
    return jax.nn.softmax(jnp.where(mask, logits.astype(jnp.float32), NEG_INF), axis=axis)


def compress_blocks(k, pos, w1, b1, w2, b2):
    b, s, g, d = k.shape
    nc = (s - CMP_BLOCK) // CMP_STRIDE + 1
    idx = np.arange(nc)[:, None] * CMP_STRIDE + np.arange(CMP_BLOCK)[None, :]
    blocks = k[:, idx] + pos[None, None, :, None, :]
    flat = blocks.transpose(0, 1, 3, 2, 4).reshape(b, nc, g, CMP_BLOCK * d)
    return jax.nn.silu(flat @ w1 + b1) @ w2 + b2


def nsa_attention(q, kc, vc, ks, vs, kw, vw, gate_logits,
                  cmp_pos, cmp_w1, cmp_b1, cmp_w2, cmp_b2, rel_table):
    b, s, _ = q.shape
    G, R, dh = NSA_KV_GROUPS, NSA_HEADS // NSA_KV_GROUPS, HEAD_DIM
    scale = dh ** -0.5
    q = q.reshape(b, s, G, R, dh)
    kc, vc, ks, vs, kw, vw = (a.reshape(b, s, G, dh) for a in (kc, vc, ks, vs, kw, vw))
    t = jnp.arange(s)
    tab = rel_table.reshape(REL_BUCKETS, G, R)

    k_cmp = compress_blocks(kc, cmp_pos[0], cmp_w1[0], cmp_b1[0], cmp_w2[0], cmp_b2[0])
    v_cmp = compress_blocks(vc, cmp_pos[1], cmp_w1[1], cmp_b1[1], cmp_w2[1], cmp_b2[1])
    nc = k_cmp.shape[1]
    ends = jnp.arange(nc) * CMP_STRIDE + CMP_BLOCK - 1
    dist_c = t[:, None] - ends[None, :]
    bias_c = tab[t5_bucket(dist_c)].transpose(2, 3, 0, 1)
    logits_c = jnp.einsum("bsgrd,bcgd->bgrsc", q, k_cmp) * scale + bias_c
    has_block = (t >= CMP_BLOCK - 1).astype(jnp.float32)[:, None]
    p_cmp = masked_softmax(logits_c, dist_c >= 0) * has_block
    o_cmp = jnp.einsum("bgrsc,bcgd->bsgrd", p_cmp, v_cmp)

    nb = s // SEL_BLOCK
    topk = min(SEL_TOPK, nb)
    c_lo = np.arange(nc) * CMP_STRIDE
    c_hi = c_lo + CMP_BLOCK - 1
    s_lo = np.arange(nb) * SEL_BLOCK
    s_hi = s_lo + SEL_BLOCK - 1
    overlap = jnp.asarray(((c_lo[:, None] <= s_hi[None, :]) &
                           (c_hi[:, None] >= s_lo[None, :])).astype(np.float32))
    importance = jnp.einsum("bgrsc,cn->bgsn", p_cmp, overlap)
    blk = jnp.arange(nb)[None, :]
    cur = (t // SEL_BLOCK)[:, None]
    forced = ((blk == 0) | (blk == cur) | (blk == cur - 1)).astype(jnp.float32)
    score = jnp.where(blk <= cur, importance + FORCE_SCORE * forced, -FORCE_SCORE)
    _, sel_idx = lax.top_k(score, topk)

    nq = s // Q_BLOCK
    ks_blk = ks.reshape(b, nb, SEL_BLOCK, G, dh).transpose(0, 3, 1, 2, 4)
    vs_blk = vs.reshape(b, nb, SEL_BLOCK, G, dh).transpose(0, 3, 1, 2, 4)
    kw_pad = jnp.pad(kw, ((0, 0), (WINDOW, 0), (0, 0), (0, 0)))
    vw_pad = jnp.pad(vw, ((0, 0), (WINDOW, 0), (0, 0), (0, 0)))
    kw_len = Q_BLOCK + WINDOW
    gather = jax.vmap(jax.vmap(lambda blocks, ids: blocks[ids]))
    g_idx = jnp.arange(G)[None, :, None, None, None]
    q_blk = q.reshape(b, nq, Q_BLOCK, G, R, dh).transpose(1, 0, 2, 3, 4, 5)
    idx_blk = sel_idx.reshape(b, G, nq, Q_BLOCK, topk).transpose(2, 0, 1, 3, 4)
    starts = jnp.arange(nq) * Q_BLOCK

    def query_block(args):
        qb, ib, q0 = args
        tq = q0 + jnp.arange(Q_BLOCK)
        k_sel = gather(ks_blk, ib)
        v_sel = gather(vs_blk, ib)
        dist_s = tq[None, None, :, None, None] - (ib[..., None] * SEL_BLOCK + jnp.arange(SEL_BLOCK))
        bias_s = tab[t5_bucket(dist_s), g_idx].transpose(0, 1, 2, 5, 3, 4)
        logits_s = jnp.einsum("bqgrd,bgqkld->bgqrkl", qb, k_sel) * scale + bias_s
        p_s = masked_softmax(logits_s, (dist_s >= 0)[:, :, :, None], axis=(-2, -1))
        o_s = jnp.einsum("bgqrkl,bgqkld->bqgrd", p_s, v_sel)
        k_win = lax.dynamic_slice_in_dim(kw_pad, q0, kw_len, axis=1)
        v_win = lax.dynamic_slice_in_dim(vw_pad, q0, kw_len, axis=1)
        kpos = q0 - WINDOW + jnp.arange(kw_len)
        dist_w = tq[:, None] - kpos[None, :]
        valid_w = (dist_w >= 0) & (dist_w < WINDOW) & (kpos[None, :] >= 0)
        bias_w = tab[t5_bucket(dist_w)].transpose(2, 3, 0, 1)
        logits_w = jnp.einsum("bqgrd,bkgd->bgrqk", qb, k_win) * scale + bias_w
        p_w = masked_softmax(logits_w, valid_w)
        o_w = jnp.einsum("bgrqk,bkgd->bqgrd", p_w, v_win)
        return o_s, o_w

    o_sel, o_win = lax.map(query_block, (q_blk, idx_blk, starts))
    unblock = lambda o: o.transpose(1, 0, 2, 3, 4, 5).reshape(b, s, G, R, dh)
    gates = jax.nn.sigmoid(gate_logits.astype(jnp.float32)).reshape(b, s, 3, G, R, 1)
    o = gates[:, :, 0] * o_cmp + gates[:, :, 1] * unblock(o_sel) + gates[:, :, 2] * unblock(o_win)
    return o.reshape(b, s, NSA_WIDTH).astype(q.dtype)


def causal_depthwise_conv(x, w, bias):
    y = lax.conv_general_dilated(
        x, w[:, None, :].astype(x.dtype), window_strides=(1,),
        padding=[(CONV_WIDTH - 1, 0)], dimension_numbers=("NWC", "WIO", "NWC"),
        feature_group_count=x.shape[-1])
    return y + bias


def ssd_scan(x, dt, a, bm, cm, d_skip):
    b, s, h, pd = x.shape
    g, n = bm.shape[2], bm.shape[3]
    r = h // g
    L = SSD_CHUNK
    c = s // L
    xf = x.astype(jnp.float32)
    xg = xf.reshape(b, c, L, g, r, pd)
    xdt = (xf * dt[..., None]).reshape(b, c, L, g, r, pd)
    bc = bm.astype(jnp.float32).reshape(b, c, L, g, n)
    cc = cm.astype(jnp.float32).reshape(b, c, L, g, n)
    da = (dt * a).reshape(b, c, L, g, r).transpose(0, 3, 4, 1, 2)
    cs = jnp.cumsum(da, axis=-1)
    causal = jnp.tril(jnp.ones((L, L), dtype=bool))
    decay_in = jnp.exp(jnp.where(causal, cs[..., :, None] - cs[..., None, :], NEG_INF))
    y_diag = jnp.einsum("bclgn,bcsgn,bgrcls,bcsgrp->bclgrp", cc, bc, decay_in, xdt)
    decay_to_end = jnp.exp(cs[..., -1:] - cs)
    chunk_states = jnp.einsum("bcsgn,bgrcs,bcsgrp->bcgrpn", bc, decay_to_end, xdt)
    chunk_decay = jnp.exp(cs[..., -1])

    def carry_state(h_prev, inp):
        st, dec = inp
        return h_prev * dec[..., None, None] + st, h_prev

    _, h_in = lax.scan(carry_state, jnp.zeros_like(chunk_states[:, 0]),
                       (chunk_states.transpose(1, 0, 2, 3, 4, 5), chunk_decay.transpose(3, 0, 1, 2)))
    h_in = h_in.transpose(1, 0, 2, 3, 4, 5)
    y_off = jnp.einsum("bclgn,bcgrpn,bgrcl->bclgrp", cc, h_in, jnp.exp(cs))
    y = y_diag + y_off + xg * d_skip.astype(jnp.float32).reshape(g, r, 1)
    return y.reshape(b, s, h * pd)


def mamba2_mixer(z, xbc, dt_raw, conv_w, conv_b, dt_bias, a_log, d_skip, norm_g):
    b, s, _ = z.shape
    xbc = jax.nn.silu(causal_depthwise_conv(xbc, conv_w, conv_b))
    bcw = SSM_GROUPS * SSM_STATE
    xs, bm, cm = jnp.split(xbc, [SSM_INNER, SSM_INNER + bcw], axis=-1)
    dt = jax.nn.softplus((dt_raw + dt_bias).astype(jnp.float32))
    a = -jnp.exp(a_log.astype(jnp.float32))
    y = ssd_scan(xs.reshape(b, s, SSM_HEADS, SSM_HEAD_DIM), dt, a,
                 bm.reshape(b, s, SSM_GROUPS, SSM_STATE), cm.reshape(b, s, SSM_GROUPS, SSM_STATE), d_skip)
    y = (y * jax.nn.silu(z.astype(jnp.float32))).reshape(b, s, SSM_GROUPS, SSM_INNER // SSM_GROUPS)
    return rmsnorm(y, norm_g.reshape(SSM_GROUPS, -1)).reshape(b, s, SSM_INNER).astype(z.dtype)


def setup_inputs(seed: int = 0) -> dict:
    key = jax.random.key(seed)
    keys = iter(list(jax.random.split(key, 40)))
    f32 = jnp.float32

    def normal(shape, scale):
        return jax.random.normal(next(keys), shape, f32) * scale

    def gain(shape):
        return 1.0 + normal(shape, 0.01)

    L = DEPTH
    dt0 = jnp.exp(jax.random.uniform(next(keys), (L, SSM_HEADS), f32, math.log(1e-3), math.log(1e-1)))
    return {
        "x": normal((BATCH, SEQ, D_MODEL), 1.0),
        "p": normal((DEPTH, BATCH, SEQ, PLE_DIM), 1.0),
        "ffn1_norm": gain((L, D_MODEL)),
        "ffn1_w_in": normal((L, D_MODEL, 2 * D_FF), D_MODEL ** -0.5),
        "ffn1_w_out": normal((L, D_FF, D_MODEL), D_FF ** -0.5),
        "mix_norm": gain((L, D_MODEL)),
        "w_mix_in": normal((L, D_MODEL, IN_WIDTH), D_MODEL ** -0.5),
        "cmp_pos": normal((L, 2, CMP_BLOCK, HEAD_DIM), 0.1),
        "cmp_w1": normal((L, 2, CMP_BLOCK * HEAD_DIM, CMP_HIDDEN), (CMP_BLOCK * HEAD_DIM) ** -0.5),
        "cmp_b1": normal((L, 2, CMP_HIDDEN), 0.02),
        "cmp_w2": normal((L, 2, CMP_HIDDEN, HEAD_DIM), CMP_HIDDEN ** -0.5),
        "cmp_b2": normal((L, 2, HEAD_DIM), 0.02),
        "rel_table": normal((REL_BUCKETS, NSA_HEADS), 0.5),
        "nsa_out_norm": gain((L, NSA_WIDTH)),
        "conv_w": normal((L, CONV_WIDTH, CONV_CH), CONV_WIDTH ** -0.5),
        "conv_b": normal((L, CONV_CH), 0.02),
        "dt_bias": dt0 + jnp.log(-jnp.expm1(-dt0)),
        "a_log": jnp.log(jax.random.uniform(next(keys), (L, SSM_HEADS), f32, 1.0, 16.0)),
        "d_skip": 1.0 + normal((L, SSM_HEADS), 0.1),
        "ssm_out_norm": gain((L, SSM_INNER)),
        "w_mix_out": normal((L, MIX_WIDTH, D_MODEL), MIX_WIDTH ** -0.5),
        "ffn2_norm": gain((L, D_MODEL)),
        "ffn2_w_in": normal((L, D_MODEL, 2 * D_FF), D_MODEL ** -0.5),
        "ffn2_w_out": normal((L, D_FF, D_MODEL), D_FF ** -0.5),
        "ple_norm": gain((L, D_MODEL)),
        "ple_gate_w": normal((L, D_MODEL, D_MODEL), D_MODEL ** -0.5),
        "ple_proj_w": normal((L, PLE_DIM, D_MODEL), PLE_DIM ** -0.5),
        "final_norm": gain((D_MODEL,)),
    }


def reference(x, p, ffn1_norm, ffn1_w_in, ffn1_w_out, mix_norm, w_mix_in,
              cmp_pos, cmp_w1, cmp_b1, cmp_w2, cmp_b2, rel_table, nsa_out_norm,
              conv_w, conv_b, dt_bias, a_log, d_skip, ssm_out_norm, w_mix_out,
              ffn2_norm, ffn2_w_in, ffn2_w_out, ple_norm, ple_gate_w, ple_proj_w, final_norm):
    offsets = [int(v) for v in np.cumsum(IN_SIZES)[:-1]]
    for i in range(DEPTH):
        x = x + 0.5 * swiglu(rmsnorm(x, ffn1_norm[i]), ffn1_w_in[i], ffn1_w_out[i])
        u = rmsnorm(x, mix_norm[i])
        (q, kc, vc, ks, vs, kw, vw, gate_logits, z, xbc, dt_raw) = jnp.split(
            u @ w_mix_in[i], offsets, axis=-1)
        o_attn = nsa_attention(q, kc, vc, ks, vs, kw, vw, gate_logits,
                               cmp_pos[i], cmp_w1[i], cmp_b1[i], cmp_w2[i], cmp_b2[i], rel_table)
        o_ssm = mamba2_mixer(z, xbc, dt_raw, conv_w[i], conv_b[i], dt_bias[i], a_log[i],
                             d_skip[i], ssm_out_norm[i])
        mixed = jnp.concatenate([rmsnorm(o_attn, nsa_out_norm[i]), o_ssm], axis=-1)
        x = x + mixed @ w_mix_out[i]
        x = x + 0.5 * swiglu(rmsnorm(x, ffn2_norm[i]), ffn2_w_in[i], ffn2_w_out[i])
        gate = jax.nn.sigmoid(rmsnorm(x, ple_norm[i]) @ ple_gate_w[i])
        x = x + gate * (p[i] @ ple_proj_w[i])
    return rmsnorm(x, final_norm)
```

```python
import functools
import math

import numpy as np
import jax
import jax.numpy as jnp
from jax import lax
from jax.experimental import pallas as pl
from jax.experimental.pallas import tpu as pltpu

F32 = jnp.float32
BF16 = jnp.bfloat16

D_MODEL = 1024
DEPTH = 4
PLE_DIM = 256
D_FF = 2816
EPS = 1e-6
NEG_INF = -1e30
FORCE_SCORE = 1e4

NSA_HEADS = 8
NSA_KV_GROUPS = 2
NSA_REP = NSA_HEADS // NSA_KV_GROUPS
HEAD_DIM = 64
NSA_WIDTH = NSA_HEADS * HEAD_DIM
KV_WIDTH = NSA_KV_GROUPS * HEAD_DIM
CMP_BLOCK = 32
CMP_STRIDE = 16
CMP_HIDDEN = 256
SEL_BLOCK = 64
SEL_TOPK = 8
WINDOW = 512
REL_BUCKETS = 32
REL_MAX_DIST = 128

SSM_HEADS = 16
SSM_HEAD_DIM = 64
SSM_INNER = SSM_HEADS * SSM_HEAD_DIM
SSM_GROUPS = 2
SSM_STATE = 128
CONV_WIDTH = 4
SSD_CHUNK = 128
CONV_CH = SSM_INNER + 2 * SSM_GROUPS * SSM_STATE

LANES = 128
VMEM_LIMIT_BYTES = 48 * 1024 * 1024

FFN_TM = 1024
FFN_TF = 256
PROJ_TM = 512
Q_TILE = 128
K_TILE = 256
HEAD_ORDER = (0, 4, 1, 5, 2, 6, 3, 7)
GATE_LANE0 = 0
DT_LANE0 = 3 * NSA_HEADS
N_BIAS_TABLES = 6


def _dot(a, b):
    return jnp.dot(a, b, preferred_element_type=F32)


def _dot_nt(a, b):
    return lax.dot_general(a, b, (((1,), (1,)), ((), ())), preferred_element_type=F32)


def _split3(v):
    hi = v.astype(BF16)
    r = v - hi.astype(F32)
    mid = r.astype(BF16)
    lo = (r - mid.astype(F32)).astype(BF16)
    return hi, mid, lo


def _dot_f32_lhs(v, e):
    hi, mid, lo = _split3(v)
    return _dot(hi, e) + _dot(mid, e) + _dot(lo, e)


def _dot_f32_rhs(e, v):
    hi, mid, lo = _split3(v)
    return _dot(e, hi) + _dot(e, mid) + _dot(e, lo)


def _rms(x, g):
    ms = jnp.mean(x * x, axis=-1, keepdims=True)
    return x * lax.rsqrt(ms + EPS) * g


def _silu(x):
    return x * jax.nn.sigmoid(x)


def _cparams(*sem):
    return pltpu.CompilerParams(dimension_semantics=sem, vmem_limit_bytes=VMEM_LIMIT_BYTES)


def _ffn_kernel(x_ref, g_ref, wg_ref, wu_ref, wo_ref, o_ref, xn_ref, acc_ref):
    j = pl.program_id(1)

    @pl.when(j == 0)
    def _():
        xn_ref[...] = _rms(x_ref[...], g_ref[...]).astype(BF16)
        acc_ref[...] = jnp.zeros_like(acc_ref)

    xn = xn_ref[...]
    gate = _dot(xn, wg_ref[...])
    up = _dot(xn, wu_ref[...])
    h = (_silu(gate) * up).astype(BF16)
    acc_ref[...] += _dot(h, wo_ref[...])

    @pl.when(j == pl.num_programs(1) - 1)
    def _():
        o_ref[...] = x_ref[...] + 0.5 * acc_ref[...]


def _ffn(x, gain, w_in, w_out, layer):
    t, d = x.shape
    nf = D_FF // FFN_TF
    return pl.pallas_call(
        _ffn_kernel,
        grid=(t // FFN_TM, nf),
        in_specs=[
            pl.BlockSpec((FFN_TM, d), lambda i, j: (i, 0)),
            pl.BlockSpec((None, 1, d), lambda i, j: (layer, 0, 0)),
            pl.BlockSpec((None, d, FFN_TF), lambda i, j: (layer, 0, j)),
            pl.BlockSpec((None, d, FFN_TF), lambda i, j: (layer, 0, j + nf)),
            pl.BlockSpec((None, FFN_TF, d), lambda i, j: (layer, j, 0)),
        ],
        out_specs=pl.BlockSpec((FFN_TM, d), lambda i, j: (i, 0)),
        out_shape=jax.ShapeDtypeStruct((t, d), F32),
        scratch_shapes=[pltpu.VMEM((FFN_TM, d), BF16), pltpu.VMEM((FFN_TM, d), F32)],
        compiler_params=_cparams("parallel", "arbitrary"),
        name="ffn",
    )(x, gain, w_in, w_in, w_out)


_C_Q = (0, NSA_WIDTH)
_C_KC = (_C_Q[1], _C_Q[1] + KV_WIDTH)
_C_VC = (_C_KC[1], _C_KC[1] + KV_WIDTH)
_C_KV = (_C_VC[1], _C_VC[1] + 4 * KV_WIDTH)
_C_MISC = (_C_KV[1], _C_KV[1] + LANES)
_C_Z = (_C_MISC[1], _C_MISC[1] + SSM_INNER)
_C_XBC = (_C_Z[1], _C_Z[1] + CONV_CH)
PROJ_COLS = _C_XBC[1]


def _inproj_kernel(x_ref, g_ref, w_ref, q_ref, kc_ref, vc_ref, kv_ref, misc_ref, z_ref, xbc_ref):
    xn = _rms(x_ref[...], g_ref[...]).astype(BF16)
    q_ref[...] = _dot(xn, w_ref[:, _C_Q[0]:_C_Q[1]]).astype(BF16)
    kc_ref[...] = _dot(xn, w_ref[:, _C_KC[0]:_C_KC[1]])
    vc_ref[...] = _dot(xn, w_ref[:, _C_VC[0]:_C_VC[1]])
    kv_ref[...] = _dot(xn, w_ref[:, _C_KV[0]:_C_KV[1]]).astype(BF16)
    misc_ref[...] = _dot(xn, w_ref[:, _C_MISC[0]:_C_MISC[1]])
    z_ref[...] = _dot(xn, w_ref[:, _C_Z[0]:_C_Z[1]])
    xbc_ref[...] = _dot(xn, w_ref[:, _C_XBC[0]:_C_XBC[1]])


def _inproj(x, gain, w, layer):
    t, d = x.shape
    tm = PROJ_TM
    widths = (NSA_WIDTH, KV_WIDTH, KV_WIDTH, 4 * KV_WIDTH, LANES, SSM_INNER, CONV_CH)
    dtypes = (BF16, F32, F32, BF16, F32, F32, F32)
    return pl.pallas_call(
        _inproj_kernel,
        grid=(t // tm,),
        in_specs=[
            pl.BlockSpec((tm, d), lambda i: (i, 0)),
            pl.BlockSpec((None, 1, d), lambda i: (layer, 0, 0)),
            pl.BlockSpec((None, d, PROJ_COLS), lambda i: (layer, 0, 0)),
        ],
        out_specs=[pl.BlockSpec((tm, n), lambda i: (i, 0)) for n in widths],
        out_shape=[jax.ShapeDtypeStruct((t, n), dt) for n, dt in zip(widths, dtypes)],
        compiler_params=_cparams("parallel"),
        name="inproj",
    )(x, gain, w)


def _prep_inproj_weight(w_mix_in):
    offs = np.cumsum((0, NSA_WIDTH) + (KV_WIDTH,) * 6 + (3 * NSA_HEADS, SSM_INNER, CONV_CH, SSM_HEADS))
    sl = lambda a, b: w_mix_in[:, :, a:b]
    q = sl(offs[0], offs[1])
    q = jnp.concatenate([q[:, :, h * HEAD_DIM:(h + 1) * HEAD_DIM] for h in HEAD_ORDER], axis=-1)
    kc, vc = sl(offs[1], offs[2]), sl(offs[2], offs[3])
    kv = sl(offs[3], offs[7])
    gates = sl(offs[7], offs[8])
    z = sl(offs[8], offs[9])
    xbc = sl(offs[9], offs[10])
    dt = sl(offs[10], offs[11])
    pad = jnp.zeros(w_mix_in.shape[:2] + (LANES - 3 * NSA_HEADS - SSM_HEADS,), w_mix_in.dtype)
    misc = jnp.concatenate([gates, dt, pad], axis=-1)
    return jnp.concatenate([q, kc, vc, kv, misc, z, xbc], axis=-1).astype(BF16)


def _compress_kernel(kc_ref, vc_ref, pos_ref, w1a_ref, w1b_ref, b1_ref, w2_ref, b2_ref,
                     kcmp_ref, vcmp_ref):
    nrow = kc_ref.shape[0]
    for which, (src, dst) in enumerate(((kc_ref, kcmp_ref), (vc_ref, vcmp_ref))):
        r = src[...]
        ra = (r + pos_ref[which, 0]).astype(BF16)
        rb = (r + pos_ref[which, 1]).astype(BF16)
        ha = _dot(ra, w1a_ref[which])
        hb = _dot(rb, w1b_ref[which])
        h = ha + pltpu.roll(hb, nrow - 1, 0) + b1_ref[which]
        out = _dot(_silu(h).astype(BF16), w2_ref[which]) + b2_ref[which]
        dst[...] = out.astype(BF16)


def _compress(kc, vc, prep, batch, seq):
    nrow = seq // CMP_STRIDE
    wide = CMP_STRIDE * KV_WIDTH
    kcr = kc.reshape(batch, nrow, wide)
    vcr = vc.reshape(batch, nrow, wide)
    full = lambda a: pl.BlockSpec(a.shape, lambda b: (0,) * a.ndim)
    consts = (prep["pos"], prep["w1a"], prep["w1b"], prep["b1"], prep["w2"], prep["b2"])
    return pl.pallas_call(
        _compress_kernel,
        grid=(batch,),
        in_specs=[pl.BlockSpec((None, nrow, wide), lambda b: (b, 0, 0))] * 2 + [full(a) for a in consts],
        out_specs=[pl.BlockSpec((None, nrow, KV_WIDTH), lambda b: (b, 0, 0))] * 2,
        out_shape=[jax.ShapeDtypeStruct((batch, nrow, KV_WIDTH), BF16)] * 2,
        compiler_params=_cparams("parallel"),
        name="nsa_compress",
    )(kcr, vcr, *consts)


def _prep_compress(cmp_pos, cmp_w1, cmp_b1, cmp_w2, cmp_b2):
    g = NSA_KV_GROUPS
    half = CMP_BLOCK // 2
    eye = jnp.eye(g, dtype=F32)
    w1 = cmp_w1.reshape(2, CMP_BLOCK, HEAD_DIM, CMP_HIDDEN)

    def expand(w):
        e = jnp.einsum("wldh,pg->wlpdgh", w, eye)
        return e.reshape(2, half * g * HEAD_DIM, g * CMP_HIDDEN).astype(BF16)

    def pos_rows(p):
        return jnp.broadcast_to(p[:, :, None, :], (2, half, g, HEAD_DIM)).reshape(2, 1, half * g * HEAD_DIM)

    w2 = jnp.einsum("whd,gp->wghpd", cmp_w2, eye).reshape(2, g * CMP_HIDDEN, g * HEAD_DIM).astype(BF16)
    return {
        "pos": jnp.stack([pos_rows(cmp_pos[:, :half]), pos_rows(cmp_pos[:, half:])], axis=1),
        "w1a": expand(w1[:, :half]),
        "w1b": expand(w1[:, half:]),
        "b1": jnp.tile(cmp_b1, (1, g))[:, None, :],
        "w2": w2,
        "b2": jnp.tile(cmp_b2, (1, g))[:, None, :],
    }


def _t5_bucket_np(dist):
    n = np.maximum(dist, 0)
    exact = REL_BUCKETS // 2
    nf = np.maximum(n, exact).astype(np.float64)
    large = exact + (np.log(nf / exact) / math.log(REL_MAX_DIST / exact) * (REL_BUCKETS - exact)).astype(np.int64)
    return np.where(n < exact, n, np.minimum(large, REL_BUCKETS - 1)).astype(np.int32)


def _nsa_tables(rel_table, seq):
    tab = rel_table[:, np.asarray(HEAD_ORDER)]
    i = np.arange(Q_TILE)[:, None]
    j = np.arange(K_TILE)[None, :]
    tbs = []
    for delta in range(N_BIAS_TABLES):
        d = Q_TILE * delta + i - j
        valid = (d >= 0) & (d < WINDOW if delta >= 4 else True)
        bias = jnp.take(tab, jnp.asarray(_t5_bucket_np(d)), axis=0)
        tbs.append(jnp.where(jnp.asarray(valid)[..., None], bias, NEG_INF).transpose(2, 0, 1))
    tb = jnp.stack(tbs)
    t = np.arange(seq)[:, None]
    c = np.arange(seq // CMP_STRIDE)[None, :]
    dc = t - (c * CMP_STRIDE + CMP_BLOCK - 1)
    bc = jnp.take(tab, jnp.asarray(_t5_bucket_np(dc)), axis=0)
    biasc = jnp.where(jnp.asarray(dc >= 0)[..., None], bc, NEG_INF).transpose(2, 0, 1)
    return tb.astype(F32), biasc.astype(F32)


def _nsa_consts(seq):
    nc = seq // CMP_STRIDE
    nb = seq // SEL_BLOCK
    c = np.arange(nc)
    blk = np.arange(LANES)
    c_lo, c_hi = c * CMP_STRIDE, c * CMP_STRIDE + CMP_BLOCK - 1
    s_lo, s_hi = blk * SEL_BLOCK, blk * SEL_BLOCK + SEL_BLOCK - 1
    ov = (c_lo[:, None] <= s_hi[None, :]) & (c_hi[:, None] >= s_lo[None, :]) & (blk[None, :] < nb)
    ov[nc - 1, :] = False
    ex = (np.arange(seq)[None, :] // SEL_BLOCK) == blk[:, None]
    return jnp.asarray(ov, BF16), jnp.asarray(ex, BF16)


def _nsa_kernel(q_ref, kcmp_ref, vcmp_ref, ks_ref, vs_ref, kw_ref, vw_ref, misc_ref, biasc_ref, tb_ref,
                ov_ref, ex_ref, o_ref, qp_ref, madd_ref, m_ref, l_ref, acc_ref, p_ref):
    n = pl.program_id(1)
    nsel_blocks = ks_ref.shape[0] // SEL_BLOCK
    nkt = ks_ref.shape[0] // K_TILE
    nhb = NSA_HEADS
    lane = lax.broadcasted_iota(jnp.int32, (Q_TILE, LANES), 1)
    low = lane < HEAD_DIM
    rows = lambda hb: slice(hb * Q_TILE, (hb + 1) * Q_TILE)

    scale = HEAD_DIM ** -0.5
    for j in range(nhb // 2):
        slab = q_ref[:, j * LANES:(j + 1) * LANES] * jnp.asarray(scale, BF16)
        zero = jnp.zeros_like(slab)
        qp_ref[rows(2 * j), :] = jnp.where(low, slab, zero)
        qp_ref[rows(2 * j + 1), :] = jnp.where(low, zero, slab)
    qp = qp_ref[...]

    trow = n * Q_TILE + lax.broadcasted_iota(jnp.int32, (Q_TILE, 1), 0)
    has_block = (trow >= CMP_BLOCK - 1).astype(F32)
    sc = _dot_nt(qp, kcmp_ref[...])
    psum = [None, None]
    for hb in range(nhb):
        s = sc[rows(hb)] + biasc_ref[hb]
        m = jnp.max(s, axis=-1, keepdims=True)
        e = jnp.exp(s - m)
        p = e / jnp.sum(e, axis=-1, keepdims=True) * has_block
        p_ref[rows(hb), 0:LANES] = p.astype(BF16)
        g = hb % 2
        psum[g] = p if psum[g] is None else psum[g] + p
    o_cmp = _dot(p_ref[:, 0:LANES], vcmp_ref[...])

    blk = lane
    cur = trow // SEL_BLOCK
    forced = ((blk == 0) | (blk == cur) | (blk == cur - 1)).astype(F32)
    blk_f = blk.astype(F32)
    for g in range(NSA_KV_GROUPS):
        imp = _dot_f32_lhs(psum[g], ov_ref[...])
        score = jnp.where(blk <= cur, imp + FORCE_SCORE * forced, -FORCE_SCORE)
        score = jnp.where(blk < nsel_blocks, score, -jnp.inf)
        sel = jnp.zeros((Q_TILE, LANES), F32)
        for _ in range(SEL_TOPK):
            mx = jnp.max(score, axis=-1, keepdims=True)
            first = jnp.min(jnp.where(score == mx, blk_f, float(LANES)), axis=-1, keepdims=True)
            hit = blk_f == first
            sel = jnp.where(hit, 1.0, sel)
            score = jnp.where(hit, -jnp.inf, score)
        selb = sel.astype(BF16)
        for t2 in range(nkt):
            e = _dot(selb, ex_ref[:, t2 * K_TILE:(t2 + 1) * K_TILE])
            madd_ref[t2, g] = (e - 1.0) * (-NEG_INF)

    def attend(k_ref, v_ref, t2_lo, t2_hi, selected):
        m_ref[...] = jnp.full_like(m_ref, -jnp.inf)
        l_ref[...] = jnp.zeros_like(l_ref)
        acc_ref[...] = jnp.zeros_like(acc_ref)

        def body(t2, carry):
            start = pl.multiple_of(t2 * K_TILE, K_TILE)
            k = k_ref[pl.ds(start, K_TILE), :]
            v = v_ref[pl.ds(start, K_TILE), :]
            s_all = _dot_nt(qp, k)
            delta = n - 2 * t2
            ti = jnp.minimum(delta, 3) if selected else delta
            for hb in range(nhb):
                s = s_all[rows(hb)] + tb_ref[ti, hb]
                if selected:
                    s = s + madd_ref[t2, hb % 2]
                m_prev = m_ref[rows(hb), :]
                m_new = jnp.maximum(m_prev, jnp.max(s, axis=-1, keepdims=True))
                alpha = jnp.exp(m_prev - m_new)
                p = jnp.exp(s - m_new)
                l_ref[rows(hb), :] = alpha * l_ref[rows(hb), :] + jnp.sum(p, axis=-1, keepdims=True)
                m_ref[rows(hb), :] = m_new
                acc_ref[rows(hb), :] = alpha * acc_ref[rows(hb), :]
                p_ref[rows(hb), :] = p.astype(BF16)
            acc_ref[...] += _dot(p_ref[...], v)
            return carry

        lax.fori_loop(t2_lo, t2_hi, body, 0)
        return acc_ref[...] / l_ref[...]

    t2_diag = n // 2
    o_sel = attend(ks_ref, vs_ref, 0, t2_diag + 1, True)
    o_win = attend(kw_ref, vw_ref, jnp.maximum(t2_diag - 2, 0), t2_diag + 1, False)

    gates = jax.nn.sigmoid(misc_ref[...])

    def gate(branch, hb):
        col = GATE_LANE0 + branch * NSA_HEADS + HEAD_ORDER[hb]
        return gates[:, col:col + 1]

    for j in range(nhb // 2):
        outs = []
        for hb in (2 * j, 2 * j + 1):
            outs.append(gate(0, hb) * o_cmp[rows(hb)] + gate(1, hb) * o_sel[rows(hb)]
                        + gate(2, hb) * o_win[rows(hb)])
        o_ref[:, j * LANES:(j + 1) * LANES] = jnp.where(low, outs[0], outs[1])


def _nsa_attention(q, kcmp, vcmp, kv, misc, tb, biasc, ov, ex, batch, seq):
    nq = seq // Q_TILE
    nc = seq // CMP_STRIDE
    rows_all = NSA_HEADS * Q_TILE
    kvspec = lambda c: pl.BlockSpec((seq, KV_WIDTH), lambda b, n: (b, c))
    return pl.pallas_call(
        _nsa_kernel,
        grid=(batch, nq),
        in_specs=[
            pl.BlockSpec((Q_TILE, NSA_WIDTH), lambda b, n: (b * nq + n, 0)),
            pl.BlockSpec((None, nc, KV_WIDTH), lambda b, n: (b, 0, 0)),
            pl.BlockSpec((None, nc, KV_WIDTH), lambda b, n: (b, 0, 0)),
            kvspec(0), kvspec(1), kvspec(2), kvspec(3),
            pl.BlockSpec((Q_TILE, LANES), lambda b, n: (b * nq + n, 0)),
            pl.BlockSpec((NSA_HEADS, Q_TILE, nc), lambda b, n: (0, n, 0)),
            pl.BlockSpec(tb.shape, lambda b, n: (0, 0, 0, 0)),
            pl.BlockSpec(ov.shape, lambda b, n: (0, 0)),
            pl.BlockSpec(ex.shape, lambda b, n: (0, 0)),
        ],
        out_specs=pl.BlockSpec((Q_TILE, NSA_WIDTH), lambda b, n: (b * nq + n, 0)),
        out_shape=jax.ShapeDtypeStruct((batch * seq, NSA_WIDTH), F32),
        scratch_shapes=[
            pltpu.VMEM((rows_all, LANES), BF16),
            pltpu.VMEM((seq // K_TILE, NSA_KV_GROUPS, Q_TILE, K_TILE), F32),
            pltpu.VMEM((rows_all, 1), F32),
            pltpu.VMEM((rows_all, 1), F32),
            pltpu.VMEM((rows_all, LANES), F32),
            pltpu.VMEM((rows_all, K_TILE), BF16),
        ],
        compiler_params=_cparams("parallel", "arbitrary"),
        name="nsa_attention",
    )(q, kcmp, vcmp, kv, kv, kv, kv, misc, biasc, tb, ov, ex)


def _softplus(x):
    return jnp.maximum(x, 0.0) + jnp.log1p(jnp.exp(-jnp.abs(x)))


def _ssd_kernel(xbc_ref, z_ref, misc_ref, convw_ref, convb_ref, dtb_ref, alog_ref, dskip_ref, gain_ref,
                tri_ref, e1_ref, e2_ref, o_ref, prev_ref, h_ref):
    c = pl.program_id(1)
    L = SSD_CHUNK
    gw = SSM_INNER // SSM_GROUPS
    hpg = SSM_HEADS // SSM_GROUPS

    @pl.when(c == 0)
    def _():
        prev_ref[...] = jnp.zeros_like(prev_ref)
        h_ref[...] = jnp.zeros_like(h_ref)

    x = xbc_ref[...]
    xp = prev_ref[...]
    row = lax.broadcasted_iota(jnp.int32, (L, 1), 0)
    acc = convb_ref[...] + x * convw_ref[CONV_WIDTH - 1:CONV_WIDTH, :]
    for k in range(1, CONV_WIDTH):
        xk = jnp.where(row >= k, pltpu.roll(x, k, 0), pltpu.roll(xp, k, 0))
        acc = acc + xk * convw_ref[CONV_WIDTH - 1 - k:CONV_WIDTH - k, :]
    prev_ref[...] = x
    xa = _silu(acc)
    xs = xa[:, :SSM_INNER]
    bm = xa[:, SSM_INNER:SSM_INNER + SSM_GROUPS * SSM_STATE]
    cm = xa[:, SSM_INNER + SSM_GROUPS * SSM_STATE:]

    dt = _softplus(misc_ref[...] + dtb_ref[...])
    da = dt * (-jnp.exp(alog_ref[...]))
    cs = _dot_f32_rhs(tri_ref[...], da)
    cs_t = cs.T
    dtx = _dot_f32_lhs(dt, e1_ref[...])
    csx = _dot_f32_lhs(cs, e1_ref[...])
    cs2 = _dot_f32_lhs(cs, e2_ref[...])
    cs_last = csx[L - 1:L, :]
    xdt = xs * dtx
    xdt_b = xdt.astype(BF16)
    xw_b = (xdt * jnp.exp(cs_last - csx)).astype(BF16)
    ecs = jnp.exp(csx)
    state_decay = jnp.exp(cs_last)

    li = lax.broadcasted_iota(jnp.int32, (L, L), 0)
    si = lax.broadcasted_iota(jnp.int32, (L, L), 1)
    causal = li >= si
    low = si < SSM_HEAD_DIM

    ys = []
    for g in range(SSM_GROUPS):
        bg = bm[:, g * SSM_STATE:(g + 1) * SSM_STATE]
        cg = cm[:, g * SSM_STATE:(g + 1) * SSM_STATE].astype(BF16)
        cb = _dot_nt(cg, bg.astype(BF16))
        h_g = h_ref[:, g * gw:(g + 1) * gw]
        y_off = _dot(cg, h_g.astype(BF16)) * ecs[:, g * gw:(g + 1) * gw]
        for pr in range(hpg // 2):
            h0 = g * hpg + 2 * pr
            gs = []
            for hh in (h0, h0 + 1):
                col = cs2[:, hh * LANES:(hh + 1) * LANES]
                rowv = cs_t[DT_LANE0 + hh:DT_LANE0 + hh + 1, :]
                dec = jnp.exp(jnp.where(causal, col - rowv, NEG_INF))
                gs.append((cb * dec).astype(BF16))
            ch = slice(h0 * SSM_HEAD_DIM, (h0 + 2) * SSM_HEAD_DIM)
            xpair = xdt_b[:, ch]
            zero = jnp.zeros_like(xpair)
            rhs = jnp.concatenate([jnp.where(low, xpair, zero), jnp.where(low, zero, xpair)], axis=0)
            y_diag = _dot(jnp.concatenate(gs, axis=1), rhs)
            off = slice(2 * pr * SSM_HEAD_DIM, (2 * pr + 2) * SSM_HEAD_DIM)
            ys.append(y_diag + y_off[:, off] + xs[:, ch] * dskip_ref[:, ch])
        st = _dot(bg.T.astype(BF16), xw_b[:, g * gw:(g + 1) * gw])
        h_ref[:, g * gw:(g + 1) * gw] = h_g * state_decay[:, g * gw:(g + 1) * gw] + st

    y = jnp.concatenate(ys, axis=1) * _silu(z_ref[...])
    outs = []
    for g in range(SSM_GROUPS):
        outs.append(_rms(y[:, g * gw:(g + 1) * gw], gain_ref[:, g * gw:(g + 1) * gw]))
    o_ref[...] = jnp.concatenate(outs, axis=1).astype(BF16)


def _ssd_consts():
    lane = np.arange(LANES)
    tri = (np.arange(SSD_CHUNK)[:, None] >= np.arange(SSD_CHUNK)[None, :])
    head1 = np.arange(SSM_INNER) // SSM_HEAD_DIM
    e1 = (lane[:, None] - DT_LANE0) == head1[None, :]
    head2 = np.arange(SSM_HEADS * LANES) // LANES
    e2 = (lane[:, None] - DT_LANE0) == head2[None, :]
    return jnp.asarray(tri, BF16), jnp.asarray(e1, BF16), jnp.asarray(e2, BF16)


def _ssd(xbc, z, misc, prm, consts, layer, batch, seq):
    nch = seq // SSD_CHUNK
    tok = lambda n: pl.BlockSpec((SSD_CHUNK, n), lambda b, c: (b * nch + c, 0))
    lay = lambda a: pl.BlockSpec((None,) + a.shape[1:], lambda b, c: (layer,) + (0,) * (a.ndim - 1))
    full = lambda a: pl.BlockSpec(a.shape, lambda b, c: (0,) * a.ndim)
    params = (prm["conv_w"], prm["conv_b"], prm["dt_bias"], prm["a_log"], prm["d_skip"], prm["ssm_gain"])
    return pl.pallas_call(
        _ssd_kernel,
        grid=(batch, nch),
        in_specs=[tok(CONV_CH), tok(SSM_INNER), tok(LANES)] + [lay(a) for a in params] + [full(a) for a in consts],
        out_specs=tok(SSM_INNER),
        out_shape=jax.ShapeDtypeStruct((batch * seq, SSM_INNER), BF16),
        scratch_shapes=[pltpu.VMEM((SSD_CHUNK, CONV_CH), F32), pltpu.VMEM((SSM_STATE, SSM_INNER), F32)],
        compiler_params=_cparams("parallel", "arbitrary"),
        name="ssd",
    )(xbc, z, misc, *params, *consts)


def _prep_ssd_params(conv_w, conv_b, dt_bias, a_log, d_skip, ssm_out_norm):
    nl = conv_w.shape[0]

    def dt_lanes(v):
        out = jnp.zeros((nl, 1, LANES), F32)
        return out.at[:, 0, DT_LANE0:DT_LANE0 + SSM_HEADS].set(v)

    return {
        "conv_w": conv_w,
        "conv_b": conv_b[:, None, :],
        "dt_bias": dt_lanes(dt_bias),
        "a_log": dt_lanes(a_log),
        "d_skip": jnp.repeat(d_skip, SSM_HEAD_DIM, axis=-1)[:, None, :],
        "ssm_gain": ssm_out_norm[:, None, :],
    }


def _outproj_kernel(x_ref, oa_ref, os_ref, g_ref, wa_ref, ws_ref, o_ref):
    an = _rms(oa_ref[...], g_ref[...]).astype(BF16)
    o_ref[...] = x_ref[...] + _dot(an, wa_ref[...]) + _dot(os_ref[...], ws_ref[...])


def _outproj(x, o_attn, o_ssm, gain, wa, ws, layer):
    t, d = x.shape
    tm = PROJ_TM
    return pl.pallas_call(
        _outproj_kernel,
        grid=(t // tm,),
        in_specs=[
            pl.BlockSpec((tm, d), lambda i: (i, 0)),
            pl.BlockSpec((tm, NSA_WIDTH), lambda i: (i, 0)),
            pl.BlockSpec((tm, SSM_INNER), lambda i: (i, 0)),
            pl.BlockSpec((None, 1, NSA_WIDTH), lambda i: (layer, 0, 0)),
            pl.BlockSpec((None, NSA_WIDTH, d), lambda i: (layer, 0, 0)),
            pl.BlockSpec((None, SSM_INNER, d), lambda i: (layer, 0, 0)),
        ],
        out_specs=pl.BlockSpec((tm, d), lambda i: (i, 0)),
        out_shape=jax.ShapeDtypeStruct((t, d), F32),
        compiler_params=_cparams("parallel"),
        name="outproj",
    )(x, o_attn, o_ssm, gain, wa, ws)


def _perm_heads(a, axis):
    idx = np.concatenate([np.arange(h * HEAD_DIM, (h + 1) * HEAD_DIM) for h in HEAD_ORDER])
    return jnp.take(a, jnp.asarray(idx), axis=axis)


def _ple_kernel(x_ref, p_ref, g_ref, wg_ref, wp_ref, fg_ref, o_ref, *, final):
    x = x_ref[...]
    xn = _rms(x, g_ref[...]).astype(BF16)
    gate = jax.nn.sigmoid(_dot(xn, wg_ref[...]))
    y = x + gate * _dot(p_ref[...].astype(BF16), wp_ref[...])
    if final:
        y = _rms(y, fg_ref[...])
    o_ref[...] = y


def _ple(x, p, gain, wg, wp, final_gain, layer, final):
    t, d = x.shape
    tm = PROJ_TM
    return pl.pallas_call(
        functools.partial(_ple_kernel, final=final),
        grid=(t // tm,),
        in_specs=[
            pl.BlockSpec((tm, d), lambda i: (i, 0)),
            pl.BlockSpec((None, tm, PLE_DIM), lambda i: (layer, i, 0)),
            pl.BlockSpec((None, 1, d), lambda i: (layer, 0, 0)),
            pl.BlockSpec((None, d, d), lambda i: (layer, 0, 0)),
            pl.BlockSpec((None, PLE_DIM, d), lambda i: (layer, 0, 0)),
            pl.BlockSpec((1, d), lambda i: (0, 0)),
        ],
        out_specs=pl.BlockSpec((tm, d), lambda i: (i, 0)),
        out_shape=jax.ShapeDtypeStruct((t, d), F32),
        compiler_params=_cparams("parallel"),
        name="ple",
    )(x, p, gain, wg, wp, final_gain)


def kernel(x, p, ffn1_norm, ffn1_w_in, ffn1_w_out, mix_norm, w_mix_in, cmp_pos, cmp_w1, cmp_b1, cmp_w2,
           cmp_b2, rel_table, nsa_out_norm, conv_w, conv_b, dt_bias, a_log, d_skip, ssm_out_norm, w_mix_out,
           ffn2_norm, ffn2_w_in, ffn2_w_out, ple_norm, ple_gate_w, ple_proj_w, final_norm):
    batch, seq, d = x.shape
    depth = p.shape[0]
    t = batch * seq
    bf = lambda a: a.astype(BF16)
    row = lambda a: a[:, None, :]

    ffn1_in, ffn1_out, ffn2_in, ffn2_out = bf(ffn1_w_in), bf(ffn1_w_out), bf(ffn2_w_in), bf(ffn2_w_out)
    w_proj = _prep_inproj_weight(w_mix_in)
    wo_attn = bf(_perm_heads(w_mix_out[:, :NSA_WIDTH], axis=1))
    wo_ssm = bf(w_mix_out[:, NSA_WIDTH:])
    nsa_gain = row(_perm_heads(nsa_out_norm, axis=1))
    ssd_prm = _prep_ssd_params(conv_w, conv_b, dt_bias, a_log, d_skip, ssm_out_norm)
    ssd_consts = _ssd_consts()
    tb, biasc = _nsa_tables(rel_table, seq)
    ov, ex = _nsa_consts(seq)
    wg, wp = bf(ple_gate_w), bf(ple_proj_w)
    p2 = p.reshape(depth, t, PLE_DIM)
    fgain = final_norm[None, :]

    h = x.reshape(t, d)
    for i in range(depth):
        h = _ffn(h, row(ffn1_norm), ffn1_in, ffn1_out, i)
        q, kc, vc, kv, misc, z, xbc = _inproj(h, row(mix_norm), w_proj, i)
        cprep = _prep_compress(cmp_pos[i], cmp_w1[i], cmp_b1[i], cmp_w2[i], cmp_b2[i])
        kcmp, vcmp = _compress(kc, vc, cprep, batch, seq)
        o_attn = _nsa_attention(q, kcmp, vcmp, kv, misc, tb, biasc, ov, ex, batch, seq)
        o_ssm = _ssd(xbc, z, misc, ssd_prm, ssd_consts, i, batch, seq)
        h = _outproj(h, o_attn, o_ssm, nsa_gain, wo_attn, wo_ssm, i)
        h = _ffn(h, row(ffn2_norm), ffn2_in, ffn2_out, i)
        h = _ple(h, p2, row(ple_norm), wg, wp, fgain, i, final=(i == depth - 1))
    return h.reshape(batch, seq, d)
```

```python
import functools
import math

import numpy as np
import jax
import jax.numpy as jnp
from jax import lax
from jax.experimental import pallas as pl
from jax.experimental.pallas import tpu as pltpu

F32 = jnp.float32
BF16 = jnp.bfloat16

D_MODEL = 1024
DEPTH = 4
PLE_DIM = 256
D_FF = 2816
EPS = 1e-6
NEG_INF = -1e30
FORCE_SCORE = 1e4

NSA_HEADS = 8
NSA_KV_GROUPS = 2
NSA_REP = NSA_HEADS // NSA_KV_GROUPS
HEAD_DIM = 64
NSA_WIDTH = NSA_HEADS * HEAD_DIM
KV_WIDTH = NSA_KV_GROUPS * HEAD_DIM
CMP_BLOCK = 32
CMP_STRIDE = 16
CMP_HIDDEN = 256
SEL_BLOCK = 64
SEL_TOPK = 8
WINDOW = 512
REL_BUCKETS = 32
REL_MAX_DIST = 128

SSM_HEADS = 16
SSM_HEAD_DIM = 64
SSM_INNER = SSM_HEADS * SSM_HEAD_DIM
SSM_GROUPS = 2
SSM_STATE = 128
CONV_WIDTH = 4
SSD_CHUNK = 128
CONV_CH = SSM_INNER + 2 * SSM_GROUPS * SSM_STATE

LANES = 128
VMEM_LIMIT_BYTES = 48 * 1024 * 1024

FFN_TM = 1024
FFN_TF = 256
PROJ_TM = 512
Q_TILE = 128
K_TILE = 256
HEAD_ORDER = (0, 4, 1, 5, 2, 6, 3, 7)
GATE_LANE0 = 0
DT_LANE0 = 3 * NSA_HEADS
N_BIAS_TABLES = 6


def _dot(a, b):
    return jnp.dot(a, b, preferred_element_type=F32)


def _dot_nt(a, b):
    return lax.dot_general(a, b, (((1,), (1,)), ((), ())), preferred_element_type=F32)


def _split3(v):
    hi = v.astype(BF16)
    r = v - hi.astype(F32)
    mid = r.astype(BF16)
    lo = (r - mid.astype(F32)).astype(BF16)
    return hi, mid, lo


def _dot_f32_lhs(v, e):
    hi, mid, lo = _split3(v)
    return _dot(hi, e) + _dot(mid, e) + _dot(lo, e)


def _dot_f32_rhs(e, v):
    hi, mid, lo = _split3(v)
    return _dot(e, hi) + _dot(e, mid) + _dot(e, lo)


def _rms(x, g):
    ms = jnp.mean(x * x, axis=-1, keepdims=True)
    return x * lax.rsqrt(ms + EPS) * g


def _silu(x):
    return x * jax.nn.sigmoid(x)


def _cparams(*sem):
    return pltpu.CompilerParams(dimension_semantics=sem, vmem_limit_bytes=VMEM_LIMIT_BYTES)


def _ffn_kernel(x_ref, g_ref, wg_ref, wu_ref, wo_ref, o_ref, xn_ref, acc_ref):
    j = pl.program_id(1)

    @pl.when(j == 0)
    def _():
        xn_ref[...] = _rms(x_ref[...], g_ref[...]).astype(BF16)
        acc_ref[...] = jnp.zeros_like(acc_ref)

    xn = xn_ref[...]
    gate = _dot(xn, wg_ref[...])
    up = _dot(xn, wu_ref[...])
    h = (_silu(gate) * up).astype(BF16)
    acc_ref[...] += _dot(h, wo_ref[...])

    @pl.when(j == pl.num_programs(1) - 1)
    def _():
        o_ref[...] = x_ref[...] + 0.5 * acc_ref[...]


def _ffn(x, gain, w_in, w_out, layer):
    t, d = x.shape
    nf = D_FF // FFN_TF
    return pl.pallas_call(
        _ffn_kernel,
        grid=(t // FFN_TM, nf),
        in_specs=[
            pl.BlockSpec((FFN_TM, d), lambda i, j: (i, 0)),
            pl.BlockSpec((None, 1, d), lambda i, j: (layer, 0, 0)),
            pl.BlockSpec((None, d, FFN_TF), lambda i, j: (layer, 0, j)),
            pl.BlockSpec((None, d, FFN_TF), lambda i, j: (layer, 0, j + nf)),
            pl.BlockSpec((None, FFN_TF, d), lambda i, j: (layer, j, 0)),
        ],
        out_specs=pl.BlockSpec((FFN_TM, d), lambda i, j: (i, 0)),
        out_shape=jax.ShapeDtypeStruct((t, d), F32),
        scratch_shapes=[pltpu.VMEM((FFN_TM, d), BF16), pltpu.VMEM((FFN_TM, d), F32)],
        compiler_params=_cparams("parallel", "arbitrary"),
        name="ffn",
    )(x, gain, w_in, w_in, w_out)


_C_Q = (0, NSA_WIDTH)
_C_KC = (_C_Q[1], _C_Q[1] + KV_WIDTH)
_C_VC = (_C_KC[1], _C_KC[1] + KV_WIDTH)
_C_KV = (_C_VC[1], _C_VC[1] + 4 * KV_WIDTH)
_C_MISC = (_C_KV[1], _C_KV[1] + LANES)
_C_Z = (_C_MISC[1], _C_MISC[1] + SSM_INNER)
_C_XBC = (_C_Z[1], _C_Z[1] + CONV_CH)
PROJ_COLS = _C_XBC[1]


def _inproj_kernel(x_ref, g_ref, w_ref, q_ref, kc_ref, vc_ref, kv_ref, misc_ref, z_ref, xbc_ref):
    xn = _rms(x_ref[...], g_ref[...]).astype(BF16)
    q_ref[...] = _dot(xn, w_ref[:, _C_Q[0]:_C_Q[1]]).astype(BF16)
    kc_ref[...] = _dot(xn, w_ref[:, _C_KC[0]:_C_KC[1]])
    vc_ref[...] = _dot(xn, w_ref[:, _C_VC[0]:_C_VC[1]])
    kv_ref[...] = _dot(xn, w_ref[:, _C_KV[0]:_C_KV[1]]).astype(BF16)
    misc_ref[...] = _dot(xn, w_ref[:, _C_MISC[0]:_C_MISC[1]])
    z_ref[...] = _dot(xn, w_ref[:, _C_Z[0]:_C_Z[1]])
    xbc_ref[...] = _dot(xn, w_ref[:, _C_XBC[0]:_C_XBC[1]])


def _inproj(x, gain, w, layer):
    t, d = x.shape
    tm = PROJ_TM
    widths = (NSA_WIDTH, KV_WIDTH, KV_WIDTH, 4 * KV_WIDTH, LANES, SSM_INNER, CONV_CH)
    dtypes = (BF16, F32, F32, BF16, F32, F32, F32)
    return pl.pallas_call(
        _inproj_kernel,
        grid=(t // tm,),
        in_specs=[
            pl.BlockSpec((tm, d), lambda i: (i, 0)),
            pl.BlockSpec((None, 1, d), lambda i: (layer, 0, 0)),
            pl.BlockSpec((None, d, PROJ_COLS), lambda i: (layer, 0, 0)),
        ],
        out_specs=[pl.BlockSpec((tm, n), lambda i: (i, 0)) for n in widths],
        out_shape=[jax.ShapeDtypeStruct((t, n), dt) for n, dt in zip(widths, dtypes)],
        compiler_params=_cparams("parallel"),
        name="inproj",
    )(x, gain, w)


def _prep_inproj_weight(w_mix_in):
    offs = np.cumsum((0, NSA_WIDTH) + (KV_WIDTH,) * 6 + (3 * NSA_HEADS, SSM_INNER, CONV_CH, SSM_HEADS))
    sl = lambda a, b: w_mix_in[:, :, a:b]
    q = sl(offs[0], offs[1])
    q = jnp.concatenate([q[:, :, h * HEAD_DIM:(h + 1) * HEAD_DIM] for h in HEAD_ORDER], axis=-1)
    kc, vc = sl(offs[1], offs[2]), sl(offs[2], offs[3])
    kv = sl(offs[3], offs[7])
    gates = sl(offs[7], offs[8])
    z = sl(offs[8], offs[9])
    xbc = sl(offs[9], offs[10])
    dt = sl(offs[10], offs[11])
    pad = jnp.zeros(w_mix_in.shape[:2] + (LANES - 3 * NSA_HEADS - SSM_HEADS,), w_mix_in.dtype)
    misc = jnp.concatenate([gates, dt, pad], axis=-1)
    return jnp.concatenate([q, kc, vc, kv, misc, z, xbc], axis=-1).astype(BF16)


def _compress_kernel(kc_ref, vc_ref, pos_ref, w1a_ref, w1b_ref, b1_ref, w2_ref, b2_ref,
                     kcmp_ref, vcmpt_ref):
    nrow = kc_ref.shape[0]
    for which, src in enumerate((kc_ref, vc_ref)):
        r = src[...]
        ra = (r + pos_ref[which, 0]).astype(BF16)
        rb = (r + pos_ref[which, 1]).astype(BF16)
        ha = _dot(ra, w1a_ref[which])
        hb = _dot(rb, w1b_ref[which])
        h = ha + pltpu.roll(hb, nrow - 1, 0) + b1_ref[which]
        out = _dot(_silu(h).astype(BF16), w2_ref[which]) + b2_ref[which]
        if which == 0:
            kcmp_ref[...] = out.astype(BF16)
        else:
            vcmpt_ref[...] = out.T.astype(BF16)


def _compress(kc, vc, prep, batch, seq):
    nrow = seq // CMP_STRIDE
    wide = CMP_STRIDE * KV_WIDTH
    kcr = kc.reshape(batch, nrow, wide)
    vcr = vc.reshape(batch, nrow, wide)
    full = lambda a: pl.BlockSpec(a.shape, lambda b: (0,) * a.ndim)
    consts = (prep["pos"], prep["w1a"], prep["w1b"], prep["b1"], prep["w2"], prep["b2"])
    return pl.pallas_call(
        _compress_kernel,
        grid=(batch,),
        in_specs=[pl.BlockSpec((None, nrow, wide), lambda b: (b, 0, 0))] * 2 + [full(a) for a in consts],
        out_specs=[pl.BlockSpec((None, nrow, KV_WIDTH), lambda b: (b, 0, 0)),
                   pl.BlockSpec((None, KV_WIDTH, nrow), lambda b: (b, 0, 0))],
        out_shape=[jax.ShapeDtypeStruct((batch, nrow, KV_WIDTH), BF16),
                   jax.ShapeDtypeStruct((batch, KV_WIDTH, nrow), BF16)],
        compiler_params=_cparams("parallel"),
        name="nsa_compress",
    )(kcr, vcr, *consts)


def _prep_compress(cmp_pos, cmp_w1, cmp_b1, cmp_w2, cmp_b2):
    g = NSA_KV_GROUPS
    half = CMP_BLOCK // 2
    eye = jnp.eye(g, dtype=F32)
    w1 = cmp_w1.reshape(2, CMP_BLOCK, HEAD_DIM, CMP_HIDDEN)

    def expand(w):
        e = jnp.einsum("wldh,pg->wlpdgh", w, eye)
        return e.reshape(2, half * g * HEAD_DIM, g * CMP_HIDDEN).astype(BF16)

    def pos_rows(p):
        return jnp.broadcast_to(p[:, :, None, :], (2, half, g, HEAD_DIM)).reshape(2, 1, half * g * HEAD_DIM)

    w2 = jnp.einsum("whd,gp->wghpd", cmp_w2, eye).reshape(2, g * CMP_HIDDEN, g * HEAD_DIM).astype(BF16)
    return {
        "pos": jnp.stack([pos_rows(cmp_pos[:, :half]), pos_rows(cmp_pos[:, half:])], axis=1),
        "w1a": expand(w1[:, :half]),
        "w1b": expand(w1[:, half:]),
        "b1": jnp.tile(cmp_b1, (1, g))[:, None, :],
        "w2": w2,
        "b2": jnp.tile(cmp_b2, (1, g))[:, None, :],
    }


def _t5_bucket_np(dist):
    n = np.maximum(dist, 0)
    exact = REL_BUCKETS // 2
    nf = np.maximum(n, exact).astype(np.float64)
    large = exact + (np.log(nf / exact) / math.log(REL_MAX_DIST / exact) * (REL_BUCKETS - exact)).astype(np.int64)
    return np.where(n < exact, n, np.minimum(large, REL_BUCKETS - 1)).astype(np.int32)


MASKED_BUCKET = REL_BUCKETS


def _bucket_maps(seq):
    j = np.arange(K_TILE)[:, None]
    i = np.arange(Q_TILE)[None, :]
    tiles = []
    for delta in range(N_BIAS_TABLES):
        d = Q_TILE * delta + i - j
        valid = (d >= 0) & ((d < WINDOW) if delta >= 4 else True)
        tiles.append(np.where(valid, _t5_bucket_np(d), MASKED_BUCKET))
    c = np.arange(seq // CMP_STRIDE)[:, None]
    t = np.arange(seq)[None, :]
    dc = t - (c * CMP_STRIDE + CMP_BLOCK - 1)
    cmp_map = np.where(dc >= 0, _t5_bucket_np(dc), MASKED_BUCKET)
    return np.stack(tiles).astype(np.int32), cmp_map.astype(np.int32)


def _tables_kernel(tab_ref, bkt_ref, bkc_ref, tb_ref, bc_ref):
    h = pl.program_id(0)
    for src, dst in ((bkt_ref, tb_ref), (bkc_ref, bc_ref)):
        bk = src[...]
        out = jnp.zeros(bk.shape, F32)
        for b in range(REL_BUCKETS + 1):
            out = jnp.where(bk == b, tab_ref[b, h], out)
        dst[...] = out


def _nsa_tables(rel_table, seq):
    bkt, bkc = _bucket_maps(seq)
    tab = jnp.concatenate([rel_table[:, np.asarray(HEAD_ORDER)],
                           jnp.full((1, NSA_HEADS), NEG_INF, F32)], axis=0)
    nc = seq // CMP_STRIDE
    return pl.pallas_call(
        _tables_kernel,
        grid=(NSA_HEADS,),
        in_specs=[
            pl.BlockSpec(memory_space=pltpu.SMEM),
            pl.BlockSpec(bkt.shape, lambda h: (0, 0, 0)),
            pl.BlockSpec(bkc.shape, lambda h: (0, 0)),
        ],
        out_specs=[pl.BlockSpec((N_BIAS_TABLES, None, K_TILE, Q_TILE), lambda h: (0, h, 0, 0)),
                   pl.BlockSpec((None, nc, seq), lambda h: (h, 0, 0))],
        out_shape=[jax.ShapeDtypeStruct((N_BIAS_TABLES, NSA_HEADS, K_TILE, Q_TILE), F32),
                   jax.ShapeDtypeStruct((NSA_HEADS, nc, seq), F32)],
        compiler_params=_cparams("parallel"),
        name="nsa_bias_tables",
    )(tab, jnp.asarray(bkt), jnp.asarray(bkc))


def _nsa_consts(seq):
    nc = seq // CMP_STRIDE
    nb = seq // SEL_BLOCK
    c = np.arange(nc)
    blk = np.arange(nb)
    c_lo, c_hi = c * CMP_STRIDE, c * CMP_STRIDE + CMP_BLOCK - 1
    s_lo, s_hi = blk * SEL_BLOCK, blk * SEL_BLOCK + SEL_BLOCK - 1
    ovt = (c_lo[None, :] <= s_hi[:, None]) & (c_hi[None, :] >= s_lo[:, None])
    ovt[:, nc - 1] = False
    ext = (np.arange(seq)[:, None] // SEL_BLOCK) == blk[None, :]
    return jnp.asarray(ovt, BF16), jnp.asarray(ext, BF16)


def _nsa_kernel(q_ref, kcmp_ref, vcmpt_ref, ks_ref, vs_ref, kw_ref, vw_ref, misc_ref, biasc_ref, tb_ref,
                ovt_ref, ext_ref, o_ref, qpt_ref, vst_ref, vwt_ref, madd_ref, m_ref, l_ref, acc_ref):
    n = pl.program_id(1)
    nkt = ks_ref.shape[0] // K_TILE
    nsel_blocks = ovt_ref.shape[0]
    nhb = NSA_HEADS
    cols = lambda hb: slice(hb * Q_TILE, (hb + 1) * Q_TILE)
    frow = lax.broadcasted_iota(jnp.int32, (KV_WIDTH, Q_TILE), 0)
    low = frow < HEAD_DIM

    @pl.when(n == 0)
    def _():
        for t2 in range(nkt):
            rows = slice(t2 * K_TILE, (t2 + 1) * K_TILE)
            vst_ref[t2] = vs_ref[rows, :].astype(F32).T.astype(BF16)
            vwt_ref[t2] = vw_ref[rows, :].astype(F32).T.astype(BF16)

    scale = HEAD_DIM ** -0.5
    for j in range(nhb // 2):
        slab = (q_ref[:, j * LANES:(j + 1) * LANES].astype(F32).T * scale).astype(BF16)
        zero = jnp.zeros_like(slab)
        qpt_ref[:, cols(2 * j)] = jnp.where(low, slab, zero)
        qpt_ref[:, cols(2 * j + 1)] = jnp.where(low, zero, slab)
    qpt = qpt_ref[...]

    tq = n * Q_TILE + lax.broadcasted_iota(jnp.int32, (1, Q_TILE), 1)
    has_block = (tq >= CMP_BLOCK - 1).astype(F32)
    sc = _dot(kcmp_ref[...], qpt)
    psum = [None, None]
    pcs = []
    for hb in range(nhb):
        s = sc[:, cols(hb)] + biasc_ref[hb]
        m = jnp.max(s, axis=0, keepdims=True)
        e = jnp.exp(s - m)
        p = e / jnp.sum(e, axis=0, keepdims=True) * has_block
        pcs.append(p.astype(BF16))
        g = hb % 2
        psum[g] = p if psum[g] is None else psum[g] + p
    o_cmp = _dot(vcmpt_ref[...], jnp.concatenate(pcs, axis=1))

    blk = lax.broadcasted_iota(jnp.int32, (nsel_blocks, Q_TILE), 0)
    blk_f = blk.astype(F32)
    cur = tq // SEL_BLOCK
    forced = ((blk == 0) | (blk == cur) | (blk == cur - 1)).astype(F32)
    for g in range(NSA_KV_GROUPS):
        imp = _dot_f32_rhs(ovt_ref[...], psum[g])
        score = jnp.where(blk <= cur, imp + FORCE_SCORE * forced, -FORCE_SCORE)
        sel = jnp.zeros(score.shape, F32)
        for _ in range(SEL_TOPK):
            mx = jnp.max(score, axis=0, keepdims=True)
            first = jnp.min(jnp.where(score == mx, blk_f, float(nsel_blocks)), axis=0, keepdims=True)
            hit = blk_f == first
            sel = jnp.where(hit, 1.0, sel)
            score = jnp.where(hit, -jnp.inf, score)
        e = _dot(ext_ref[...], sel.astype(BF16))
        madd_ref[g] = (e - 1.0) * (-NEG_INF)

    def attend(k_ref, vt_ref, t2_lo, t2_hi, selected):
        m_ref[...] = jnp.full_like(m_ref, -jnp.inf)
        l_ref[...] = jnp.zeros_like(l_ref)
        acc_ref[...] = jnp.zeros_like(acc_ref)

        def body(t2, carry):
            start = pl.multiple_of(t2 * K_TILE, K_TILE)
            k = k_ref[pl.ds(start, K_TILE), :]
            s_all = _dot(k, qpt)
            delta = n - 2 * t2
            ti = jnp.minimum(delta, 3) if selected else delta
            ps, alphas = [], []
            for hb in range(nhb):
                s = s_all[:, cols(hb)] + tb_ref[ti, hb]
                if selected:
                    s = s + madd_ref[hb % 2, pl.ds(start, K_TILE), :]
                m_prev = m_ref[:, cols(hb)]
                m_new = jnp.maximum(m_prev, jnp.max(s, axis=0, keepdims=True))
                alpha = jnp.exp(m_prev - m_new)
                p = jnp.exp(s - m_new)
                l_ref[:, cols(hb)] = alpha * l_ref[:, cols(hb)] + jnp.sum(p, axis=0, keepdims=True)
                m_ref[:, cols(hb)] = m_new
                ps.append(p.astype(BF16))
                alphas.append(alpha)
            pv = _dot(vt_ref[t2], jnp.concatenate(ps, axis=1))
            acc_ref[...] = acc_ref[...] * jnp.concatenate(alphas, axis=1) + pv
            return carry

        lax.fori_loop(t2_lo, t2_hi, body, 0)
        return acc_ref[...] / l_ref[...]

    t2_diag = n // 2
    o_sel = attend(ks_ref, vst_ref, 0, t2_diag + 1, True)
    o_win = attend(kw_ref, vwt_ref, jnp.maximum(t2_diag - 2, 0), t2_diag + 1, False)

    gates = jax.nn.sigmoid(misc_ref[...]).T

    def gate(branch, hb):
        col = GATE_LANE0 + branch * NSA_HEADS + HEAD_ORDER[hb]
        return gates[col:col + 1, :]

    for j in range(nhb // 2):
        outs = []
        for hb in (2 * j, 2 * j + 1):
            outs.append(gate(0, hb) * o_cmp[:, cols(hb)] + gate(1, hb) * o_sel[:, cols(hb)]
                        + gate(2, hb) * o_win[:, cols(hb)])
        o_ref[:, j * LANES:(j + 1) * LANES] = jnp.where(low, outs[0], outs[1]).T


def _nsa_attention(q, kcmp, vcmpt, kv, misc, tb, biasc, ovt, ext, batch, seq):
    nq = seq // Q_TILE
    nc = seq // CMP_STRIDE
    nkt = seq // K_TILE
    cols_all = NSA_HEADS * Q_TILE
    kvspec = lambda c: pl.BlockSpec((seq, KV_WIDTH), lambda b, n: (b, c))
    return pl.pallas_call(
        _nsa_kernel,
        grid=(batch, nq),
        in_specs=[
            pl.BlockSpec((Q_TILE, NSA_WIDTH), lambda b, n: (b * nq + n, 0)),
            pl.BlockSpec((None, nc, KV_WIDTH), lambda b, n: (b, 0, 0)),
            pl.BlockSpec((None, KV_WIDTH, nc), lambda b, n: (b, 0, 0)),
            kvspec(0), kvspec(1), kvspec(2), kvspec(3),
            pl.BlockSpec((Q_TILE, LANES), lambda b, n: (b * nq + n, 0)),
            pl.BlockSpec((NSA_HEADS, nc, Q_TILE), lambda b, n: (0, 0, n)),
            pl.BlockSpec(tb.shape, lambda b, n: (0, 0, 0, 0)),
            pl.BlockSpec(ovt.shape, lambda b, n: (0, 0)),
            pl.BlockSpec(ext.shape, lambda b, n: (0, 0)),
        ],
        out_specs=pl.BlockSpec((Q_TILE, NSA_WIDTH), lambda b, n: (b * nq + n, 0)),
        out_shape=jax.ShapeDtypeStruct((batch * seq, NSA_WIDTH), F32),
        scratch_shapes=[
            pltpu.VMEM((KV_WIDTH, cols_all), BF16),
            pltpu.VMEM((nkt, KV_WIDTH, K_TILE), BF16),
            pltpu.VMEM((nkt, KV_WIDTH, K_TILE), BF16),
            pltpu.VMEM((NSA_KV_GROUPS, seq, Q_TILE), F32),
            pltpu.VMEM((1, cols_all), F32),
            pltpu.VMEM((1, cols_all), F32),
            pltpu.VMEM((KV_WIDTH, cols_all), F32),
        ],
        compiler_params=_cparams("arbitrary", "arbitrary"),
        name="nsa_attention",
    )(q, kcmp, vcmpt, kv, kv, kv, kv, misc, biasc, tb, ovt, ext)


def _softplus(x):
    return jnp.maximum(x, 0.0) + jnp.log1p(jnp.exp(-jnp.abs(x)))


def _ssd_kernel(xbc_ref, z_ref, misc_ref, convw_ref, convb_ref, dtb_ref, alog_ref, dskip_ref, gain_ref,
                tri_ref, e1_ref, e2_ref, o_ref, prev_ref, h_ref):
    c = pl.program_id(1)
    L = SSD_CHUNK
    gw = SSM_INNER // SSM_GROUPS
    hpg = SSM_HEADS // SSM_GROUPS

    @pl.when(c == 0)
    def _():
        prev_ref[...] = jnp.zeros_like(prev_ref)
        h_ref[...] = jnp.zeros_like(h_ref)

    x = xbc_ref[...]
    xp = prev_ref[...]
    row = lax.broadcasted_iota(jnp.int32, (L, 1), 0)
    acc = convb_ref[...] + x * convw_ref[CONV_WIDTH - 1:CONV_WIDTH, :]
    for k in range(1, CONV_WIDTH):
        xk = jnp.where(row >= k, pltpu.roll(x, k, 0), pltpu.roll(xp, k, 0))
        acc = acc + xk * convw_ref[CONV_WIDTH - 1 - k:CONV_WIDTH - k, :]
    prev_ref[...] = x
    xa = _silu(acc)
    xs = xa[:, :SSM_INNER]
    bm = xa[:, SSM_INNER:SSM_INNER + SSM_GROUPS * SSM_STATE]
    cm = xa[:, SSM_INNER + SSM_GROUPS * SSM_STATE:]

    dt = _softplus(misc_ref[...] + dtb_ref[...])
    da = dt * (-jnp.exp(alog_ref[...]))
    cs = _dot_f32_rhs(tri_ref[...], da)
    cs_t = cs.T
    dtx = _dot_f32_lhs(dt, e1_ref[...])
    csx = _dot_f32_lhs(cs, e1_ref[...])
    cs2 = _dot_f32_lhs(cs, e2_ref[...])
    cs_last = csx[L - 1:L, :]
    xdt = xs * dtx
    xdt_b = xdt.astype(BF16)
    xw_b = (xdt * jnp.exp(cs_last - csx)).astype(BF16)
    ecs = jnp.exp(csx)
    state_decay = jnp.exp(cs_last)

    li = lax.broadcasted_iota(jnp.int32, (L, L), 0)
    si = lax.broadcasted_iota(jnp.int32, (L, L), 1)
    causal = li >= si
    low = si < SSM_HEAD_DIM

    ys = []
    for g in range(SSM_GROUPS):
        bg = bm[:, g * SSM_STATE:(g + 1) * SSM_STATE]
        cg = cm[:, g * SSM_STATE:(g + 1) * SSM_STATE].astype(BF16)
        cb = _dot_nt(cg, bg.astype(BF16))
        h_g = h_ref[:, g * gw:(g + 1) * gw]
        y_off = _dot(cg, h_g.astype(BF16)) * ecs[:, g * gw:(g + 1) * gw]
        for pr in range(hpg // 2):
            h0 = g * hpg + 2 * pr
            gs = []
            for hh in (h0, h0 + 1):
                col = cs2[:, hh * LANES:(hh + 1) * LANES]
                rowv = cs_t[DT_LANE0 + hh:DT_LANE0 + hh + 1, :]
                dec = jnp.exp(jnp.where(causal, col - rowv, NEG_INF))
                gs.append((cb * dec).astype(BF16))
            ch = slice(h0 * SSM_HEAD_DIM, (h0 + 2) * SSM_HEAD_DIM)
            xpair = xdt_b[:, ch]
            zero = jnp.zeros_like(xpair)
            rhs = jnp.concatenate([jnp.where(low, xpair, zero), jnp.where(low, zero, xpair)], axis=0)
            y_diag = _dot(jnp.concatenate(gs, axis=1), rhs)
            off = slice(2 * pr * SSM_HEAD_DIM, (2 * pr + 2) * SSM_HEAD_DIM)
            ys.append(y_diag + y_off[:, off] + xs[:, ch] * dskip_ref[:, ch])
        st = _dot(bg.T.astype(BF16), xw_b[:, g * gw:(g + 1) * gw])
        h_ref[:, g * gw:(g + 1) * gw] = h_g * state_decay[:, g * gw:(g + 1) * gw] + st

    y = jnp.concatenate(ys, axis=1) * _silu(z_ref[...])
    outs = []
    for g in range(SSM_GROUPS):
        outs.append(_rms(y[:, g * gw:(g + 1) * gw], gain_ref[:, g * gw:(g + 1) * gw]))
    o_ref[...] = jnp.concatenate(outs, axis=1).astype(BF16)


def _ssd_consts():
    lane = np.arange(LANES)
    tri = (np.arange(SSD_CHUNK)[:, None] >= np.arange(SSD_CHUNK)[None, :])
    head1 = np.arange(SSM_INNER) // SSM_HEAD_DIM
    e1 = (lane[:, None] - DT_LANE0) == head1[None, :]
    head2 = np.arange(SSM_HEADS * LANES) // LANES
    e2 = (lane[:, None] - DT_LANE0) == head2[None, :]
    return jnp.asarray(tri, BF16), jnp.asarray(e1, BF16), jnp.asarray(e2, BF16)


def _ssd(xbc, z, misc, prm, consts, layer, batch, seq):
    nch = seq // SSD_CHUNK
    tok = lambda n: pl.BlockSpec((SSD_CHUNK, n), lambda b, c: (b * nch + c, 0))
    lay = lambda a: pl.BlockSpec((None,) + a.shape[1:], lambda b, c: (layer,) + (0,) * (a.ndim - 1))
    full = lambda a: pl.BlockSpec(a.shape, lambda b, c: (0,) * a.ndim)
    params = (prm["conv_w"], prm["conv_b"], prm["dt_bias"], prm["a_log"], prm["d_skip"], prm["ssm_gain"])
    return pl.pallas_call(
        _ssd_kernel,
        grid=(batch, nch),
        in_specs=[tok(CONV_CH), tok(SSM_INNER), tok(LANES)] + [lay(a) for a in params] + [full(a) for a in consts],
        out_specs=tok(SSM_INNER),
        out_shape=jax.ShapeDtypeStruct((batch * seq, SSM_INNER), BF16),
        scratch_shapes=[pltpu.VMEM((SSD_CHUNK, CONV_CH), F32), pltpu.VMEM((SSM_STATE, SSM_INNER), F32)],
        compiler_params=_cparams("parallel", "arbitrary"),
        name="ssd",
    )(xbc, z, misc, *params, *consts)


def _prep_ssd_params(conv_w, conv_b, dt_bias, a_log, d_skip, ssm_out_norm):
    nl = conv_w.shape[0]

    def dt_lanes(v):
        out = jnp.zeros((nl, 1, LANES), F32)
        return out.at[:, 0, DT_LANE0:DT_LANE0 + SSM_HEADS].set(v)

    return {
        "conv_w": conv_w,
        "conv_b": conv_b[:, None, :],
        "dt_bias": dt_lanes(dt_bias),
        "a_log": dt_lanes(a_log),
        "d_skip": jnp.repeat(d_skip, SSM_HEAD_DIM, axis=-1)[:, None, :],
        "ssm_gain": ssm_out_norm[:, None, :],
    }


def _outproj_kernel(x_ref, oa_ref, os_ref, g_ref, wa_ref, ws_ref, o_ref):
    an = _rms(oa_ref[...], g_ref[...]).astype(BF16)
    o_ref[...] = x_ref[...] + _dot(an, wa_ref[...]) + _dot(os_ref[...], ws_ref[...])


def _outproj(x, o_attn, o_ssm, gain, wa, ws, layer):
    t, d = x.shape
    tm = PROJ_TM
    return pl.pallas_call(
        _outproj_kernel,
        grid=(t // tm,),
        in_specs=[
            pl.BlockSpec((tm, d), lambda i: (i, 0)),
            pl.BlockSpec((tm, NSA_WIDTH), lambda i: (i, 0)),
            pl.BlockSpec((tm, SSM_INNER), lambda i: (i, 0)),
            pl.BlockSpec((None, 1, NSA_WIDTH), lambda i: (layer, 0, 0)),
            pl.BlockSpec((None, NSA_WIDTH, d), lambda i: (layer, 0, 0)),
            pl.BlockSpec((None, SSM_INNER, d), lambda i: (layer, 0, 0)),
        ],
        out_specs=pl.BlockSpec((tm, d), lambda i: (i, 0)),
        out_shape=jax.ShapeDtypeStruct((t, d), F32),
        compiler_params=_cparams("parallel"),
        name="outproj",
    )(x, o_attn, o_ssm, gain, wa, ws)


def _perm_heads(a, axis):
    idx = np.concatenate([np.arange(h * HEAD_DIM, (h + 1) * HEAD_DIM) for h in HEAD_ORDER])
    return jnp.take(a, jnp.asarray(idx), axis=axis)


def _ple_kernel(x_ref, p_ref, g_ref, wg_ref, wp_ref, fg_ref, o_ref, *, final):
    x = x_ref[...]
    xn = _rms(x, g_ref[...]).astype(BF16)
    gate = jax.nn.sigmoid(_dot(xn, wg_ref[...]))
    y = x + gate * _dot(p_ref[...].astype(BF16), wp_ref[...])
    if final:
        y = _rms(y, fg_ref[...])
    o_ref[...] = y


def _ple(x, p, gain, wg, wp, final_gain, layer, final):
    t, d = x.shape
    tm = PROJ_TM
    return pl.pallas_call(
        functools.partial(_ple_kernel, final=final),
        grid=(t // tm,),
        in_specs=[
            pl.BlockSpec((tm, d), lambda i: (i, 0)),
            pl.BlockSpec((None, tm, PLE_DIM), lambda i: (layer, i, 0)),
            pl.BlockSpec((None, 1, d), lambda i: (layer, 0, 0)),
            pl.BlockSpec((None, d, d), lambda i: (layer, 0, 0)),
            pl.BlockSpec((None, PLE_DIM, d), lambda i: (layer, 0, 0)),
            pl.BlockSpec((1, d), lambda i: (0, 0)),
        ],
        out_specs=pl.BlockSpec((tm, d), lambda i: (i, 0)),
        out_shape=jax.ShapeDtypeStruct((t, d), F32),
        compiler_params=_cparams("parallel"),
        name="ple",
    )(x, p, gain, wg, wp, final_gain)


def kernel(x, p, ffn1_norm, ffn1_w_in, ffn1_w_out, mix_norm, w_mix_in, cmp_pos, cmp_w1, cmp_b1, cmp_w2,
           cmp_b2, rel_table, nsa_out_norm, conv_w, conv_b, dt_bias, a_log, d_skip, ssm_out_norm, w_mix_out,
           ffn2_norm, ffn2_w_in, ffn2_w_out, ple_norm, ple_gate_w, ple_proj_w, final_norm):
    batch, seq, d = x.shape
    depth = p.shape[0]
    t = batch * seq
    bf = lambda a: a.astype(BF16)
    row = lambda a: a[:, None, :]

    ffn1_in, ffn1_out, ffn2_in, ffn2_out = bf(ffn1_w_in), bf(ffn1_w_out), bf(ffn2_w_in), bf(ffn2_w_out)
    w_proj = _prep_inproj_weight(w_mix_in)
    wo_attn = bf(_perm_heads(w_mix_out[:, :NSA_WIDTH], axis=1))
    wo_ssm = bf(w_mix_out[:, NSA_WIDTH:])
    nsa_gain = row(_perm_heads(nsa_out_norm, axis=1))
    ssd_prm = _prep_ssd_params(conv_w, conv_b, dt_bias, a_log, d_skip, ssm_out_norm)
    ssd_consts = _ssd_consts()
    tb, biasc = _nsa_tables(rel_table, seq)
    ov, ex = _nsa_consts(seq)
    wg, wp = bf(ple_gate_w), bf(ple_proj_w)
    p2 = p.reshape(depth, t, PLE_DIM)
    fgain = final_norm[None, :]

    h = x.reshape(t, d)
    for i in range(depth):
        h = _ffn(h, row(ffn1_norm), ffn1_in, ffn1_out, i)
        q, kc, vc, kv, misc, z, xbc = _inproj(h, row(mix_norm), w_proj, i)
        cprep = _prep_compress(cmp_pos[i], cmp_w1[i], cmp_b1[i], cmp_w2[i], cmp_b2[i])
        kcmp, vcmp = _compress(kc, vc, cprep, batch, seq)
        o_attn = _nsa_attention(q, kcmp, vcmp, kv, misc, tb, biasc, ov, ex, batch, seq)
        o_ssm = _ssd(xbc, z, misc, ssd_prm, ssd_consts, i, batch, seq)
        h = _outproj(h, o_attn, o_ssm, nsa_gain, wo_attn, wo_ssm, i)
        h = _ffn(h, row(ffn2_norm), ffn2_in, ffn2_out, i)
        h = _ple(h, p2, row(ple_norm), wg, wp, fgain, i, final=(i == depth - 1))
    return h.reshape(batch, seq, d)
```

```python
import functools
import math

import numpy as np
import jax
import jax.numpy as jnp
from jax import lax
from jax.experimental import pallas as pl
from jax.experimental.pallas import tpu as pltpu

F32 = jnp.float32
BF16 = jnp.bfloat16

D_MODEL = 1024
DEPTH = 4
PLE_DIM = 256
D_FF = 2816
EPS = 1e-6
NEG_INF = -1e30
FORCE_SCORE = 1e4

NSA_HEADS = 8
NSA_KV_GROUPS = 2
NSA_REP = NSA_HEADS // NSA_KV_GROUPS
HEAD_DIM = 64
NSA_WIDTH = NSA_HEADS * HEAD_DIM
KV_WIDTH = NSA_KV_GROUPS * HEAD_DIM
CMP_BLOCK = 32
CMP_STRIDE = 16
CMP_HIDDEN = 256
SEL_BLOCK = 64
SEL_TOPK = 8
WINDOW = 512
REL_BUCKETS = 32
REL_MAX_DIST = 128

SSM_HEADS = 16
SSM_HEAD_DIM = 64
SSM_INNER = SSM_HEADS * SSM_HEAD_DIM
SSM_GROUPS = 2
SSM_STATE = 128
CONV_WIDTH = 4
SSD_CHUNK = 128
CONV_CH = SSM_INNER + 2 * SSM_GROUPS * SSM_STATE

LANES = 128
VMEM_LIMIT_BYTES = 48 * 1024 * 1024

FFN_TM = 1024
FFN_TF = 256
PROJ_TM = 512
Q_TILE = 128
K_TILE = 256
HEAD_ORDER = (0, 4, 1, 5, 2, 6, 3, 7)
GATE_LANE0 = 0
DT_LANE0 = 3 * NSA_HEADS
N_BIAS_TABLES = 6
LOG2E = math.log2(math.e)
BF16_SUBLANES = 16
CONV_TAIL = 8


def _dot(a, b):
    return jnp.dot(a, b, preferred_element_type=F32)


def _dot_nt(a, b):
    return lax.dot_general(a, b, (((1,), (1,)), ((), ())), preferred_element_type=F32)


def _split3(v):
    hi = v.astype(BF16)
    r = v - hi.astype(F32)
    mid = r.astype(BF16)
    lo = (r - mid.astype(F32)).astype(BF16)
    return hi, mid, lo


def _dot_f32x2_lhs(v, e):
    hi = v.astype(BF16)
    lo = (v - hi.astype(F32)).astype(BF16)
    return _dot(hi, e) + _dot(lo, e)


def _dot_f32_rhs(e, v):
    hi, mid, lo = _split3(v)
    return _dot(e, hi) + _dot(e, mid) + _dot(e, lo)


def _rms(x, g):
    ms = jnp.mean(x * x, axis=-1, keepdims=True)
    return x * lax.rsqrt(ms + EPS) * g


def _silu(x):
    return x * jax.nn.sigmoid(x)


def _cparams(*sem):
    return pltpu.CompilerParams(dimension_semantics=sem, vmem_limit_bytes=VMEM_LIMIT_BYTES)


def _ffn_kernel(x_ref, g_ref, wg_ref, wu_ref, wo_ref, o_ref, xn_ref, acc_ref):
    j = pl.program_id(1)

    @pl.when(j == 0)
    def _():
        xn_ref[...] = _rms(x_ref[...], g_ref[...]).astype(BF16)
        acc_ref[...] = jnp.zeros_like(acc_ref)

    xn = xn_ref[...]
    gate = _dot(xn, wg_ref[...])
    up = _dot(xn, wu_ref[...])
    h = (_silu(gate) * up).astype(BF16)
    acc_ref[...] += _dot(h, wo_ref[...])

    @pl.when(j == pl.num_programs(1) - 1)
    def _():
        o_ref[...] = x_ref[...] + 0.5 * acc_ref[...]


def _ffn(x, gain, w_in, w_out, layer):
    t, d = x.shape
    nf = D_FF // FFN_TF
    return pl.pallas_call(
        _ffn_kernel,
        grid=(t // FFN_TM, nf),
        in_specs=[
            pl.BlockSpec((FFN_TM, d), lambda i, j: (i, 0)),
            pl.BlockSpec((None, 1, d), lambda i, j: (layer, 0, 0)),
            pl.BlockSpec((None, d, FFN_TF), lambda i, j: (layer, 0, j)),
            pl.BlockSpec((None, d, FFN_TF), lambda i, j: (layer, 0, j + nf)),
            pl.BlockSpec((None, FFN_TF, d), lambda i, j: (layer, j, 0)),
        ],
        out_specs=pl.BlockSpec((FFN_TM, d), lambda i, j: (i, 0)),
        out_shape=jax.ShapeDtypeStruct((t, d), F32),
        scratch_shapes=[pltpu.VMEM((FFN_TM, d), BF16), pltpu.VMEM((FFN_TM, d), F32)],
        compiler_params=_cparams("parallel", "arbitrary"),
        name="ffn",
    )(x, gain, w_in, w_in, w_out)


_C_Q = (0, NSA_WIDTH)
_C_KC = (_C_Q[1], _C_Q[1] + KV_WIDTH)
_C_VC = (_C_KC[1], _C_KC[1] + KV_WIDTH)
_C_KV = (_C_VC[1], _C_VC[1] + 4 * KV_WIDTH)
_C_MISC = (_C_KV[1], _C_KV[1] + LANES)
_C_Z = (_C_MISC[1], _C_MISC[1] + SSM_INNER)
_C_XBC = (_C_Z[1], _C_Z[1] + CONV_CH)
PROJ_COLS = _C_XBC[1]


def _inproj_kernel(x_ref, g_ref, w_ref, q_ref, kc_ref, vc_ref, kv_ref, misc_ref, z_ref, xbc_ref):
    xn = _rms(x_ref[...], g_ref[...]).astype(BF16)
    q_ref[...] = _dot(xn, w_ref[:, _C_Q[0]:_C_Q[1]])
    kc_ref[...] = _dot(xn, w_ref[:, _C_KC[0]:_C_KC[1]])
    vc_ref[...] = _dot(xn, w_ref[:, _C_VC[0]:_C_VC[1]])
    kv_ref[...] = _dot(xn, w_ref[:, _C_KV[0]:_C_KV[1]]).astype(BF16)
    misc_ref[...] = _dot(xn, w_ref[:, _C_MISC[0]:_C_MISC[1]])
    z_ref[...] = _dot(xn, w_ref[:, _C_Z[0]:_C_Z[1]])
    xbc_ref[...] = _dot(xn, w_ref[:, _C_XBC[0]:_C_XBC[1]])


def _inproj(x, gain, w, layer):
    t, d = x.shape
    tm = PROJ_TM
    widths = (NSA_WIDTH, KV_WIDTH, KV_WIDTH, 4 * KV_WIDTH, LANES, SSM_INNER, CONV_CH)
    dtypes = (F32, F32, F32, BF16, F32, F32, F32)
    return pl.pallas_call(
        _inproj_kernel,
        grid=(t // tm,),
        in_specs=[
            pl.BlockSpec((tm, d), lambda i: (i, 0)),
            pl.BlockSpec((None, 1, d), lambda i: (layer, 0, 0)),
            pl.BlockSpec((None, d, PROJ_COLS), lambda i: (layer, 0, 0)),
        ],
        out_specs=[pl.BlockSpec((tm, n), lambda i: (i, 0)) for n in widths],
        out_shape=[jax.ShapeDtypeStruct((t, n), dt) for n, dt in zip(widths, dtypes)],
        compiler_params=_cparams("parallel"),
        name="inproj",
    )(x, gain, w)


def _prep_inproj_weight(w_mix_in):
    offs = np.cumsum((0, NSA_WIDTH) + (KV_WIDTH,) * 6 + (3 * NSA_HEADS, SSM_INNER, CONV_CH, SSM_HEADS))
    sl = lambda a, b: w_mix_in[:, :, a:b]
    q = sl(offs[0], offs[1])
    q = jnp.concatenate([q[:, :, h * HEAD_DIM:(h + 1) * HEAD_DIM] for h in HEAD_ORDER], axis=-1)
    kc, vc = sl(offs[1], offs[2]), sl(offs[2], offs[3])
    kv = sl(offs[3], offs[7])
    gates = sl(offs[7], offs[8])
    z = sl(offs[8], offs[9])
    xbc = sl(offs[9], offs[10])
    dt = sl(offs[10], offs[11])
    pad = jnp.zeros(w_mix_in.shape[:2] + (LANES - 3 * NSA_HEADS - SSM_HEADS,), w_mix_in.dtype)
    misc = jnp.concatenate([gates, dt, pad], axis=-1)
    return jnp.concatenate([q, kc, vc, kv, misc, z, xbc], axis=-1).astype(BF16)


def _compress_kernel(kc_ref, vc_ref, pos_ref, w1a_ref, w1b_ref, b1_ref, w2_ref, b2_ref,
                     kcmp_ref, vcmpt_ref):
    nrow = kc_ref.shape[0]
    for which, src in enumerate((kc_ref, vc_ref)):
        r = src[...]
        ra = (r + pos_ref[which, 0]).astype(BF16)
        rb = (r + pos_ref[which, 1]).astype(BF16)
        ha = _dot(ra, w1a_ref[which])
        hb = _dot(rb, w1b_ref[which])
        h = ha + pltpu.roll(hb, nrow - 1, 0) + b1_ref[which]
        out = _dot(_silu(h).astype(BF16), w2_ref[which]) + b2_ref[which]
        if which == 0:
            kcmp_ref[...] = out.astype(BF16)
        else:
            vcmpt_ref[...] = out.T.astype(BF16)


def _compress(kc, vc, prep, batch, seq):
    nrow = seq // CMP_STRIDE
    wide = CMP_STRIDE * KV_WIDTH
    kcr = kc.reshape(batch, nrow, wide)
    vcr = vc.reshape(batch, nrow, wide)
    full = lambda a: pl.BlockSpec(a.shape, lambda b: (0,) * a.ndim)
    consts = (prep["pos"], prep["w1a"], prep["w1b"], prep["b1"], prep["w2"], prep["b2"])
    return pl.pallas_call(
        _compress_kernel,
        grid=(batch,),
        in_specs=[pl.BlockSpec((None, nrow, wide), lambda b: (b, 0, 0))] * 2 + [full(a) for a in consts],
        out_specs=[pl.BlockSpec((None, nrow, KV_WIDTH), lambda b: (b, 0, 0)),
                   pl.BlockSpec((None, KV_WIDTH, nrow), lambda b: (b, 0, 0))],
        out_shape=[jax.ShapeDtypeStruct((batch, nrow, KV_WIDTH), BF16),
                   jax.ShapeDtypeStruct((batch, KV_WIDTH, nrow), BF16)],
        compiler_params=_cparams("parallel"),
        name="nsa_compress",
    )(kcr, vcr, *consts)


def _prep_compress(cmp_pos, cmp_w1, cmp_b1, cmp_w2, cmp_b2):
    g = NSA_KV_GROUPS
    half = CMP_BLOCK // 2
    eye = jnp.eye(g, dtype=F32)
    w1 = cmp_w1.reshape(2, CMP_BLOCK, HEAD_DIM, CMP_HIDDEN)

    def expand(w):
        e = jnp.einsum("wldh,pg->wlpdgh", w, eye)
        return e.reshape(2, half * g * HEAD_DIM, g * CMP_HIDDEN).astype(BF16)

    def pos_rows(p):
        return jnp.broadcast_to(p[:, :, None, :], (2, half, g, HEAD_DIM)).reshape(2, 1, half * g * HEAD_DIM)

    w2 = jnp.einsum("whd,gp->wghpd", cmp_w2, eye).reshape(2, g * CMP_HIDDEN, g * HEAD_DIM).astype(BF16)
    return {
        "pos": jnp.stack([pos_rows(cmp_pos[:, :half]), pos_rows(cmp_pos[:, half:])], axis=1),
        "w1a": expand(w1[:, :half]),
        "w1b": expand(w1[:, half:]),
        "b1": jnp.tile(cmp_b1, (1, g))[:, None, :],
        "w2": w2,
        "b2": jnp.tile(cmp_b2, (1, g))[:, None, :],
    }


def _t5_bucket_np(dist):
    n = np.maximum(dist, 0)
    exact = REL_BUCKETS // 2
    nf = np.maximum(n, exact).astype(np.float64)
    large = exact + (np.log(nf / exact) / math.log(REL_MAX_DIST / exact) * (REL_BUCKETS - exact)).astype(np.int64)
    return np.where(n < exact, n, np.minimum(large, REL_BUCKETS - 1)).astype(np.int32)


MASKED_BUCKET = REL_BUCKETS


def _bucket_maps(seq):
    j = np.arange(K_TILE)[:, None]
    i = np.arange(Q_TILE)[None, :]
    tiles = []
    for delta in range(N_BIAS_TABLES):
        d = Q_TILE * delta + i - j
        valid = (d >= 0) & ((d < WINDOW) if delta >= 4 else True)
        tiles.append(np.where(valid, _t5_bucket_np(d), MASKED_BUCKET))
    c = np.arange(seq // CMP_STRIDE)[:, None]
    t = np.arange(seq)[None, :]
    dc = t - (c * CMP_STRIDE + CMP_BLOCK - 1)
    cmp_map = np.where(dc >= 0, _t5_bucket_np(dc), MASKED_BUCKET)
    return np.stack(tiles).astype(np.int32), cmp_map.astype(np.int32)


def _tables_kernel(tab_ref, bkt_ref, bkc_ref, tb_ref, bc_ref):
    h = pl.program_id(0)
    for src, dst in ((bkt_ref, tb_ref), (bkc_ref, bc_ref)):
        bk = src[...]
        out = jnp.zeros(bk.shape, F32)
        for b in range(REL_BUCKETS + 1):
            out = jnp.where(bk == b, tab_ref[b, h], out)
        dst[...] = out * LOG2E


def _nsa_tables(rel_table, seq):
    bkt, bkc = _bucket_maps(seq)
    tab = jnp.concatenate([rel_table[:, np.asarray(HEAD_ORDER)],
                           jnp.full((1, NSA_HEADS), NEG_INF, F32)], axis=0)
    nc = seq // CMP_STRIDE
    return pl.pallas_call(
        _tables_kernel,
        grid=(NSA_HEADS,),
        in_specs=[
            pl.BlockSpec(memory_space=pltpu.SMEM),
            pl.BlockSpec(bkt.shape, lambda h: (0, 0, 0)),
            pl.BlockSpec(bkc.shape, lambda h: (0, 0)),
        ],
        out_specs=[pl.BlockSpec((N_BIAS_TABLES, None, K_TILE, Q_TILE), lambda h: (0, h, 0, 0)),
                   pl.BlockSpec((None, nc, seq), lambda h: (h, 0, 0))],
        out_shape=[jax.ShapeDtypeStruct((N_BIAS_TABLES, NSA_HEADS, K_TILE, Q_TILE), F32),
                   jax.ShapeDtypeStruct((NSA_HEADS, nc, seq), F32)],
        compiler_params=_cparams("parallel"),
        name="nsa_bias_tables",
    )(tab, jnp.asarray(bkt), jnp.asarray(bkc))


def _nsa_consts(seq):
    nc = seq // CMP_STRIDE
    nb = seq // SEL_BLOCK
    c = np.arange(nc)
    blk = np.arange(nb)
    c_lo, c_hi = c * CMP_STRIDE, c * CMP_STRIDE + CMP_BLOCK - 1
    s_lo, s_hi = blk * SEL_BLOCK, blk * SEL_BLOCK + SEL_BLOCK - 1
    ovt = (c_lo[None, :] <= s_hi[:, None]) & (c_hi[None, :] >= s_lo[:, None])
    ovt[:, nc - 1] = False
    return jnp.asarray(ovt, BF16)


def _nsa_kernel(q_ref, kcmp_ref, vcmpt_ref, ks_ref, vs_ref, kw_ref, vw_ref, misc_ref, biasc_ref, tb_ref,
                ovt_ref, o_ref, qpt_ref, vst_ref, vwt_ref, madd_ref, m_ref, acc_ref):
    n = pl.program_id(1)
    nkt = ks_ref.shape[0] // K_TILE
    nsel_blocks = ovt_ref.shape[0]
    nhb = NSA_HEADS
    cols = lambda hb: slice(hb * Q_TILE, (hb + 1) * Q_TILE)
    frow = lax.broadcasted_iota(jnp.int32, (KV_WIDTH, Q_TILE), 0)
    low = frow < HEAD_DIM

    @pl.when(n == 0)
    def _():
        arow = lax.broadcasted_iota(jnp.int32, (BF16_SUBLANES, K_TILE), 0)
        ones_row = jnp.where(arow == 0, 1.0, 0.0).astype(BF16)
        for t2 in range(nkt):
            rows = slice(t2 * K_TILE, (t2 + 1) * K_TILE)
            for src, dst in ((vs_ref, vst_ref), (vw_ref, vwt_ref)):
                dst[t2, 0:KV_WIDTH, :] = src[rows, :].astype(F32).T.astype(BF16)
                dst[t2, KV_WIDTH:, :] = ones_row

    scale = HEAD_DIM ** -0.5 * LOG2E
    for j in range(nhb // 2):
        slab = (q_ref[:, j * LANES:(j + 1) * LANES].T * scale).astype(BF16)
        zero = jnp.zeros_like(slab)
        qpt_ref[:, cols(2 * j)] = jnp.where(low, slab, zero)
        qpt_ref[:, cols(2 * j + 1)] = jnp.where(low, zero, slab)
    qpt = qpt_ref[...]

    tq = n * Q_TILE + lax.broadcasted_iota(jnp.int32, (1, Q_TILE), 1)
    has_block = (tq >= CMP_BLOCK - 1).astype(F32)
    sc = _dot(kcmp_ref[...], qpt)
    psum = [None, None]
    pcs = []
    for hb in range(nhb):
        s = sc[:, cols(hb)] + biasc_ref[hb]
        m = jnp.max(s, axis=0, keepdims=True)
        e = jnp.exp2(s - m)
        p = e / jnp.sum(e, axis=0, keepdims=True) * has_block
        pcs.append(p.astype(BF16))
        g = hb % 2
        psum[g] = p if psum[g] is None else psum[g] + p
    o_cmp = _dot(vcmpt_ref[...], jnp.concatenate(pcs, axis=1))

    blk = lax.broadcasted_iota(jnp.int32, (nsel_blocks, Q_TILE), 0)
    blk_f = blk.astype(F32)
    cur = tq // SEL_BLOCK
    forced = ((blk == 0) | (blk == cur) | (blk == cur - 1)).astype(F32)
    for g in range(NSA_KV_GROUPS):
        imp = _dot_f32_rhs(ovt_ref[...], psum[g])
        score = jnp.where(blk <= cur, imp + FORCE_SCORE * forced, -FORCE_SCORE)
        sel = jnp.zeros(score.shape, F32)
        for _ in range(SEL_TOPK):
            mx = jnp.max(score, axis=0, keepdims=True)
            first = jnp.min(jnp.where(score == mx, blk_f, float(nsel_blocks)), axis=0, keepdims=True)
            hit = blk_f == first
            sel = jnp.where(hit, 1.0, sel)
            score = jnp.where(hit, -jnp.inf, score)
        madd_ref[g] = (sel - 1.0) * (-NEG_INF)

    def attend(k_ref, vt_ref, t2_lo, t2_hi, selected):
        pair_cols = lambda pair: slice(2 * pair * Q_TILE, (2 * pair + 2) * Q_TILE)

        def scores(t2):
            start = pl.multiple_of(t2 * K_TILE, K_TILE)
            k = k_ref[pl.ds(start, K_TILE), :]
            return [_dot(k, qpt_ref[:, pair_cols(pair)]) for pair in range(nhb // 2)]

        def softmax_pv(t2, s_pairs):
            vt = vt_ref[t2]
            delta = n - 2 * t2
            ti = jnp.minimum(delta, 3) if selected else delta
            blk0 = t2 * (K_TILE // SEL_BLOCK)
            for pair in range(nhb // 2):
                c2 = pair_cols(pair)
                s2 = s_pairs[pair]
                ps, alphas = [], []
                for half in range(2):
                    hb = 2 * pair + half
                    s = s2[:, half * Q_TILE:(half + 1) * Q_TILE] + tb_ref[ti, hb]
                    if selected:
                        s = jnp.concatenate(
                            [s[j * SEL_BLOCK:(j + 1) * SEL_BLOCK] + madd_ref[half, pl.ds(blk0 + j, 1), :]
                             for j in range(K_TILE // SEL_BLOCK)], axis=0)
                    m_prev = m_ref[:, cols(hb)]
                    m_new = jnp.maximum(m_prev, jnp.max(s, axis=0, keepdims=True))
                    m_ref[:, cols(hb)] = m_new
                    alphas.append(jnp.exp2(m_prev - m_new))
                    ps.append(jnp.exp2(s - m_new).astype(BF16))
                pv = _dot(vt, jnp.concatenate(ps, axis=1))
                acc_ref[:, c2] = acc_ref[:, c2] * jnp.concatenate(alphas, axis=1) + pv

        m_ref[...] = jnp.full_like(m_ref, -jnp.inf)
        acc_ref[...] = jnp.zeros_like(acc_ref)
        count = t2_hi - t2_lo

        def two_steps(i, carry):
            ta = t2_lo + 2 * i
            sa, sb = scores(ta), scores(ta + 1)
            softmax_pv(ta, sa)
            softmax_pv(ta + 1, sb)
            return carry

        lax.fori_loop(0, count // 2, two_steps, 0)

        @pl.when(count % 2 == 1)
        def _():
            softmax_pv(t2_hi - 1, scores(t2_hi - 1))

        acc = acc_ref[...]
        return acc[0:KV_WIDTH] / acc[KV_WIDTH:KV_WIDTH + 1]

    t2_diag = n // 2
    o_sel = attend(ks_ref, vst_ref, 0, t2_diag + 1, True)
    o_win = attend(kw_ref, vwt_ref, jnp.maximum(t2_diag - 2, 0), t2_diag + 1, False)

    gates = jax.nn.sigmoid(misc_ref[...]).T

    def gate(branch, hb):
        col = GATE_LANE0 + branch * NSA_HEADS + HEAD_ORDER[hb]
        return gates[col:col + 1, :]

    for j in range(nhb // 2):
        outs = []
        for hb in (2 * j, 2 * j + 1):
            outs.append(gate(0, hb) * o_cmp[:, cols(hb)] + gate(1, hb) * o_sel[:, cols(hb)]
                        + gate(2, hb) * o_win[:, cols(hb)])
        o_ref[:, j * LANES:(j + 1) * LANES] = jnp.where(low, outs[0], outs[1]).T


def _nsa_attention(q, kcmp, vcmpt, kv, misc, tb, biasc, ovt, batch, seq):
    nq = seq // Q_TILE
    nc = seq // CMP_STRIDE
    nkt = seq // K_TILE
    cols_all = NSA_HEADS * Q_TILE
    vrows = KV_WIDTH + BF16_SUBLANES
    kvspec = lambda c: pl.BlockSpec((seq, KV_WIDTH), lambda b, n: (b, c))
    return pl.pallas_call(
        _nsa_kernel,
        grid=(batch, nq),
        in_specs=[
            pl.BlockSpec((Q_TILE, NSA_WIDTH), lambda b, n: (b * nq + n, 0)),
            pl.BlockSpec((None, nc, KV_WIDTH), lambda b, n: (b, 0, 0)),
            pl.BlockSpec((None, KV_WIDTH, nc), lambda b, n: (b, 0, 0)),
            kvspec(0), kvspec(1), kvspec(2), kvspec(3),
            pl.BlockSpec((Q_TILE, LANES), lambda b, n: (b * nq + n, 0)),
            pl.BlockSpec((NSA_HEADS, nc, Q_TILE), lambda b, n: (0, 0, n)),
            pl.BlockSpec(tb.shape, lambda b, n: (0, 0, 0, 0)),
            pl.BlockSpec(ovt.shape, lambda b, n: (0, 0)),
        ],
        out_specs=pl.BlockSpec((Q_TILE, NSA_WIDTH), lambda b, n: (b * nq + n, 0)),
        out_shape=jax.ShapeDtypeStruct((batch * seq, NSA_WIDTH), F32),
        scratch_shapes=[
            pltpu.VMEM((KV_WIDTH, cols_all), BF16),
            pltpu.VMEM((nkt, vrows, K_TILE), BF16),
            pltpu.VMEM((nkt, vrows, K_TILE), BF16),
            pltpu.VMEM((NSA_KV_GROUPS, seq // SEL_BLOCK, Q_TILE), F32),
            pltpu.VMEM((1, cols_all), F32),
            pltpu.VMEM((vrows, cols_all), F32),
        ],
        compiler_params=_cparams("arbitrary", "arbitrary"),
        name="nsa_attention",
    )(q, kcmp, vcmpt, kv, kv, kv, kv, misc, biasc, tb, ovt)


def _softplus(x):
    return jnp.maximum(x, 0.0) + jnp.log1p(jnp.exp(-jnp.abs(x)))


def _ssd_kernel(xbc_ref, z_ref, misc_ref, convw_ref, convb_ref, dtb_ref, alog_ref, dskip_ref, gain_ref,
                tri_ref, e1_ref, o_ref, prev_ref, h_ref):
    c = pl.program_id(1)
    L = SSD_CHUNK
    gw = SSM_INNER // SSM_GROUPS
    hpg = SSM_HEADS // SSM_GROUPS

    @pl.when(c == 0)
    def _():
        prev_ref[...] = jnp.zeros_like(prev_ref)
        h_ref[...] = jnp.zeros_like(h_ref)

    x = xbc_ref[...]
    prev_ref[CONV_TAIL:, :] = x
    acc = convb_ref[...] + x * convw_ref[CONV_WIDTH - 1:CONV_WIDTH, :]
    for k in range(1, CONV_WIDTH):
        xk = prev_ref[CONV_TAIL - k:CONV_TAIL - k + L, :]
        acc = acc + xk * convw_ref[CONV_WIDTH - 1 - k:CONV_WIDTH - k, :]
    prev_ref[0:CONV_TAIL, :] = x[L - CONV_TAIL:L]
    xa = _silu(acc)
    xs = xa[:, :SSM_INNER]
    bm = xa[:, SSM_INNER:SSM_INNER + SSM_GROUPS * SSM_STATE]
    cm = xa[:, SSM_INNER + SSM_GROUPS * SSM_STATE:]

    dt = _softplus(misc_ref[...] + dtb_ref[...])
    da = dt * (-jnp.exp(alog_ref[...]))
    cs = _dot_f32_rhs(tri_ref[...], da)
    cs_t = cs.T
    dt_t = dt.T
    ecs = _dot_f32x2_lhs(jnp.exp(cs), e1_ref[...])
    to_end = dt * jnp.exp(cs[L - 1:L, :] - cs)
    xw_b = (xs * _dot_f32x2_lhs(to_end, e1_ref[...])).astype(BF16)
    xs_b = xs.astype(BF16)
    state_decay = ecs[L - 1:L, :]

    li = lax.broadcasted_iota(jnp.int32, (L, L), 0)
    si = lax.broadcasted_iota(jnp.int32, (L, L), 1)
    causal = li >= si
    low = si < SSM_HEAD_DIM

    ys = []
    for g in range(SSM_GROUPS):
        bg = bm[:, g * SSM_STATE:(g + 1) * SSM_STATE]
        cg = cm[:, g * SSM_STATE:(g + 1) * SSM_STATE].astype(BF16)
        cb = _dot_nt(cg, bg.astype(BF16))
        h_g = h_ref[:, g * gw:(g + 1) * gw]
        y_off = _dot(cg, h_g.astype(BF16)) * ecs[:, g * gw:(g + 1) * gw]
        for pr in range(hpg // 2):
            h0 = g * hpg + 2 * pr
            gs = []
            for hh in (h0, h0 + 1):
                ln = DT_LANE0 + hh
                col = jnp.broadcast_to(cs[:, ln:ln + 1], (L, L))
                dec = jnp.exp(jnp.where(causal, col - cs_t[ln:ln + 1, :], NEG_INF))
                gs.append((cb * dec * dt_t[ln:ln + 1, :]).astype(BF16))
            ch = slice(h0 * SSM_HEAD_DIM, (h0 + 2) * SSM_HEAD_DIM)
            xpair = xs_b[:, ch]
            zero = jnp.zeros_like(xpair)
            rhs = jnp.concatenate([jnp.where(low, xpair, zero), jnp.where(low, zero, xpair)], axis=0)
            y_diag = _dot(jnp.concatenate(gs, axis=1), rhs)
            off = slice(2 * pr * SSM_HEAD_DIM, (2 * pr + 2) * SSM_HEAD_DIM)
            ys.append(y_diag + y_off[:, off] + xs[:, ch] * dskip_ref[:, ch])
        st = _dot(bg.T.astype(BF16), xw_b[:, g * gw:(g + 1) * gw])
        h_ref[:, g * gw:(g + 1) * gw] = h_g * state_decay[:, g * gw:(g + 1) * gw] + st

    y = jnp.concatenate(ys, axis=1) * _silu(z_ref[...])
    outs = []
    for g in range(SSM_GROUPS):
        outs.append(_rms(y[:, g * gw:(g + 1) * gw], gain_ref[:, g * gw:(g + 1) * gw]))
    o_ref[...] = jnp.concatenate(outs, axis=1).astype(BF16)


def _ssd_consts():
    lane = np.arange(LANES)
    tri = (np.arange(SSD_CHUNK)[:, None] >= np.arange(SSD_CHUNK)[None, :])
    head1 = np.arange(SSM_INNER) // SSM_HEAD_DIM
    e1 = (lane[:, None] - DT_LANE0) == head1[None, :]
    return jnp.asarray(tri, BF16), jnp.asarray(e1, BF16)


def _ssd(xbc, z, misc, prm, consts, layer, batch, seq):
    nch = seq // SSD_CHUNK
    tok = lambda n: pl.BlockSpec((SSD_CHUNK, n), lambda b, c: (b * nch + c, 0))
    lay = lambda a: pl.BlockSpec((None,) + a.shape[1:], lambda b, c: (layer,) + (0,) * (a.ndim - 1))
    full = lambda a: pl.BlockSpec(a.shape, lambda b, c: (0,) * a.ndim)
    params = (prm["conv_w"], prm["conv_b"], prm["dt_bias"], prm["a_log"], prm["d_skip"], prm["ssm_gain"])
    return pl.pallas_call(
        _ssd_kernel,
        grid=(batch, nch),
        in_specs=[tok(CONV_CH), tok(SSM_INNER), tok(LANES)] + [lay(a) for a in params] + [full(a) for a in consts],
        out_specs=tok(SSM_INNER),
        out_shape=jax.ShapeDtypeStruct((batch * seq, SSM_INNER), BF16),
        scratch_shapes=[pltpu.VMEM((CONV_TAIL + SSD_CHUNK, CONV_CH), F32), pltpu.VMEM((SSM_STATE, SSM_INNER), F32)],
        compiler_params=_cparams("parallel", "arbitrary"),
        name="ssd",
    )(xbc, z, misc, *params, *consts)


def _prep_ssd_params(conv_w, conv_b, dt_bias, a_log, d_skip, ssm_out_norm):
    nl = conv_w.shape[0]

    def dt_lanes(v):
        out = jnp.zeros((nl, 1, LANES), F32)
        return out.at[:, 0, DT_LANE0:DT_LANE0 + SSM_HEADS].set(v)

    return {
        "conv_w": conv_w,
        "conv_b": conv_b[:, None, :],
        "dt_bias": dt_lanes(dt_bias),
        "a_log": dt_lanes(a_log),
        "d_skip": jnp.repeat(d_skip, SSM_HEAD_DIM, axis=-1)[:, None, :],
        "ssm_gain": ssm_out_norm[:, None, :],
    }


def _outproj_kernel(x_ref, oa_ref, os_ref, g_ref, wa_ref, ws_ref, o_ref):
    an = _rms(oa_ref[...], g_ref[...]).astype(BF16)
    o_ref[...] = x_ref[...] + _dot(an, wa_ref[...]) + _dot(os_ref[...], ws_ref[...])


def _outproj(x, o_attn, o_ssm, gain, wa, ws, layer):
    t, d = x.shape
    tm = PROJ_TM
    return pl.pallas_call(
        _outproj_kernel,
        grid=(t // tm,),
        in_specs=[
            pl.BlockSpec((tm, d), lambda i: (i, 0)),
            pl.BlockSpec((tm, NSA_WIDTH), lambda i: (i, 0)),
            pl.BlockSpec((tm, SSM_INNER), lambda i: (i, 0)),
            pl.BlockSpec((None, 1, NSA_WIDTH), lambda i: (layer, 0, 0)),
            pl.BlockSpec((None, NSA_WIDTH, d), lambda i: (layer, 0, 0)),
            pl.BlockSpec((None, SSM_INNER, d), lambda i: (layer, 0, 0)),
        ],
        out_specs=pl.BlockSpec((tm, d), lambda i: (i, 0)),
        out_shape=jax.ShapeDtypeStruct((t, d), F32),
        compiler_params=_cparams("parallel"),
        name="outproj",
    )(x, o_attn, o_ssm, gain, wa, ws)


def _perm_heads(a, axis):
    idx = np.concatenate([np.arange(h * HEAD_DIM, (h + 1) * HEAD_DIM) for h in HEAD_ORDER])
    return jnp.take(a, jnp.asarray(idx), axis=axis)


def _ple_kernel(x_ref, p_ref, g_ref, wg_ref, wp_ref, fg_ref, o_ref, *, final):
    x = x_ref[...]
    xn = _rms(x, g_ref[...]).astype(BF16)
    gate = jax.nn.sigmoid(_dot(xn, wg_ref[...]))
    y = x + gate * _dot(p_ref[...].astype(BF16), wp_ref[...])
    if final:
        y = _rms(y, fg_ref[...])
    o_ref[...] = y


def _ple(x, p, gain, wg, wp, final_gain, layer, final):
    t, d = x.shape
    tm = PROJ_TM
    return pl.pallas_call(
        functools.partial(_ple_kernel, final=final),
        grid=(t // tm,),
        in_specs=[
            pl.BlockSpec((tm, d), lambda i: (i, 0)),
            pl.BlockSpec((None, tm, PLE_DIM), lambda i: (layer, i, 0)),
            pl.BlockSpec((None, 1, d), lambda i: (layer, 0, 0)),
            pl.BlockSpec((None, d, d), lambda i: (layer, 0, 0)),
            pl.BlockSpec((None, PLE_DIM, d), lambda i: (layer, 0, 0)),
            pl.BlockSpec((1, d), lambda i: (0, 0)),
        ],
        out_specs=pl.BlockSpec((tm, d), lambda i: (i, 0)),
        out_shape=jax.ShapeDtypeStruct((t, d), F32),
        compiler_params=_cparams("parallel"),
        name="ple",
    )(x, p, gain, wg, wp, final_gain)


def kernel(x, p, ffn1_norm, ffn1_w_in, ffn1_w_out, mix_norm, w_mix_in, cmp_pos, cmp_w1, cmp_b1, cmp_w2,
           cmp_b2, rel_table, nsa_out_norm, conv_w, conv_b, dt_bias, a_log, d_skip, ssm_out_norm, w_mix_out,
           ffn2_norm, ffn2_w_in, ffn2_w_out, ple_norm, ple_gate_w, ple_proj_w, final_norm):
    batch, seq, d = x.shape
    depth = p.shape[0]
    t = batch * seq
    bf = lambda a: a.astype(BF16)
    row = lambda a: a[:, None, :]

    ffn1_in, ffn1_out, ffn2_in, ffn2_out = bf(ffn1_w_in), bf(ffn1_w_out), bf(ffn2_w_in), bf(ffn2_w_out)
    w_proj = _prep_inproj_weight(w_mix_in)
    wo_attn = bf(_perm_heads(w_mix_out[:, :NSA_WIDTH], axis=1))
    wo_ssm = bf(w_mix_out[:, NSA_WIDTH:])
    nsa_gain = row(_perm_heads(nsa_out_norm, axis=1))
    ssd_prm = _prep_ssd_params(conv_w, conv_b, dt_bias, a_log, d_skip, ssm_out_norm)
    ssd_consts = _ssd_consts()
    tb, biasc = _nsa_tables(rel_table, seq)
    ovt = _nsa_consts(seq)
    wg, wp = bf(ple_gate_w), bf(ple_proj_w)
    p2 = p.reshape(depth, t, PLE_DIM)
    fgain = final_norm[None, :]

    h = x.reshape(t, d)
    for i in range(depth):
        h = _ffn(h, row(ffn1_norm), ffn1_in, ffn1_out, i)
        q, kc, vc, kv, misc, z, xbc = _inproj(h, row(mix_norm), w_proj, i)
        cprep = _prep_compress(cmp_pos[i], cmp_w1[i], cmp_b1[i], cmp_w2[i], cmp_b2[i])
        kcmp, vcmp = _compress(kc, vc, cprep, batch, seq)
        o_attn = _nsa_attention(q, kcmp, vcmp, kv, misc, tb, biasc, ovt, batch, seq)
        o_ssm = _ssd(xbc, z, misc, ssd_prm, ssd_consts, i, batch, seq)
        h = _outproj(h, o_attn, o_ssm, nsa_gain, wo_attn, wo_ssm, i)
        h = _ffn(h, row(ffn2_norm), ffn2_in, ffn2_out, i)
        h = _ple(h, p2, row(ple_norm), wg, wp, fgain, i, final=(i == depth - 1))
    return h.reshape(batch, seq, d)
```

```python
import functools
import math

import numpy as np
import jax
import jax.numpy as jnp
from jax import lax
from jax.experimental import pallas as pl
from jax.experimental.pallas import tpu as pltpu

F32 = jnp.float32
BF16 = jnp.bfloat16

D_MODEL = 1024
DEPTH = 4
PLE_DIM = 256
D_FF = 2816
EPS = 1e-6
NEG_INF = -1e30
FORCE_SCORE = 1e4

NSA_HEADS = 8
NSA_KV_GROUPS = 2
NSA_REP = NSA_HEADS // NSA_KV_GROUPS
HEAD_DIM = 64
NSA_WIDTH = NSA_HEADS * HEAD_DIM
KV_WIDTH = NSA_KV_GROUPS * HEAD_DIM
CMP_BLOCK = 32
CMP_STRIDE = 16
CMP_HIDDEN = 256
SEL_BLOCK = 64
SEL_TOPK = 8
WINDOW = 512
REL_BUCKETS = 32
REL_MAX_DIST = 128

SSM_HEADS = 16
SSM_HEAD_DIM = 64
SSM_INNER = SSM_HEADS * SSM_HEAD_DIM
SSM_GROUPS = 2
SSM_STATE = 128
CONV_WIDTH = 4
SSD_CHUNK = 128
CONV_CH = SSM_INNER + 2 * SSM_GROUPS * SSM_STATE

LANES = 128
VMEM_LIMIT_BYTES = 48 * 1024 * 1024

FFN_TM = 1024
FFN_TF = 256
PROJ_TM = 512
Q_TILE = 128
K_TILE = 256
ATTN_UNROLL = 4
HEAD_ORDER = (0, 4, 1, 5, 2, 6, 3, 7)
GATE_LANE0 = 0
DT_LANE0 = 3 * NSA_HEADS
N_BIAS_TABLES = 6
LOG2E = math.log2(math.e)
BF16_SUBLANES = 16
CONV_TAIL = 8


def _dot(a, b):
    return jnp.dot(a, b, preferred_element_type=F32)


def _dot_nt(a, b):
    return lax.dot_general(a, b, (((1,), (1,)), ((), ())), preferred_element_type=F32)


def _split3(v):
    hi = v.astype(BF16)
    r = v - hi.astype(F32)
    mid = r.astype(BF16)
    lo = (r - mid.astype(F32)).astype(BF16)
    return hi, mid, lo


def _dot_f32x2_lhs(v, e):
    hi = v.astype(BF16)
    lo = (v - hi.astype(F32)).astype(BF16)
    return _dot(hi, e) + _dot(lo, e)


def _dot_f32_rhs(e, v):
    hi, mid, lo = _split3(v)
    return _dot(e, hi) + _dot(e, mid) + _dot(e, lo)


def _rms(x, g):
    ms = jnp.mean(x * x, axis=-1, keepdims=True)
    return x * lax.rsqrt(ms + EPS) * g


def _silu(x):
    return x * jax.nn.sigmoid(x)


def _cparams(*sem):
    return pltpu.CompilerParams(dimension_semantics=sem, vmem_limit_bytes=VMEM_LIMIT_BYTES)


def _ffn_kernel(x_ref, g_ref, wg_ref, wu_ref, wo_ref, o_ref, xn_ref, acc_ref):
    j = pl.program_id(1)

    @pl.when(j == 0)
    def _():
        xn_ref[...] = _rms(x_ref[...], g_ref[...]).astype(BF16)
        acc_ref[...] = jnp.zeros_like(acc_ref)

    xn = xn_ref[...]
    gate = _dot(xn, wg_ref[...])
    up = _dot(xn, wu_ref[...])
    h = (_silu(gate) * up).astype(BF16)
    acc_ref[...] += _dot(h, wo_ref[...])

    @pl.when(j == pl.num_programs(1) - 1)
    def _():
        o_ref[...] = x_ref[...] + 0.5 * acc_ref[...]


def _ffn(x, gain, w_in, w_out, layer):
    t, d = x.shape
    nf = D_FF // FFN_TF
    return pl.pallas_call(
        _ffn_kernel,
        grid=(t // FFN_TM, nf),
        in_specs=[
            pl.BlockSpec((FFN_TM, d), lambda i, j: (i, 0)),
            pl.BlockSpec((None, 1, d), lambda i, j: (layer, 0, 0)),
            pl.BlockSpec((None, d, FFN_TF), lambda i, j: (layer, 0, j)),
            pl.BlockSpec((None, d, FFN_TF), lambda i, j: (layer, 0, j + nf)),
            pl.BlockSpec((None, FFN_TF, d), lambda i, j: (layer, j, 0)),
        ],
        out_specs=pl.BlockSpec((FFN_TM, d), lambda i, j: (i, 0)),
        out_shape=jax.ShapeDtypeStruct((t, d), F32),
        scratch_shapes=[pltpu.VMEM((FFN_TM, d), BF16), pltpu.VMEM((FFN_TM, d), F32)],
        compiler_params=_cparams("parallel", "arbitrary"),
        name="ffn",
    )(x, gain, w_in, w_in, w_out)


_C_Q = (0, NSA_WIDTH)
_C_KC = (_C_Q[1], _C_Q[1] + KV_WIDTH)
_C_VC = (_C_KC[1], _C_KC[1] + KV_WIDTH)
_C_KV = (_C_VC[1], _C_VC[1] + 4 * KV_WIDTH)
_C_MISC = (_C_KV[1], _C_KV[1] + LANES)
_C_Z = (_C_MISC[1], _C_MISC[1] + SSM_INNER)
_C_XBC = (_C_Z[1], _C_Z[1] + CONV_CH)
PROJ_COLS = _C_XBC[1]


def _inproj_kernel(x_ref, g_ref, w_ref, q_ref, kc_ref, vc_ref, kv_ref, misc_ref, z_ref, xbc_ref):
    xn = _rms(x_ref[...], g_ref[...]).astype(BF16)
    q_ref[...] = _dot(xn, w_ref[:, _C_Q[0]:_C_Q[1]])
    kc_ref[...] = _dot(xn, w_ref[:, _C_KC[0]:_C_KC[1]])
    vc_ref[...] = _dot(xn, w_ref[:, _C_VC[0]:_C_VC[1]])
    kv_ref[...] = _dot(xn, w_ref[:, _C_KV[0]:_C_KV[1]]).astype(BF16)
    misc_ref[...] = _dot(xn, w_ref[:, _C_MISC[0]:_C_MISC[1]])
    z_ref[...] = _dot(xn, w_ref[:, _C_Z[0]:_C_Z[1]])
    xbc_ref[...] = _dot(xn, w_ref[:, _C_XBC[0]:_C_XBC[1]])


def _inproj(x, gain, w, layer):
    t, d = x.shape
    tm = PROJ_TM
    widths = (NSA_WIDTH, KV_WIDTH, KV_WIDTH, 4 * KV_WIDTH, LANES, SSM_INNER, CONV_CH)
    dtypes = (F32, F32, F32, BF16, F32, F32, F32)
    return pl.pallas_call(
        _inproj_kernel,
        grid=(t // tm,),
        in_specs=[
            pl.BlockSpec((tm, d), lambda i: (i, 0)),
            pl.BlockSpec((None, 1, d), lambda i: (layer, 0, 0)),
            pl.BlockSpec((None, d, PROJ_COLS), lambda i: (layer, 0, 0)),
        ],
        out_specs=[pl.BlockSpec((tm, n), lambda i: (i, 0)) for n in widths],
        out_shape=[jax.ShapeDtypeStruct((t, n), dt) for n, dt in zip(widths, dtypes)],
        compiler_params=_cparams("parallel"),
        name="inproj",
    )(x, gain, w)


def _prep_inproj_weight(w_mix_in):
    offs = np.cumsum((0, NSA_WIDTH) + (KV_WIDTH,) * 6 + (3 * NSA_HEADS, SSM_INNER, CONV_CH, SSM_HEADS))
    sl = lambda a, b: w_mix_in[:, :, a:b]
    q = sl(offs[0], offs[1])
    q = jnp.concatenate([q[:, :, h * HEAD_DIM:(h + 1) * HEAD_DIM] for h in HEAD_ORDER], axis=-1)
    kc, vc = sl(offs[1], offs[2]), sl(offs[2], offs[3])
    kv = sl(offs[3], offs[7])
    gates = sl(offs[7], offs[8])
    z = sl(offs[8], offs[9])
    xbc = sl(offs[9], offs[10])
    dt = sl(offs[10], offs[11])
    pad = jnp.zeros(w_mix_in.shape[:2] + (LANES - 3 * NSA_HEADS - SSM_HEADS,), w_mix_in.dtype)
    misc = jnp.concatenate([gates, dt, pad], axis=-1)
    return jnp.concatenate([q, kc, vc, kv, misc, z, xbc], axis=-1).astype(BF16)


def _compress_kernel(kc_ref, vc_ref, pos_ref, w1a_ref, w1b_ref, b1_ref, w2_ref, b2_ref,
                     kcmp_ref, vcmpt_ref):
    nrow = kc_ref.shape[0]
    for which, src in enumerate((kc_ref, vc_ref)):
        r = src[...]
        ra = (r + pos_ref[which, 0]).astype(BF16)
        rb = (r + pos_ref[which, 1]).astype(BF16)
        ha = _dot(ra, w1a_ref[which])
        hb = _dot(rb, w1b_ref[which])
        h = ha + pltpu.roll(hb, nrow - 1, 0) + b1_ref[which]
        out = _dot(_silu(h).astype(BF16), w2_ref[which]) + b2_ref[which]
        if which == 0:
            kcmp_ref[...] = out.astype(BF16)
        else:
            vcmpt_ref[...] = out.T.astype(BF16)


def _compress(kc, vc, prep, batch, seq):
    nrow = seq // CMP_STRIDE
    wide = CMP_STRIDE * KV_WIDTH
    kcr = kc.reshape(batch, nrow, wide)
    vcr = vc.reshape(batch, nrow, wide)
    full = lambda a: pl.BlockSpec(a.shape, lambda b: (0,) * a.ndim)
    consts = (prep["pos"], prep["w1a"], prep["w1b"], prep["b1"], prep["w2"], prep["b2"])
    return pl.pallas_call(
        _compress_kernel,
        grid=(batch,),
        in_specs=[pl.BlockSpec((None, nrow, wide), lambda b: (b, 0, 0))] * 2 + [full(a) for a in consts],
        out_specs=[pl.BlockSpec((None, nrow, KV_WIDTH), lambda b: (b, 0, 0)),
                   pl.BlockSpec((None, KV_WIDTH, nrow), lambda b: (b, 0, 0))],
        out_shape=[jax.ShapeDtypeStruct((batch, nrow, KV_WIDTH), BF16),
                   jax.ShapeDtypeStruct((batch, KV_WIDTH, nrow), BF16)],
        compiler_params=_cparams("parallel"),
        name="nsa_compress",
    )(kcr, vcr, *consts)


def _prep_compress(cmp_pos, cmp_w1, cmp_b1, cmp_w2, cmp_b2):
    g = NSA_KV_GROUPS
    half = CMP_BLOCK // 2
    eye = jnp.eye(g, dtype=F32)
    w1 = cmp_w1.reshape(2, CMP_BLOCK, HEAD_DIM, CMP_HIDDEN)

    def expand(w):
        e = jnp.einsum("wldh,pg->wlpdgh", w, eye)
        return e.reshape(2, half * g * HEAD_DIM, g * CMP_HIDDEN).astype(BF16)

    def pos_rows(p):
        return jnp.broadcast_to(p[:, :, None, :], (2, half, g, HEAD_DIM)).reshape(2, 1, half * g * HEAD_DIM)

    w2 = jnp.einsum("whd,gp->wghpd", cmp_w2, eye).reshape(2, g * CMP_HIDDEN, g * HEAD_DIM).astype(BF16)
    return {
        "pos": jnp.stack([pos_rows(cmp_pos[:, :half]), pos_rows(cmp_pos[:, half:])], axis=1),
        "w1a": expand(w1[:, :half]),
        "w1b": expand(w1[:, half:]),
        "b1": jnp.tile(cmp_b1, (1, g))[:, None, :],
        "w2": w2,
        "b2": jnp.tile(cmp_b2, (1, g))[:, None, :],
    }


def _t5_bucket_np(dist):
    n = np.maximum(dist, 0)
    exact = REL_BUCKETS // 2
    nf = np.maximum(n, exact).astype(np.float64)
    large = exact + (np.log(nf / exact) / math.log(REL_MAX_DIST / exact) * (REL_BUCKETS - exact)).astype(np.int64)
    return np.where(n < exact, n, np.minimum(large, REL_BUCKETS - 1)).astype(np.int32)


MASKED_BUCKET = REL_BUCKETS


def _bucket_maps(seq):
    j = np.arange(K_TILE)[:, None]
    i = np.arange(Q_TILE)[None, :]
    tiles = []
    for delta in range(N_BIAS_TABLES):
        d = Q_TILE * delta + i - j
        valid = (d >= 0) & ((d < WINDOW) if delta >= 4 else True)
        tiles.append(np.where(valid, _t5_bucket_np(d), MASKED_BUCKET))
    c = np.arange(seq // CMP_STRIDE)[:, None]
    t = np.arange(seq)[None, :]
    dc = t - (c * CMP_STRIDE + CMP_BLOCK - 1)
    cmp_map = np.where(dc >= 0, _t5_bucket_np(dc), MASKED_BUCKET)
    return np.stack(tiles).astype(np.int32), cmp_map.astype(np.int32)


def _tables_kernel(tab_ref, bkt_ref, bkc_ref, tb_ref, bc_ref):
    h = pl.program_id(0)
    for src, dst in ((bkt_ref, tb_ref), (bkc_ref, bc_ref)):
        bk = src[...]
        out = jnp.zeros(bk.shape, F32)
        for b in range(REL_BUCKETS + 1):
            out = jnp.where(bk == b, tab_ref[b, h], out)
        dst[...] = (out * LOG2E).astype(dst.dtype)


def _nsa_tables(rel_table, seq):
    bkt, bkc = _bucket_maps(seq)
    tab = jnp.concatenate([rel_table[:, np.asarray(HEAD_ORDER)],
                           jnp.full((1, NSA_HEADS), NEG_INF, F32)], axis=0)
    nc = seq // CMP_STRIDE
    return pl.pallas_call(
        _tables_kernel,
        grid=(NSA_HEADS,),
        in_specs=[
            pl.BlockSpec(memory_space=pltpu.SMEM),
            pl.BlockSpec(bkt.shape, lambda h: (0, 0, 0)),
            pl.BlockSpec(bkc.shape, lambda h: (0, 0)),
        ],
        out_specs=[pl.BlockSpec((N_BIAS_TABLES, None, K_TILE, Q_TILE), lambda h: (0, h, 0, 0)),
                   pl.BlockSpec((None, nc, seq), lambda h: (h, 0, 0))],
        out_shape=[jax.ShapeDtypeStruct((N_BIAS_TABLES, NSA_HEADS, K_TILE, Q_TILE), BF16),
                   jax.ShapeDtypeStruct((NSA_HEADS, nc, seq), F32)],
        compiler_params=_cparams("parallel"),
        name="nsa_bias_tables",
    )(tab, jnp.asarray(bkt), jnp.asarray(bkc))


def _nsa_consts(seq):
    nc = seq // CMP_STRIDE
    nb = seq // SEL_BLOCK
    c = np.arange(nc)
    blk = np.arange(nb)
    c_lo, c_hi = c * CMP_STRIDE, c * CMP_STRIDE + CMP_BLOCK - 1
    s_lo, s_hi = blk * SEL_BLOCK, blk * SEL_BLOCK + SEL_BLOCK - 1
    ovt = (c_lo[None, :] <= s_hi[:, None]) & (c_hi[None, :] >= s_lo[:, None])
    ovt[:, nc - 1] = False
    return jnp.asarray(ovt, BF16)


def _nsa_kernel(q_ref, kcmp_ref, vcmpt_ref, ks_ref, vs_ref, kw_ref, vw_ref, misc_ref, biasc_ref, tb_ref,
                ovt_ref, o_ref, qpt_ref, vst_ref, vwt_ref, madd_ref, m_ref, acc_ref):
    n = pl.program_id(1)
    nkt = ks_ref.shape[0] // K_TILE
    nsel_blocks = ovt_ref.shape[0]
    nhb = NSA_HEADS
    cols = lambda hb: slice(hb * Q_TILE, (hb + 1) * Q_TILE)
    frow = lax.broadcasted_iota(jnp.int32, (KV_WIDTH, Q_TILE), 0)
    low = frow < HEAD_DIM

    @pl.when(n == 0)
    def _():
        arow = lax.broadcasted_iota(jnp.int32, (BF16_SUBLANES, K_TILE), 0)
        ones_row = jnp.where(arow == 0, 1.0, 0.0).astype(BF16)
        for t2 in range(nkt):
            rows = slice(t2 * K_TILE, (t2 + 1) * K_TILE)
            for src, dst in ((vs_ref, vst_ref), (vw_ref, vwt_ref)):
                dst[t2, 0:KV_WIDTH, :] = src[rows, :].astype(F32).T.astype(BF16)
                dst[t2, KV_WIDTH:, :] = ones_row

    scale = HEAD_DIM ** -0.5 * LOG2E
    for j in range(nhb // 2):
        slab = (q_ref[:, j * LANES:(j + 1) * LANES].T * scale).astype(BF16)
        zero = jnp.zeros_like(slab)
        qpt_ref[:, cols(2 * j)] = jnp.where(low, slab, zero)
        qpt_ref[:, cols(2 * j + 1)] = jnp.where(low, zero, slab)
    qpt = qpt_ref[...]

    tq = n * Q_TILE + lax.broadcasted_iota(jnp.int32, (1, Q_TILE), 1)
    has_block = (tq >= CMP_BLOCK - 1).astype(F32)
    sc = _dot(kcmp_ref[...], qpt)
    psum = [None, None]
    pcs = []
    for hb in range(nhb):
        s = sc[:, cols(hb)] + biasc_ref[hb]
        m = jnp.max(s, axis=0, keepdims=True)
        e = jnp.exp2(s - m)
        p = e / jnp.sum(e, axis=0, keepdims=True) * has_block
        pcs.append(p.astype(BF16))
        g = hb % 2
        psum[g] = p if psum[g] is None else psum[g] + p
    o_cmp = _dot(vcmpt_ref[...], jnp.concatenate(pcs, axis=1))

    blk = lax.broadcasted_iota(jnp.int32, (nsel_blocks, Q_TILE), 0)
    blk_f = blk.astype(F32)
    cur = tq // SEL_BLOCK
    forced = ((blk == 0) | (blk == cur) | (blk == cur - 1)).astype(F32)
    for g in range(NSA_KV_GROUPS):
        imp = _dot_f32_rhs(ovt_ref[...], psum[g])
        score = jnp.where(blk <= cur, imp + FORCE_SCORE * forced, -FORCE_SCORE)
        sel = jnp.zeros(score.shape, F32)
        for _ in range(SEL_TOPK):
            mx = jnp.max(score, axis=0, keepdims=True)
            first = jnp.min(jnp.where(score == mx, blk_f, float(nsel_blocks)), axis=0, keepdims=True)
            hit = blk_f == first
            sel = jnp.where(hit, 1.0, sel)
            score = jnp.where(hit, -jnp.inf, score)
        madd_ref[g] = (sel - 1.0) * (-NEG_INF)

    def attend(k_ref, vt_ref, t2_lo, t2_hi, selected, max_tiles):
        pair_cols = lambda pair: slice(2 * pair * Q_TILE, (2 * pair + 2) * Q_TILE)

        def scores(t2):
            start = pl.multiple_of(t2 * K_TILE, K_TILE)
            k = k_ref[pl.ds(start, K_TILE), :]
            return [_dot(k, qpt_ref[:, pair_cols(pair)]) for pair in range(nhb // 2)]

        def softmax_pv(t2, s_pairs):
            vt = vt_ref[t2]
            delta = n - 2 * t2
            ti = jnp.minimum(delta, 3) if selected else delta
            blk0 = t2 * (K_TILE // SEL_BLOCK)
            for pair in range(nhb // 2):
                c2 = pair_cols(pair)
                s2 = s_pairs[pair]
                ps, alphas = [], []
                for half in range(2):
                    hb = 2 * pair + half
                    s = s2[:, half * Q_TILE:(half + 1) * Q_TILE].astype(BF16) + tb_ref[ti, hb]
                    if selected:
                        s = jnp.concatenate(
                            [s[j * SEL_BLOCK:(j + 1) * SEL_BLOCK]
                             + madd_ref[half, pl.ds(blk0 + j, 1), :].astype(BF16)
                             for j in range(K_TILE // SEL_BLOCK)], axis=0)
                    m_prev = m_ref[:, cols(hb)]
                    m_new = jnp.maximum(m_prev, jnp.max(s, axis=0, keepdims=True).astype(F32))
                    m_ref[:, cols(hb)] = m_new
                    alphas.append(jnp.exp2(m_prev - m_new))
                    ps.append(jnp.exp2(s - m_new.astype(BF16)))
                pv = _dot(vt, jnp.concatenate(ps, axis=1))
                acc_ref[:, c2] = acc_ref[:, c2] * jnp.concatenate(alphas, axis=1) + pv

        m_ref[...] = jnp.full_like(m_ref, -jnp.inf)
        acc_ref[...] = jnp.zeros_like(acc_ref)
        count = t2_hi - t2_lo

        def tiles(first, k):
            ss = [scores(first + i) for i in range(k)]
            for i in range(k):
                softmax_pv(first + i, ss[i])

        def full_blocks(i, carry):
            tiles(t2_lo + ATTN_UNROLL * i, ATTN_UNROLL)
            return carry

        if max_tiles >= ATTN_UNROLL:
            lax.fori_loop(0, count // ATTN_UNROLL, full_blocks, 0)
        rest_first = t2_lo + (count // ATTN_UNROLL) * ATTN_UNROLL
        for k in range(1, min(ATTN_UNROLL, max_tiles + 1)):
            @pl.when(count % ATTN_UNROLL == k)
            def _(k=k):
                tiles(rest_first, k)

        acc = acc_ref[...]
        return acc[0:KV_WIDTH] / acc[KV_WIDTH:KV_WIDTH + 1]

    t2_diag = n // 2
    o_sel = attend(ks_ref, vst_ref, 0, t2_diag + 1, True, nkt)
    o_win = attend(kw_ref, vwt_ref, jnp.maximum(t2_diag - 2, 0), t2_diag + 1, False, WINDOW // K_TILE + 1)

    gates = jax.nn.sigmoid(misc_ref[...]).T

    def gate(branch, hb):
        col = GATE_LANE0 + branch * NSA_HEADS + HEAD_ORDER[hb]
        return gates[col:col + 1, :]

    for j in range(nhb // 2):
        outs = []
        for hb in (2 * j, 2 * j + 1):
            outs.append(gate(0, hb) * o_cmp[:, cols(hb)] + gate(1, hb) * o_sel[:, cols(hb)]
                        + gate(2, hb) * o_win[:, cols(hb)])
        o_ref[:, j * LANES:(j + 1) * LANES] = jnp.where(low, outs[0], outs[1]).T


def _nsa_attention(q, kcmp, vcmpt, kv, misc, tb, biasc, ovt, batch, seq):
    nq = seq // Q_TILE
    nc = seq // CMP_STRIDE
    nkt = seq // K_TILE
    cols_all = NSA_HEADS * Q_TILE
    vrows = KV_WIDTH + BF16_SUBLANES
    kvspec = lambda c: pl.BlockSpec((seq, KV_WIDTH), lambda b, n: (b, c))
    return pl.pallas_call(
        _nsa_kernel,
        grid=(batch, nq),
        in_specs=[
            pl.BlockSpec((Q_TILE, NSA_WIDTH), lambda b, n: (b * nq + n, 0)),
            pl.BlockSpec((None, nc, KV_WIDTH), lambda b, n: (b, 0, 0)),
            pl.BlockSpec((None, KV_WIDTH, nc), lambda b, n: (b, 0, 0)),
            kvspec(0), kvspec(1), kvspec(2), kvspec(3),
            pl.BlockSpec((Q_TILE, LANES), lambda b, n: (b * nq + n, 0)),
            pl.BlockSpec((NSA_HEADS, nc, Q_TILE), lambda b, n: (0, 0, n)),
            pl.BlockSpec(tb.shape, lambda b, n: (0, 0, 0, 0)),
            pl.BlockSpec(ovt.shape, lambda b, n: (0, 0)),
        ],
        out_specs=pl.BlockSpec((Q_TILE, NSA_WIDTH), lambda b, n: (b * nq + n, 0)),
        out_shape=jax.ShapeDtypeStruct((batch * seq, NSA_WIDTH), F32),
        scratch_shapes=[
            pltpu.VMEM((KV_WIDTH, cols_all), BF16),
            pltpu.VMEM((nkt, vrows, K_TILE), BF16),
            pltpu.VMEM((nkt, vrows, K_TILE), BF16),
            pltpu.VMEM((NSA_KV_GROUPS, seq // SEL_BLOCK, Q_TILE), F32),
            pltpu.VMEM((1, cols_all), F32),
            pltpu.VMEM((vrows, cols_all), F32),
        ],
        compiler_params=_cparams("arbitrary", "arbitrary"),
        name="nsa_attention",
    )(q, kcmp, vcmpt, kv, kv, kv, kv, misc, biasc, tb, ovt)


def _softplus(x):
    return jnp.maximum(x, 0.0) + jnp.log1p(jnp.exp(-jnp.abs(x)))


def _ssd_kernel(xbc_ref, z_ref, misc_ref, convw_ref, convb_ref, dtb_ref, alog_ref, dskip_ref, gain_ref,
                tri_ref, e1_ref, o_ref, prev_ref, h_ref):
    c = pl.program_id(1)
    L = SSD_CHUNK
    gw = SSM_INNER // SSM_GROUPS
    hpg = SSM_HEADS // SSM_GROUPS

    @pl.when(c == 0)
    def _():
        prev_ref[...] = jnp.zeros_like(prev_ref)
        h_ref[...] = jnp.zeros_like(h_ref)

    x = xbc_ref[...]
    prev_ref[CONV_TAIL:, :] = x
    acc = convb_ref[...] + x * convw_ref[CONV_WIDTH - 1:CONV_WIDTH, :]
    for k in range(1, CONV_WIDTH):
        xk = prev_ref[CONV_TAIL - k:CONV_TAIL - k + L, :]
        acc = acc + xk * convw_ref[CONV_WIDTH - 1 - k:CONV_WIDTH - k, :]
    prev_ref[0:CONV_TAIL, :] = x[L - CONV_TAIL:L]
    xa = _silu(acc)
    xs = xa[:, :SSM_INNER]
    bm = xa[:, SSM_INNER:SSM_INNER + SSM_GROUPS * SSM_STATE]
    cm = xa[:, SSM_INNER + SSM_GROUPS * SSM_STATE:]

    dt = _softplus(misc_ref[...] + dtb_ref[...])
    da = dt * (-jnp.exp(alog_ref[...]))
    cs = _dot_f32_rhs(tri_ref[...], da)
    cs_t = cs.T
    dt_t = dt.T
    ecs = _dot_f32x2_lhs(jnp.exp(cs), e1_ref[...])
    to_end = dt * jnp.exp(cs[L - 1:L, :] - cs)
    xw_b = (xs * _dot_f32x2_lhs(to_end, e1_ref[...])).astype(BF16)
    xs_b = xs.astype(BF16)
    state_decay = ecs[L - 1:L, :]

    li = lax.broadcasted_iota(jnp.int32, (L, L), 0)
    si = lax.broadcasted_iota(jnp.int32, (L, L), 1)
    causal = li >= si
    low = si < SSM_HEAD_DIM

    ys = []
    for g in range(SSM_GROUPS):
        bg = bm[:, g * SSM_STATE:(g + 1) * SSM_STATE]
        cg = cm[:, g * SSM_STATE:(g + 1) * SSM_STATE].astype(BF16)
        cb = _dot_nt(cg, bg.astype(BF16))
        h_g = h_ref[:, g * gw:(g + 1) * gw]
        y_off = _dot(cg, h_g.astype(BF16)) * ecs[:, g * gw:(g + 1) * gw]
        for pr in range(hpg // 2):
            h0 = g * hpg + 2 * pr
            gs = []
            for hh in (h0, h0 + 1):
                ln = DT_LANE0 + hh
                col = jnp.broadcast_to(cs[:, ln:ln + 1], (L, L))
                dec = jnp.exp(jnp.where(causal, col - cs_t[ln:ln + 1, :], NEG_INF))
                gs.append((cb * dec * dt_t[ln:ln + 1, :]).astype(BF16))
            ch = slice(h0 * SSM_HEAD_DIM, (h0 + 2) * SSM_HEAD_DIM)
            xpair = xs_b[:, ch]
            zero = jnp.zeros_like(xpair)
            rhs = jnp.concatenate([jnp.where(low, xpair, zero), jnp.where(low, zero, xpair)], axis=0)
            y_diag = _dot(jnp.concatenate(gs, axis=1), rhs)
            off = slice(2 * pr * SSM_HEAD_DIM, (2 * pr + 2) * SSM_HEAD_DIM)
            ys.append(y_diag + y_off[:, off] + xs[:, ch] * dskip_ref[:, ch])
        st = _dot(bg.T.astype(BF16), xw_b[:, g * gw:(g + 1) * gw])
        h_ref[:, g * gw:(g + 1) * gw] = h_g * state_decay[:, g * gw:(g + 1) * gw] + st

    y = jnp.concatenate(ys, axis=1) * _silu(z_ref[...])
    outs = []
    for g in range(SSM_GROUPS):
        outs.append(_rms(y[:, g * gw:(g + 1) * gw], gain_ref[:, g * gw:(g + 1) * gw]))
    o_ref[...] = jnp.concatenate(outs, axis=1).astype(BF16)


def _ssd_consts():
    lane = np.arange(LANES)
    tri = (np.arange(SSD_CHUNK)[:, None] >= np.arange(SSD_CHUNK)[None, :])
    head1 = np.arange(SSM_INNER) // SSM_HEAD_DIM
    e1 = (lane[:, None] - DT_LANE0) == head1[None, :]
    return jnp.asarray(tri, BF16), jnp.asarray(e1, BF16)


def _ssd(xbc, z, misc, prm, consts, layer, batch, seq):
    nch = seq // SSD_CHUNK
    tok = lambda n: pl.BlockSpec((SSD_CHUNK, n), lambda b, c: (b * nch + c, 0))
    lay = lambda a: pl.BlockSpec((None,) + a.shape[1:], lambda b, c: (layer,) + (0,) * (a.ndim - 1))
    full = lambda a: pl.BlockSpec(a.shape, lambda b, c: (0,) * a.ndim)
    params = (prm["conv_w"], prm["conv_b"], prm["dt_bias"], prm["a_log"], prm["d_skip"], prm["ssm_gain"])
    return pl.pallas_call(
        _ssd_kernel,
        grid=(batch, nch),
        in_specs=[tok(CONV_CH), tok(SSM_INNER), tok(LANES)] + [lay(a) for a in params] + [full(a) for a in consts],
        out_specs=tok(SSM_INNER),
        out_shape=jax.ShapeDtypeStruct((batch * seq, SSM_INNER), BF16),
        scratch_shapes=[pltpu.VMEM((CONV_TAIL + SSD_CHUNK, CONV_CH), F32), pltpu.VMEM((SSM_STATE, SSM_INNER), F32)],
        compiler_params=_cparams("parallel", "arbitrary"),
        name="ssd",
    )(xbc, z, misc, *params, *consts)


def _prep_ssd_params(conv_w, conv_b, dt_bias, a_log, d_skip, ssm_out_norm):
    nl = conv_w.shape[0]

    def dt_lanes(v):
        out = jnp.zeros((nl, 1, LANES), F32)
        return out.at[:, 0, DT_LANE0:DT_LANE0 + SSM_HEADS].set(v)

    return {
        "conv_w": conv_w,
        "conv_b": conv_b[:, None, :],
        "dt_bias": dt_lanes(dt_bias),
        "a_log": dt_lanes(a_log),
        "d_skip": jnp.repeat(d_skip, SSM_HEAD_DIM, axis=-1)[:, None, :],
        "ssm_gain": ssm_out_norm[:, None, :],
    }


def _outproj_kernel(x_ref, oa_ref, os_ref, g_ref, wa_ref, ws_ref, o_ref):
    an = _rms(oa_ref[...], g_ref[...]).astype(BF16)
    o_ref[...] = x_ref[...] + _dot(an, wa_ref[...]) + _dot(os_ref[...], ws_ref[...])


def _outproj(x, o_attn, o_ssm, gain, wa, ws, layer):
    t, d = x.shape
    tm = PROJ_TM
    return pl.pallas_call(
        _outproj_kernel,
        grid=(t // tm,),
        in_specs=[
            pl.BlockSpec((tm, d), lambda i: (i, 0)),
            pl.BlockSpec((tm, NSA_WIDTH), lambda i: (i, 0)),
            pl.BlockSpec((tm, SSM_INNER), lambda i: (i, 0)),
            pl.BlockSpec((None, 1, NSA_WIDTH), lambda i: (layer, 0, 0)),
            pl.BlockSpec((None, NSA_WIDTH, d), lambda i: (layer, 0, 0)),
            pl.BlockSpec((None, SSM_INNER, d), lambda i: (layer, 0, 0)),
        ],
        out_specs=pl.BlockSpec((tm, d), lambda i: (i, 0)),
        out_shape=jax.ShapeDtypeStruct((t, d), F32),
        compiler_params=_cparams("parallel"),
        name="outproj",
    )(x, o_attn, o_ssm, gain, wa, ws)


def _perm_heads(a, axis):
    idx = np.concatenate([np.arange(h * HEAD_DIM, (h + 1) * HEAD_DIM) for h in HEAD_ORDER])
    return jnp.take(a, jnp.asarray(idx), axis=axis)


def _ple_kernel(x_ref, p_ref, g_ref, wg_ref, wp_ref, fg_ref, o_ref, *, final):
    x = x_ref[...]
    xn = _rms(x, g_ref[...]).astype(BF16)
    gate = jax.nn.sigmoid(_dot(xn, wg_ref[...]))
    y = x + gate * _dot(p_ref[...].astype(BF16), wp_ref[...])
    if final:
        y = _rms(y, fg_ref[...])
    o_ref[...] = y


def _ple(x, p, gain, wg, wp, final_gain, layer, final):
    t, d = x.shape
    tm = PROJ_TM
    return pl.pallas_call(
        functools.partial(_ple_kernel, final=final),
        grid=(t // tm,),
        in_specs=[
            pl.BlockSpec((tm, d), lambda i: (i, 0)),
            pl.BlockSpec((None, tm, PLE_DIM), lambda i: (layer, i, 0)),
            pl.BlockSpec((None, 1, d), lambda i: (layer, 0, 0)),
            pl.BlockSpec((None, d, d), lambda i: (layer, 0, 0)),
            pl.BlockSpec((None, PLE_DIM, d), lambda i: (layer, 0, 0)),
            pl.BlockSpec((1, d), lambda i: (0, 0)),
        ],
        out_specs=pl.BlockSpec((tm, d), lambda i: (i, 0)),
        out_shape=jax.ShapeDtypeStruct((t, d), F32),
        compiler_params=_cparams("parallel"),
        name="ple",
    )(x, p, gain, wg, wp, final_gain)


def kernel(x, p, ffn1_norm, ffn1_w_in, ffn1_w_out, mix_norm, w_mix_in, cmp_pos, cmp_w1, cmp_b1, cmp_w2,
           cmp_b2, rel_table, nsa_out_norm, conv_w, conv_b, dt_bias, a_log, d_skip, ssm_out_norm, w_mix_out,
           ffn2_norm, ffn2_w_in, ffn2_w_out, ple_norm, ple_gate_w, ple_proj_w, final_norm):
    batch, seq, d = x.shape
    depth = p.shape[0]
    t = batch * seq
    bf = lambda a: a.astype(BF16)
    row = lambda a: a[:, None, :]

    ffn1_in, ffn1_out, ffn2_in, ffn2_out = bf(ffn1_w_in), bf(ffn1_w_out), bf(ffn2_w_in), bf(ffn2_w_out)
    w_proj = _prep_inproj_weight(w_mix_in)
    wo_attn = bf(_perm_heads(w_mix_out[:, :NSA_WIDTH], axis=1))
    wo_ssm = bf(w_mix_out[:, NSA_WIDTH:])
    nsa_gain = row(_perm_heads(nsa_out_norm, axis=1))
    ssd_prm = _prep_ssd_params(conv_w, conv_b, dt_bias, a_log, d_skip, ssm_out_norm)
    ssd_consts = _ssd_consts()
    tb, biasc = _nsa_tables(rel_table, seq)
    ovt = _nsa_consts(seq)
    wg, wp = bf(ple_gate_w), bf(ple_proj_w)
    p2 = p.reshape(depth, t, PLE_DIM)
    fgain = final_norm[None, :]

    h = x.reshape(t, d)
    for i in range(depth):
        h = _ffn(h, row(ffn1_norm), ffn1_in, ffn1_out, i)
        q, kc, vc, kv, misc, z, xbc = _inproj(h, row(mix_norm), w_proj, i)
        cprep = _prep_compress(cmp_pos[i], cmp_w1[i], cmp_b1[i], cmp_w2[i], cmp_b2[i])
        kcmp, vcmp = _compress(kc, vc, cprep, batch, seq)
        o_attn = _nsa_attention(q, kcmp, vcmp, kv, misc, tb, biasc, ovt, batch, seq)
        o_ssm = _ssd(xbc, z, misc, ssd_prm, ssd_consts, i, batch, seq)
        h = _outproj(h, o_attn, o_ssm, nsa_gain, wo_attn, wo_ssm, i)
        h = _ffn(h, row(ffn2_norm), ffn2_in, ffn2_out, i)
        h = _ple(h, p2, row(ple_norm), wg, wp, fgain, i, final=(i == depth - 1))
    return h.reshape(batch, seq, d)
```

```python
import functools
import math

import numpy as np
import jax
import jax.numpy as jnp
from jax import lax
from jax.experimental import pallas as pl
from jax.experimental.pallas import tpu as pltpu

F32 = jnp.float32
BF16 = jnp.bfloat16

D_MODEL = 1024
DEPTH = 4
PLE_DIM = 256
D_FF = 2816
EPS = 1e-6
NEG_INF = -1e30
FORCE_SCORE = 1e4

NSA_HEADS = 8
NSA_KV_GROUPS = 2
NSA_REP = NSA_HEADS // NSA_KV_GROUPS
HEAD_DIM = 64
NSA_WIDTH = NSA_HEADS * HEAD_DIM
KV_WIDTH = NSA_KV_GROUPS * HEAD_DIM
CMP_BLOCK = 32
CMP_STRIDE = 16
CMP_HIDDEN = 256
SEL_BLOCK = 64
SEL_TOPK = 8
WINDOW = 512
REL_BUCKETS = 32
REL_MAX_DIST = 128

SSM_HEADS = 16
SSM_HEAD_DIM = 64
SSM_INNER = SSM_HEADS * SSM_HEAD_DIM
SSM_GROUPS = 2
SSM_STATE = 128
CONV_WIDTH = 4
SSD_CHUNK = 128
CONV_CH = SSM_INNER + 2 * SSM_GROUPS * SSM_STATE

LANES = 128
VMEM_LIMIT_BYTES = 48 * 1024 * 1024

FFN_TM = 1024
FFN_TF = 256
FFN_NORM_ROWS = 128
PROJ_TM = 512
Q_TILE = 128
K_TILE = 256
ATTN_UNROLL = 4
HEAD_ORDER = (0, 4, 1, 5, 2, 6, 3, 7)
GATE_LANE0 = 0
DT_LANE0 = 3 * NSA_HEADS
N_BIAS_TABLES = 6
LOG2E = math.log2(math.e)
BF16_SUBLANES = 16
CONV_TAIL = 8


def _dot(a, b):
    return jnp.dot(a, b, preferred_element_type=F32)


def _dot_nt(a, b):
    return lax.dot_general(a, b, (((1,), (1,)), ((), ())), preferred_element_type=F32)


def _split3(v):
    hi = v.astype(BF16)
    r = v - hi.astype(F32)
    mid = r.astype(BF16)
    lo = (r - mid.astype(F32)).astype(BF16)
    return hi, mid, lo


def _dot_f32x2_lhs(v, e):
    hi = v.astype(BF16)
    lo = (v - hi.astype(F32)).astype(BF16)
    return _dot(hi, e) + _dot(lo, e)


def _dot_f32_rhs(e, v):
    hi, mid, lo = _split3(v)
    return _dot(e, hi) + _dot(e, mid) + _dot(e, lo)


def _rms(x, g):
    ms = jnp.mean(x * x, axis=-1, keepdims=True)
    return x * lax.rsqrt(ms + EPS) * g


def _silu(x):
    return x * jax.nn.sigmoid(x)


def _cparams(*sem):
    return pltpu.CompilerParams(dimension_semantics=sem, vmem_limit_bytes=VMEM_LIMIT_BYTES)


def _ffn_kernel(x_ref, xnext_ref, g_ref, wg_ref, wu_ref, wo_ref, o_ref, xn_ref, acc_ref):
    i = pl.program_id(0)
    j = pl.program_id(1)
    slot = i % 2

    @pl.when((i == 0) & (j == 0))
    def _():
        xn_ref[0] = _rms(x_ref[...], g_ref[...]).astype(BF16)

    @pl.when(j == 0)
    def _():
        acc_ref[...] = jnp.zeros_like(acc_ref)

    xn = xn_ref[slot]
    gate = _dot(xn, wg_ref[...])
    up = _dot(xn, wu_ref[...])
    h = (_silu(gate) * up).astype(BF16)
    acc_ref[...] += _dot(h, wo_ref[...])

    chunk = jnp.minimum(j, FFN_TM // FFN_NORM_ROWS - 1)
    rows = pl.ds(pl.multiple_of(chunk * FFN_NORM_ROWS, FFN_NORM_ROWS), FFN_NORM_ROWS)
    xn_ref[1 - slot, rows, :] = _rms(xnext_ref[rows, :], g_ref[...]).astype(BF16)

    @pl.when(j == pl.num_programs(1) - 1)
    def _():
        o_ref[...] = x_ref[...] + 0.5 * acc_ref[...]


def _ffn(x, gain, w_in, w_out, layer):
    t, d = x.shape
    nf = D_FF // FFN_TF
    nt = t // FFN_TM
    assert nf >= FFN_TM // FFN_NORM_ROWS
    return pl.pallas_call(
        _ffn_kernel,
        grid=(nt, nf),
        in_specs=[
            pl.BlockSpec((FFN_TM, d), lambda i, j: (i, 0)),
            pl.BlockSpec((FFN_TM, d), lambda i, j: (jnp.minimum(i + 1, nt - 1), 0)),
            pl.BlockSpec((None, 1, d), lambda i, j: (layer, 0, 0)),
            pl.BlockSpec((None, d, FFN_TF), lambda i, j: (layer, 0, j)),
            pl.BlockSpec((None, d, FFN_TF), lambda i, j: (layer, 0, j + nf)),
            pl.BlockSpec((None, FFN_TF, d), lambda i, j: (layer, j, 0)),
        ],
        out_specs=pl.BlockSpec((FFN_TM, d), lambda i, j: (i, 0)),
        out_shape=jax.ShapeDtypeStruct((t, d), F32),
        scratch_shapes=[pltpu.VMEM((2, FFN_TM, d), BF16), pltpu.VMEM((FFN_TM, d), F32)],
        compiler_params=_cparams("arbitrary", "arbitrary"),
        name="ffn",
    )(x, x, gain, w_in, w_in, w_out)


_C_Q = (0, NSA_WIDTH)
_C_KC = (_C_Q[1], _C_Q[1] + KV_WIDTH)
_C_VC = (_C_KC[1], _C_KC[1] + KV_WIDTH)
_C_KV = (_C_VC[1], _C_VC[1] + 4 * KV_WIDTH)
_C_MISC = (_C_KV[1], _C_KV[1] + LANES)
_C_Z = (_C_MISC[1], _C_MISC[1] + SSM_INNER)
_C_XBC = (_C_Z[1], _C_Z[1] + CONV_CH)
PROJ_COLS = _C_XBC[1]


def _inproj_kernel(x_ref, g_ref, w_ref, q_ref, kc_ref, vc_ref, kv_ref, misc_ref, z_ref, xbc_ref):
    xn = _rms(x_ref[...], g_ref[...]).astype(BF16)
    q_ref[...] = _dot(xn, w_ref[:, _C_Q[0]:_C_Q[1]])
    kc_ref[...] = _dot(xn, w_ref[:, _C_KC[0]:_C_KC[1]])
    vc_ref[...] = _dot(xn, w_ref[:, _C_VC[0]:_C_VC[1]])
    kv_ref[...] = _dot(xn, w_ref[:, _C_KV[0]:_C_KV[1]]).astype(BF16)
    misc_ref[...] = _dot(xn, w_ref[:, _C_MISC[0]:_C_MISC[1]])
    z_ref[...] = _dot(xn, w_ref[:, _C_Z[0]:_C_Z[1]])
    xbc_ref[...] = _dot(xn, w_ref[:, _C_XBC[0]:_C_XBC[1]])


def _inproj(x, gain, w, layer):
    t, d = x.shape
    tm = PROJ_TM
    widths = (NSA_WIDTH, KV_WIDTH, KV_WIDTH, 4 * KV_WIDTH, LANES, SSM_INNER, CONV_CH)
    dtypes = (F32, F32, F32, BF16, F32, F32, F32)
    return pl.pallas_call(
        _inproj_kernel,
        grid=(t // tm,),
        in_specs=[
            pl.BlockSpec((tm, d), lambda i: (i, 0)),
            pl.BlockSpec((None, 1, d), lambda i: (layer, 0, 0)),
            pl.BlockSpec((None, d, PROJ_COLS), lambda i: (layer, 0, 0)),
        ],
        out_specs=[pl.BlockSpec((tm, n), lambda i: (i, 0)) for n in widths],
        out_shape=[jax.ShapeDtypeStruct((t, n), dt) for n, dt in zip(widths, dtypes)],
        compiler_params=_cparams("parallel"),
        name="inproj",
    )(x, gain, w)


_IN_OFFS = tuple(int(v) for v in np.cumsum(
    (0, NSA_WIDTH) + (KV_WIDTH,) * 6 + (3 * NSA_HEADS, SSM_INNER, CONV_CH, SSM_HEADS)))
PREP_ROWS = 256


def _prep_inproj_kernel(w_ref, o_ref):
    o = _IN_OFFS
    o_ref[:, _C_KC[0]:_C_KV[1]] = w_ref[:, o[1]:o[7]].astype(BF16)
    o_ref[:, _C_Z[0]:_C_XBC[1]] = w_ref[:, o[8]:o[10]].astype(BF16)
    for pos, h in enumerate(HEAD_ORDER):
        o_ref[:, pos * HEAD_DIM:(pos + 1) * HEAD_DIM] = w_ref[:, h * HEAD_DIM:(h + 1) * HEAD_DIM].astype(BF16)
    ngate, ndt = o[8] - o[7], o[11] - o[10]
    m0 = _C_MISC[0]
    o_ref[:, m0:m0 + ngate] = w_ref[:, o[7]:o[8]].astype(BF16)
    o_ref[:, m0 + ngate:m0 + ngate + ndt] = w_ref[:, o[10]:o[11]].astype(BF16)
    o_ref[:, m0 + ngate + ndt:_C_MISC[1]] = jnp.zeros((o_ref.shape[0], LANES - ngate - ndt), BF16)


def _prep_inproj_weight(w_mix_in):
    nl, d, win = w_mix_in.shape
    return pl.pallas_call(
        _prep_inproj_kernel,
        grid=(nl, d // PREP_ROWS),
        in_specs=[pl.BlockSpec((None, PREP_ROWS, win), lambda l, r: (l, r, 0))],
        out_specs=pl.BlockSpec((None, PREP_ROWS, PROJ_COLS), lambda l, r: (l, r, 0)),
        out_shape=jax.ShapeDtypeStruct((nl, d, PROJ_COLS), BF16),
        compiler_params=_cparams("parallel", "parallel"),
        name="inproj_weight_layout",
    )(w_mix_in)


def _compress_kernel(kc_ref, vc_ref, pos_ref, w1a_ref, w1b_ref, b1_ref, w2_ref, b2_ref,
                     kcmp_ref, vcmpt_ref):
    nrow = kc_ref.shape[0]
    for which, src in enumerate((kc_ref, vc_ref)):
        r = src[...]
        ra = (r + pos_ref[which, 0]).astype(BF16)
        rb = (r + pos_ref[which, 1]).astype(BF16)
        ha = _dot(ra, w1a_ref[which])
        hb = _dot(rb, w1b_ref[which])
        h = ha + pltpu.roll(hb, nrow - 1, 0) + b1_ref[which]
        out = _dot(_silu(h).astype(BF16), w2_ref[which]) + b2_ref[which]
        if which == 0:
            kcmp_ref[...] = out.astype(BF16)
        else:
            vcmpt_ref[...] = out.T.astype(BF16)


def _compress(kc, vc, prep, batch, seq):
    nrow = seq // CMP_STRIDE
    wide = CMP_STRIDE * KV_WIDTH
    kcr = kc.reshape(batch, nrow, wide)
    vcr = vc.reshape(batch, nrow, wide)
    full = lambda a: pl.BlockSpec(a.shape, lambda b: (0,) * a.ndim)
    consts = (prep["pos"], prep["w1a"], prep["w1b"], prep["b1"], prep["w2"], prep["b2"])
    return pl.pallas_call(
        _compress_kernel,
        grid=(batch,),
        in_specs=[pl.BlockSpec((None, nrow, wide), lambda b: (b, 0, 0))] * 2 + [full(a) for a in consts],
        out_specs=[pl.BlockSpec((None, nrow, KV_WIDTH), lambda b: (b, 0, 0)),
                   pl.BlockSpec((None, KV_WIDTH, nrow), lambda b: (b, 0, 0))],
        out_shape=[jax.ShapeDtypeStruct((batch, nrow, KV_WIDTH), BF16),
                   jax.ShapeDtypeStruct((batch, KV_WIDTH, nrow), BF16)],
        compiler_params=_cparams("parallel"),
        name="nsa_compress",
    )(kcr, vcr, *consts)


def _prep_compress(cmp_pos, cmp_w1, cmp_b1, cmp_w2, cmp_b2):
    g = NSA_KV_GROUPS
    half = CMP_BLOCK // 2
    eye = jnp.eye(g, dtype=F32)
    w1 = cmp_w1.reshape(2, CMP_BLOCK, HEAD_DIM, CMP_HIDDEN)

    def expand(w):
        e = jnp.einsum("wldh,pg->wlpdgh", w, eye)
        return e.reshape(2, half * g * HEAD_DIM, g * CMP_HIDDEN).astype(BF16)

    def pos_rows(p):
        return jnp.broadcast_to(p[:, :, None, :], (2, half, g, HEAD_DIM)).reshape(2, 1, half * g * HEAD_DIM)

    w2 = jnp.einsum("whd,gp->wghpd", cmp_w2, eye).reshape(2, g * CMP_HIDDEN, g * HEAD_DIM).astype(BF16)
    return {
        "pos": jnp.stack([pos_rows(cmp_pos[:, :half]), pos_rows(cmp_pos[:, half:])], axis=1),
        "w1a": expand(w1[:, :half]),
        "w1b": expand(w1[:, half:]),
        "b1": jnp.tile(cmp_b1, (1, g))[:, None, :],
        "w2": w2,
        "b2": jnp.tile(cmp_b2, (1, g))[:, None, :],
    }


def _t5_bucket_np(dist):
    n = np.maximum(dist, 0)
    exact = REL_BUCKETS // 2
    nf = np.maximum(n, exact).astype(np.float64)
    large = exact + (np.log(nf / exact) / math.log(REL_MAX_DIST / exact) * (REL_BUCKETS - exact)).astype(np.int64)
    return np.where(n < exact, n, np.minimum(large, REL_BUCKETS - 1)).astype(np.int32)


MASKED_BUCKET = REL_BUCKETS


def _bucket_maps(seq):
    j = np.arange(K_TILE)[:, None]
    i = np.arange(Q_TILE)[None, :]
    tiles = []
    for delta in range(N_BIAS_TABLES):
        d = Q_TILE * delta + i - j
        valid = (d >= 0) & ((d < WINDOW) if delta >= 4 else True)
        tiles.append(np.where(valid, _t5_bucket_np(d), MASKED_BUCKET))
    c = np.arange(seq // CMP_STRIDE)[:, None]
    t = np.arange(seq)[None, :]
    dc = t - (c * CMP_STRIDE + CMP_BLOCK - 1)
    cmp_map = np.where(dc >= 0, _t5_bucket_np(dc), MASKED_BUCKET)
    return np.stack(tiles).astype(np.int32), cmp_map.astype(np.int32)


def _tables_kernel(tab_ref, bkt_ref, bkc_ref, tb_ref, bc_ref):
    h = pl.program_id(0)
    for src, dst in ((bkt_ref, tb_ref), (bkc_ref, bc_ref)):
        bk = src[...]
        out = jnp.zeros(bk.shape, F32)
        for b in range(REL_BUCKETS + 1):
            out = jnp.where(bk == b, tab_ref[b, h], out)
        dst[...] = (out * LOG2E).astype(dst.dtype)


def _nsa_tables(rel_table, seq):
    bkt, bkc = _bucket_maps(seq)
    tab = jnp.concatenate([rel_table[:, np.asarray(HEAD_ORDER)],
                           jnp.full((1, NSA_HEADS), NEG_INF, F32)], axis=0)
    nc = seq // CMP_STRIDE
    return pl.pallas_call(
        _tables_kernel,
        grid=(NSA_HEADS,),
        in_specs=[
            pl.BlockSpec(memory_space=pltpu.SMEM),
            pl.BlockSpec(bkt.shape, lambda h: (0, 0, 0)),
            pl.BlockSpec(bkc.shape, lambda h: (0, 0)),
        ],
        out_specs=[pl.BlockSpec((N_BIAS_TABLES, None, K_TILE, Q_TILE), lambda h: (0, h, 0, 0)),
                   pl.BlockSpec((None, nc, seq), lambda h: (h, 0, 0))],
        out_shape=[jax.ShapeDtypeStruct((N_BIAS_TABLES, NSA_HEADS, K_TILE, Q_TILE), BF16),
                   jax.ShapeDtypeStruct((NSA_HEADS, nc, seq), F32)],
        compiler_params=_cparams("parallel"),
        name="nsa_bias_tables",
    )(tab, jnp.asarray(bkt), jnp.asarray(bkc))


def _nsa_consts(seq):
    nc = seq // CMP_STRIDE
    nb = seq // SEL_BLOCK
    c = np.arange(nc)
    blk = np.arange(nb)
    c_lo, c_hi = c * CMP_STRIDE, c * CMP_STRIDE + CMP_BLOCK - 1
    s_lo, s_hi = blk * SEL_BLOCK, blk * SEL_BLOCK + SEL_BLOCK - 1
    ovt = (c_lo[None, :] <= s_hi[:, None]) & (c_hi[None, :] >= s_lo[:, None])
    ovt[:, nc - 1] = False
    return jnp.asarray(ovt, BF16)


def _nsa_kernel(q_ref, kcmp_ref, vcmpt_ref, ks_ref, vs_ref, kw_ref, vw_ref, misc_ref, biasc_ref, tb_ref,
                ovt_ref, o_ref, qpt_ref, vst_ref, vwt_ref, madd_ref, m_ref, acc_ref, mw_ref, accw_ref, ocmp_ref):
    n = pl.program_id(1)
    nkt = ks_ref.shape[0] // K_TILE
    nsel_blocks = ovt_ref.shape[0]
    nhb = NSA_HEADS
    cols = lambda hb: slice(hb * Q_TILE, (hb + 1) * Q_TILE)
    frow = lax.broadcasted_iota(jnp.int32, (KV_WIDTH, Q_TILE), 0)
    low = frow < HEAD_DIM

    @pl.when(n == 0)
    def _():
        arow = lax.broadcasted_iota(jnp.int32, (BF16_SUBLANES, K_TILE), 0)
        ones_row = jnp.where(arow == 0, 1.0, 0.0).astype(BF16)
        for t2 in range(nkt):
            rows = slice(t2 * K_TILE, (t2 + 1) * K_TILE)
            for src, dst in ((vs_ref, vst_ref), (vw_ref, vwt_ref)):
                dst[t2, 0:KV_WIDTH, :] = src[rows, :].astype(F32).T.astype(BF16)
                dst[t2, KV_WIDTH:, :] = ones_row

    scale = HEAD_DIM ** -0.5 * LOG2E
    for j in range(nhb // 2):
        slab = (q_ref[:, j * LANES:(j + 1) * LANES].T * scale).astype(BF16)
        zero = jnp.zeros_like(slab)
        qpt_ref[:, cols(2 * j)] = jnp.where(low, slab, zero)
        qpt_ref[:, cols(2 * j + 1)] = jnp.where(low, zero, slab)
    qpt = qpt_ref[...]

    pair_cols = lambda pair: slice(2 * pair * Q_TILE, (2 * pair + 2) * Q_TILE)

    def branch(k_ref, vt_ref, mx_ref, ac_ref, selected):
        def scores(t2):
            start = pl.multiple_of(t2 * K_TILE, K_TILE)
            k = k_ref[pl.ds(start, K_TILE), :]
            return [_dot(k, qpt_ref[:, pair_cols(pair)]) for pair in range(nhb // 2)]

        def softmax_pv(t2, s_pairs):
            vt = vt_ref[t2]
            delta = n - 2 * t2
            ti = jnp.minimum(delta, 3) if selected else delta
            blk0 = t2 * (K_TILE // SEL_BLOCK)
            for pair in range(nhb // 2):
                c2 = pair_cols(pair)
                s2 = s_pairs[pair]
                ps, alphas = [], []
                for half in range(2):
                    hb = 2 * pair + half
                    s = s2[:, half * Q_TILE:(half + 1) * Q_TILE].astype(BF16) + tb_ref[ti, hb]
                    if selected:
                        s = jnp.concatenate(
                            [s[j * SEL_BLOCK:(j + 1) * SEL_BLOCK]
                             + madd_ref[half, pl.ds(blk0 + j, 1), :].astype(BF16)
                             for j in range(K_TILE // SEL_BLOCK)], axis=0)
                    m_prev = mx_ref[:, cols(hb)]
                    m_new = jnp.maximum(m_prev, jnp.max(s, axis=0, keepdims=True).astype(F32))
                    mx_ref[:, cols(hb)] = m_new
                    alphas.append(jnp.exp2(m_prev - m_new))
                    ps.append(jnp.exp2(s - m_new.astype(BF16)))
                pv = _dot(vt, jnp.concatenate(ps, axis=1))
                ac_ref[:, c2] = ac_ref[:, c2] * jnp.concatenate(alphas, axis=1) + pv

        def issue(first, k):
            return [scores(first + i) for i in range(k)]

        def finish(first, ss):
            for i, s_pairs in enumerate(ss):
                softmax_pv(first + i, s_pairs)

        def init():
            mx_ref[...] = jnp.full_like(mx_ref, -jnp.inf)
            ac_ref[...] = jnp.zeros_like(ac_ref)

        def result():
            acc = ac_ref[...]
            return acc[0:KV_WIDTH] / acc[KV_WIDTH:KV_WIDTH + 1]

        return init, issue, finish, result

    sel_init, sel_issue, sel_finish, sel_result = branch(ks_ref, vst_ref, m_ref, acc_ref, True)
    win_init, win_issue, win_finish, win_result = branch(kw_ref, vwt_ref, mw_ref, accw_ref, False)

    tq = n * Q_TILE + lax.broadcasted_iota(jnp.int32, (1, Q_TILE), 1)

    def compressed_and_select(win_first=None, win_k=0):
        has_block = (tq >= CMP_BLOCK - 1).astype(F32)
        sc = _dot(kcmp_ref[...], qpt)
        win_scores = win_issue(win_first, win_k)
        psum = [None, None]
        pcs = []
        for hb in range(nhb):
            s = sc[:, cols(hb)] + biasc_ref[hb]
            m = jnp.max(s, axis=0, keepdims=True)
            e = jnp.exp2(s - m)
            p = e / jnp.sum(e, axis=0, keepdims=True) * has_block
            pcs.append(p.astype(BF16))
            g = hb % 2
            psum[g] = p if psum[g] is None else psum[g] + p
        ocmp_ref[...] = _dot(vcmpt_ref[...], jnp.concatenate(pcs, axis=1))

        blk = lax.broadcasted_iota(jnp.int32, (nsel_blocks, Q_TILE), 0)
        blk_f = blk.astype(F32)
        cur = tq // SEL_BLOCK
        forced = ((blk == 0) | (blk == cur) | (blk == cur - 1)).astype(F32)
        for g in range(NSA_KV_GROUPS):
            imp = _dot_f32_rhs(ovt_ref[...], psum[g])
            score = jnp.where(blk <= cur, imp + FORCE_SCORE * forced, -FORCE_SCORE)
            sel = jnp.zeros(score.shape, F32)
            for _ in range(SEL_TOPK):
                mx = jnp.max(score, axis=0, keepdims=True)
                first = jnp.min(jnp.where(score == mx, blk_f, float(nsel_blocks)), axis=0, keepdims=True)
                hit = blk_f == first
                sel = jnp.where(hit, 1.0, sel)
                score = jnp.where(hit, -jnp.inf, score)
            madd_ref[g] = (sel - 1.0) * (-NEG_INF)
        win_finish(win_first, win_scores)

    t2_diag = n // 2
    win_full = WINDOW // K_TILE + 1
    win_init()

    @pl.when(t2_diag >= win_full - 1)
    def _():
        compressed_and_select(t2_diag - (win_full - 1), win_full)

    @pl.when(t2_diag < win_full - 1)
    def _():
        compressed_and_select()
        for k in range(1, win_full):
            @pl.when(t2_diag + 1 == k)
            def _(k=k):
                win_finish(0, win_issue(0, k))

    sel_init()
    count = t2_diag + 1

    def full_blocks(i, carry):
        sel_finish(ATTN_UNROLL * i, sel_issue(ATTN_UNROLL * i, ATTN_UNROLL))
        return carry

    lax.fori_loop(0, count // ATTN_UNROLL, full_blocks, 0)
    rest_first = (count // ATTN_UNROLL) * ATTN_UNROLL
    for k in range(1, ATTN_UNROLL):
        @pl.when(count % ATTN_UNROLL == k)
        def _(k=k):
            sel_finish(rest_first, sel_issue(rest_first, k))

    o_cmp = ocmp_ref[...]
    o_sel = sel_result()
    o_win = win_result()

    gates = jax.nn.sigmoid(misc_ref[...]).T

    def gate(branch, hb):
        col = GATE_LANE0 + branch * NSA_HEADS + HEAD_ORDER[hb]
        return gates[col:col + 1, :]

    for j in range(nhb // 2):
        outs = []
        for hb in (2 * j, 2 * j + 1):
            outs.append(gate(0, hb) * o_cmp[:, cols(hb)] + gate(1, hb) * o_sel[:, cols(hb)]
                        + gate(2, hb) * o_win[:, cols(hb)])
        o_ref[:, j * LANES:(j + 1) * LANES] = jnp.where(low, outs[0], outs[1]).T


def _nsa_attention(q, kcmp, vcmpt, kv, misc, tb, biasc, ovt, batch, seq):
    nq = seq // Q_TILE
    nc = seq // CMP_STRIDE
    nkt = seq // K_TILE
    cols_all = NSA_HEADS * Q_TILE
    vrows = KV_WIDTH + BF16_SUBLANES
    kvspec = lambda c: pl.BlockSpec((seq, KV_WIDTH), lambda b, n: (b, c))
    return pl.pallas_call(
        _nsa_kernel,
        grid=(batch, nq),
        in_specs=[
            pl.BlockSpec((Q_TILE, NSA_WIDTH), lambda b, n: (b * nq + n, 0)),
            pl.BlockSpec((None, nc, KV_WIDTH), lambda b, n: (b, 0, 0)),
            pl.BlockSpec((None, KV_WIDTH, nc), lambda b, n: (b, 0, 0)),
            kvspec(0), kvspec(1), kvspec(2), kvspec(3),
            pl.BlockSpec((Q_TILE, LANES), lambda b, n: (b * nq + n, 0)),
            pl.BlockSpec((NSA_HEADS, nc, Q_TILE), lambda b, n: (0, 0, n)),
            pl.BlockSpec(tb.shape, lambda b, n: (0, 0, 0, 0)),
            pl.BlockSpec(ovt.shape, lambda b, n: (0, 0)),
        ],
        out_specs=pl.BlockSpec((Q_TILE, NSA_WIDTH), lambda b, n: (b * nq + n, 0)),
        out_shape=jax.ShapeDtypeStruct((batch * seq, NSA_WIDTH), F32),
        scratch_shapes=[
            pltpu.VMEM((KV_WIDTH, cols_all), BF16),
            pltpu.VMEM((nkt, vrows, K_TILE), BF16),
            pltpu.VMEM((nkt, vrows, K_TILE), BF16),
            pltpu.VMEM((NSA_KV_GROUPS, seq // SEL_BLOCK, Q_TILE), F32),
            pltpu.VMEM((1, cols_all), F32),
            pltpu.VMEM((vrows, cols_all), F32),
            pltpu.VMEM((1, cols_all), F32),
            pltpu.VMEM((vrows, cols_all), F32),
            pltpu.VMEM((KV_WIDTH, cols_all), F32),
        ],
        compiler_params=_cparams("arbitrary", "arbitrary"),
        name="nsa_attention",
    )(q, kcmp, vcmpt, kv, kv, kv, kv, misc, biasc, tb, ovt)


def _softplus(x):
    return jnp.maximum(x, 0.0) + jnp.log1p(jnp.exp(-jnp.abs(x)))


def _ssd_kernel(xbc_ref, z_ref, misc_ref, convw_ref, convb_ref, dtb_ref, alog_ref, dskip_ref, gain_ref,
                tri_ref, e1_ref, o_ref, prev_ref, h_ref):
    c = pl.program_id(1)
    L = SSD_CHUNK
    gw = SSM_INNER // SSM_GROUPS
    hpg = SSM_HEADS // SSM_GROUPS

    @pl.when(c == 0)
    def _():
        prev_ref[...] = jnp.zeros_like(prev_ref)
        h_ref[...] = jnp.zeros_like(h_ref)

    x = xbc_ref[...]
    prev_ref[CONV_TAIL:, :] = x
    acc = convb_ref[...] + x * convw_ref[CONV_WIDTH - 1:CONV_WIDTH, :]
    for k in range(1, CONV_WIDTH):
        xk = prev_ref[CONV_TAIL - k:CONV_TAIL - k + L, :]
        acc = acc + xk * convw_ref[CONV_WIDTH - 1 - k:CONV_WIDTH - k, :]
    prev_ref[0:CONV_TAIL, :] = x[L - CONV_TAIL:L]
    xa = _silu(acc)
    xs = xa[:, :SSM_INNER]
    bm = xa[:, SSM_INNER:SSM_INNER + SSM_GROUPS * SSM_STATE]
    cm = xa[:, SSM_INNER + SSM_GROUPS * SSM_STATE:]

    dt = _softplus(misc_ref[...] + dtb_ref[...])
    da = dt * (-jnp.exp(alog_ref[...]))
    cs = _dot_f32_rhs(tri_ref[...], da)
    cs_t = cs.T
    dt_t = dt.T
    ecs = _dot_f32x2_lhs(jnp.exp(cs), e1_ref[...])
    to_end = dt * jnp.exp(cs[L - 1:L, :] - cs)
    xw_b = (xs * _dot_f32x2_lhs(to_end, e1_ref[...])).astype(BF16)
    xs_b = xs.astype(BF16)
    state_decay = ecs[L - 1:L, :]

    li = lax.broadcasted_iota(jnp.int32, (L, L), 0)
    si = lax.broadcasted_iota(jnp.int32, (L, L), 1)
    causal = li >= si
    low = si < SSM_HEAD_DIM

    ys = []
    for g in range(SSM_GROUPS):
        bg = bm[:, g * SSM_STATE:(g + 1) * SSM_STATE]
        cg = cm[:, g * SSM_STATE:(g + 1) * SSM_STATE].astype(BF16)
        cb = _dot_nt(cg, bg.astype(BF16))
        h_g = h_ref[:, g * gw:(g + 1) * gw]
        y_off = _dot(cg, h_g.astype(BF16)) * ecs[:, g * gw:(g + 1) * gw]
        for pr in range(hpg // 2):
            h0 = g * hpg + 2 * pr
            gs = []
            for hh in (h0, h0 + 1):
                ln = DT_LANE0 + hh
                col = jnp.broadcast_to(cs[:, ln:ln + 1], (L, L))
                dec = jnp.exp(jnp.where(causal, col - cs_t[ln:ln + 1, :], NEG_INF))
                gs.append((cb * dec * dt_t[ln:ln + 1, :]).astype(BF16))
            ch = slice(h0 * SSM_HEAD_DIM, (h0 + 2) * SSM_HEAD_DIM)
            xpair = xs_b[:, ch]
            zero = jnp.zeros_like(xpair)
            rhs = jnp.concatenate([jnp.where(low, xpair, zero), jnp.where(low, zero, xpair)], axis=0)
            y_diag = _dot(jnp.concatenate(gs, axis=1), rhs)
            off = slice(2 * pr * SSM_HEAD_DIM, (2 * pr + 2) * SSM_HEAD_DIM)
            ys.append(y_diag + y_off[:, off] + xs[:, ch] * dskip_ref[:, ch])
        st = _dot(bg.T.astype(BF16), xw_b[:, g * gw:(g + 1) * gw])
        h_ref[:, g * gw:(g + 1) * gw] = h_g * state_decay[:, g * gw:(g + 1) * gw] + st

    y = jnp.concatenate(ys, axis=1) * _silu(z_ref[...])
    outs = []
    for g in range(SSM_GROUPS):
        outs.append(_rms(y[:, g * gw:(g + 1) * gw], gain_ref[:, g * gw:(g + 1) * gw]))
    o_ref[...] = jnp.concatenate(outs, axis=1).astype(BF16)


def _ssd_consts():
    lane = np.arange(LANES)
    tri = (np.arange(SSD_CHUNK)[:, None] >= np.arange(SSD_CHUNK)[None, :])
    head1 = np.arange(SSM_INNER) // SSM_HEAD_DIM
    e1 = (lane[:, None] - DT_LANE0) == head1[None, :]
    return jnp.asarray(tri, BF16), jnp.asarray(e1, BF16)


def _ssd(xbc, z, misc, prm, consts, layer, batch, seq):
    nch = seq // SSD_CHUNK
    tok = lambda n: pl.BlockSpec((SSD_CHUNK, n), lambda b, c: (b * nch + c, 0))
    lay = lambda a: pl.BlockSpec((None,) + a.shape[1:], lambda b, c: (layer,) + (0,) * (a.ndim - 1))
    full = lambda a: pl.BlockSpec(a.shape, lambda b, c: (0,) * a.ndim)
    params = (prm["conv_w"], prm["conv_b"], prm["dt_bias"], prm["a_log"], prm["d_skip"], prm["ssm_gain"])
    return pl.pallas_call(
        _ssd_kernel,
        grid=(batch, nch),
        in_specs=[tok(CONV_CH), tok(SSM_INNER), tok(LANES)] + [lay(a) for a in params] + [full(a) for a in consts],
        out_specs=tok(SSM_INNER),
        out_shape=jax.ShapeDtypeStruct((batch * seq, SSM_INNER), BF16),
        scratch_shapes=[pltpu.VMEM((CONV_TAIL + SSD_CHUNK, CONV_CH), F32), pltpu.VMEM((SSM_STATE, SSM_INNER), F32)],
        compiler_params=_cparams("parallel", "arbitrary"),
        name="ssd",
    )(xbc, z, misc, *params, *consts)


def _prep_ssd_params(conv_w, conv_b, dt_bias, a_log, d_skip, ssm_out_norm):
    nl = conv_w.shape[0]

    def dt_lanes(v):
        out = jnp.zeros((nl, 1, LANES), F32)
        return out.at[:, 0, DT_LANE0:DT_LANE0 + SSM_HEADS].set(v)

    return {
        "conv_w": conv_w,
        "conv_b": conv_b[:, None, :],
        "dt_bias": dt_lanes(dt_bias),
        "a_log": dt_lanes(a_log),
        "d_skip": jnp.repeat(d_skip, SSM_HEAD_DIM, axis=-1)[:, None, :],
        "ssm_gain": ssm_out_norm[:, None, :],
    }


def _outproj_kernel(x_ref, oa_ref, os_ref, g_ref, wa_ref, ws_ref, o_ref):
    an = _rms(oa_ref[...], g_ref[...]).astype(BF16)
    o_ref[...] = x_ref[...] + _dot(an, wa_ref[...]) + _dot(os_ref[...], ws_ref[...])


def _outproj(x, o_attn, o_ssm, gain, wa, ws, layer):
    t, d = x.shape
    tm = PROJ_TM
    return pl.pallas_call(
        _outproj_kernel,
        grid=(t // tm,),
        in_specs=[
            pl.BlockSpec((tm, d), lambda i: (i, 0)),
            pl.BlockSpec((tm, NSA_WIDTH), lambda i: (i, 0)),
            pl.BlockSpec((tm, SSM_INNER), lambda i: (i, 0)),
            pl.BlockSpec((None, 1, NSA_WIDTH), lambda i: (layer, 0, 0)),
            pl.BlockSpec((None, NSA_WIDTH, d), lambda i: (layer, 0, 0)),
            pl.BlockSpec((None, SSM_INNER, d), lambda i: (layer, 0, 0)),
        ],
        out_specs=pl.BlockSpec((tm, d), lambda i: (i, 0)),
        out_shape=jax.ShapeDtypeStruct((t, d), F32),
        compiler_params=_cparams("parallel"),
        name="outproj",
    )(x, o_attn, o_ssm, gain, wa, ws)


def _perm_heads(a, axis):
    idx = np.concatenate([np.arange(h * HEAD_DIM, (h + 1) * HEAD_DIM) for h in HEAD_ORDER])
    return jnp.take(a, jnp.asarray(idx), axis=axis)


def _ple_kernel(x_ref, p_ref, g_ref, wg_ref, wp_ref, fg_ref, o_ref, *, final):
    x = x_ref[...]
    xn = _rms(x, g_ref[...]).astype(BF16)
    gate = jax.nn.sigmoid(_dot(xn, wg_ref[...]))
    y = x + gate * _dot(p_ref[...].astype(BF16), wp_ref[...])
    if final:
        y = _rms(y, fg_ref[...])
    o_ref[...] = y


def _ple(x, p, gain, wg, wp, final_gain, layer, final):
    t, d = x.shape
    tm = PROJ_TM
    return pl.pallas_call(
        functools.partial(_ple_kernel, final=final),
        grid=(t // tm,),
        in_specs=[
            pl.BlockSpec((tm, d), lambda i: (i, 0)),
            pl.BlockSpec((None, tm, PLE_DIM), lambda i: (layer, i, 0)),
            pl.BlockSpec((None, 1, d), lambda i: (layer, 0, 0)),
            pl.BlockSpec((None, d, d), lambda i: (layer, 0, 0)),
            pl.BlockSpec((None, PLE_DIM, d), lambda i: (layer, 0, 0)),
            pl.BlockSpec((1, d), lambda i: (0, 0)),
        ],
        out_specs=pl.BlockSpec((tm, d), lambda i: (i, 0)),
        out_shape=jax.ShapeDtypeStruct((t, d), F32),
        compiler_params=_cparams("parallel"),
        name="ple",
    )(x, p, gain, wg, wp, final_gain)


def kernel(x, p, ffn1_norm, ffn1_w_in, ffn1_w_out, mix_norm, w_mix_in, cmp_pos, cmp_w1, cmp_b1, cmp_w2,
           cmp_b2, rel_table, nsa_out_norm, conv_w, conv_b, dt_bias, a_log, d_skip, ssm_out_norm, w_mix_out,
           ffn2_norm, ffn2_w_in, ffn2_w_out, ple_norm, ple_gate_w, ple_proj_w, final_norm):
    batch, seq, d = x.shape
    depth = p.shape[0]
    t = batch * seq
    bf = lambda a: a.astype(BF16)
    row = lambda a: a[:, None, :]

    ffn1_in, ffn1_out, ffn2_in, ffn2_out = bf(ffn1_w_in), bf(ffn1_w_out), bf(ffn2_w_in), bf(ffn2_w_out)
    w_proj = _prep_inproj_weight(w_mix_in)
    wo_attn = bf(_perm_heads(w_mix_out[:, :NSA_WIDTH], axis=1))
    wo_ssm = bf(w_mix_out[:, NSA_WIDTH:])
    nsa_gain = row(_perm_heads(nsa_out_norm, axis=1))
    ssd_prm = _prep_ssd_params(conv_w, conv_b, dt_bias, a_log, d_skip, ssm_out_norm)
    ssd_consts = _ssd_consts()
    tb, biasc = _nsa_tables(rel_table, seq)
    ovt = _nsa_consts(seq)
    wg, wp = bf(ple_gate_w), bf(ple_proj_w)
    p2 = p.reshape(depth, t, PLE_DIM)
    fgain = final_norm[None, :]

    h = x.reshape(t, d)
    for i in range(depth):
        h = _ffn(h, row(ffn1_norm), ffn1_in, ffn1_out, i)
        q, kc, vc, kv, misc, z, xbc = _inproj(h, row(mix_norm), w_proj, i)
        cprep = _prep_compress(cmp_pos[i], cmp_w1[i], cmp_b1[i], cmp_w2[i], cmp_b2[i])
        kcmp, vcmp = _compress(kc, vc, cprep, batch, seq)
        o_attn = _nsa_attention(q, kcmp, vcmp, kv, misc, tb, biasc, ovt, batch, seq)
        o_ssm = _ssd(xbc, z, misc, ssd_prm, ssd_consts, i, batch, seq)
        h = _outproj(h, o_attn, o_ssm, nsa_gain, wo_attn, wo_ssm, i)
        h = _ffn(h, row(ffn2_norm), ffn2_in, ffn2_out, i)
        h = _ple(h, p2, row(ple_norm), wg, wp, fgain, i, final=(i == depth - 1))
    return h.reshape(batch, seq, d)
```

```python
import functools
import math

import numpy as np
import jax
import jax.numpy as jnp
from jax import lax
from jax.experimental import pallas as pl
from jax.experimental.pallas import tpu as pltpu

F32 = jnp.float32
BF16 = jnp.bfloat16

D_MODEL = 1024
DEPTH = 4
PLE_DIM = 256
D_FF = 2816
EPS = 1e-6
NEG_INF = -1e30
FORCE_SCORE = 1e4

NSA_HEADS = 8
NSA_KV_GROUPS = 2
NSA_REP = NSA_HEADS // NSA_KV_GROUPS
HEAD_DIM = 64
NSA_WIDTH = NSA_HEADS * HEAD_DIM
KV_WIDTH = NSA_KV_GROUPS * HEAD_DIM
CMP_BLOCK = 32
CMP_STRIDE = 16
CMP_HIDDEN = 256
SEL_BLOCK = 64
SEL_TOPK = 8
WINDOW = 512
REL_BUCKETS = 32
REL_MAX_DIST = 128

SSM_HEADS = 16
SSM_HEAD_DIM = 64
SSM_INNER = SSM_HEADS * SSM_HEAD_DIM
SSM_GROUPS = 2
SSM_STATE = 128
CONV_WIDTH = 4
SSD_CHUNK = 128
CONV_CH = SSM_INNER + 2 * SSM_GROUPS * SSM_STATE

LANES = 128
VMEM_LIMIT_BYTES = 48 * 1024 * 1024

FFN_TM = 1024
FFN_TF = 256
PROJ_TM = 512
Q_TILE = 128
K_TILE = 256
ATTN_UNROLL = 4
HEAD_ORDER = (0, 4, 1, 5, 2, 6, 3, 7)
GATE_LANE0 = 0
DT_LANE0 = 3 * NSA_HEADS
N_BIAS_TABLES = 6
LOG2E = math.log2(math.e)
BF16_SUBLANES = 16
CONV_TAIL = 8


def _dot(a, b):
    return jnp.dot(a, b, preferred_element_type=F32)


def _dot_nt(a, b):
    return lax.dot_general(a, b, (((1,), (1,)), ((), ())), preferred_element_type=F32)


def _split3(v):
    hi = v.astype(BF16)
    r = v - hi.astype(F32)
    mid = r.astype(BF16)
    lo = (r - mid.astype(F32)).astype(BF16)
    return hi, mid, lo


def _dot_f32x2_lhs(v, e):
    hi = v.astype(BF16)
    lo = (v - hi.astype(F32)).astype(BF16)
    return _dot(hi, e) + _dot(lo, e)


def _dot_f32_rhs(e, v):
    hi, mid, lo = _split3(v)
    return _dot(e, hi) + _dot(e, mid) + _dot(e, lo)


def _rms(x, g):
    ms = jnp.mean(x * x, axis=-1, keepdims=True)
    return x * lax.rsqrt(ms + EPS) * g


def _silu(x):
    return x * jax.nn.sigmoid(x)


def _cparams(*sem):
    return pltpu.CompilerParams(dimension_semantics=sem, vmem_limit_bytes=VMEM_LIMIT_BYTES)


def _ffn_kernel(x_ref, g_ref, wg_ref, wu_ref, wo_ref, o_ref, xn_ref, acc_ref):
    j = pl.program_id(1)

    @pl.when(j == 0)
    def _():
        xn_ref[...] = _rms(x_ref[...], g_ref[...]).astype(BF16)
        acc_ref[...] = jnp.zeros_like(acc_ref)

    xn = xn_ref[...]
    gate = _dot(xn, wg_ref[...])
    up = _dot(xn, wu_ref[...])
    h = (_silu(gate) * up).astype(BF16)
    acc_ref[...] += _dot(h, wo_ref[...])

    @pl.when(j == pl.num_programs(1) - 1)
    def _():
        o_ref[...] = x_ref[...] + 0.5 * acc_ref[...]


def _ffn(x, gain, w_in, w_out, layer):
    t, d = x.shape
    nf = D_FF // FFN_TF
    return pl.pallas_call(
        _ffn_kernel,
        grid=(t // FFN_TM, nf),
        in_specs=[
            pl.BlockSpec((FFN_TM, d), lambda i, j: (i, 0)),
            pl.BlockSpec((None, 1, d), lambda i, j: (layer, 0, 0)),
            pl.BlockSpec((None, d, FFN_TF), lambda i, j: (layer, 0, j)),
            pl.BlockSpec((None, d, FFN_TF), lambda i, j: (layer, 0, j + nf)),
            pl.BlockSpec((None, FFN_TF, d), lambda i, j: (layer, j, 0)),
        ],
        out_specs=pl.BlockSpec((FFN_TM, d), lambda i, j: (i, 0)),
        out_shape=jax.ShapeDtypeStruct((t, d), F32),
        scratch_shapes=[pltpu.VMEM((FFN_TM, d), BF16), pltpu.VMEM((FFN_TM, d), F32)],
        compiler_params=_cparams("parallel", "arbitrary"),
        name="ffn",
    )(x, gain, w_in, w_in, w_out)


_C_Q = (0, NSA_WIDTH)
_C_KC = (_C_Q[1], _C_Q[1] + KV_WIDTH)
_C_VC = (_C_KC[1], _C_KC[1] + KV_WIDTH)
_C_KV = (_C_VC[1], _C_VC[1] + 4 * KV_WIDTH)
_C_MISC = (_C_KV[1], _C_KV[1] + LANES)
_C_Z = (_C_MISC[1], _C_MISC[1] + SSM_INNER)
_C_XBC = (_C_Z[1], _C_Z[1] + CONV_CH)
PROJ_COLS = _C_XBC[1]


def _inproj_kernel(x_ref, g_ref, w_ref, q_ref, kc_ref, vc_ref, kv_ref, misc_ref, z_ref, xbc_ref):
    xn = _rms(x_ref[...], g_ref[...]).astype(BF16)
    q_ref[...] = _dot(xn, w_ref[:, _C_Q[0]:_C_Q[1]])
    kc_ref[...] = _dot(xn, w_ref[:, _C_KC[0]:_C_KC[1]])
    vc_ref[...] = _dot(xn, w_ref[:, _C_VC[0]:_C_VC[1]])
    kv_ref[...] = _dot(xn, w_ref[:, _C_KV[0]:_C_KV[1]]).astype(BF16)
    misc_ref[...] = _dot(xn, w_ref[:, _C_MISC[0]:_C_MISC[1]])
    z_ref[...] = _dot(xn, w_ref[:, _C_Z[0]:_C_Z[1]])
    xbc_ref[...] = _dot(xn, w_ref[:, _C_XBC[0]:_C_XBC[1]])


def _inproj(x, gain, w, layer):
    t, d = x.shape
    tm = PROJ_TM
    widths = (NSA_WIDTH, KV_WIDTH, KV_WIDTH, 4 * KV_WIDTH, LANES, SSM_INNER, CONV_CH)
    dtypes = (F32, F32, F32, BF16, F32, F32, F32)
    return pl.pallas_call(
        _inproj_kernel,
        grid=(t // tm,),
        in_specs=[
            pl.BlockSpec((tm, d), lambda i: (i, 0)),
            pl.BlockSpec((None, 1, d), lambda i: (layer, 0, 0)),
            pl.BlockSpec((None, d, PROJ_COLS), lambda i: (layer, 0, 0)),
        ],
        out_specs=[pl.BlockSpec((tm, n), lambda i: (i, 0)) for n in widths],
        out_shape=[jax.ShapeDtypeStruct((t, n), dt) for n, dt in zip(widths, dtypes)],
        compiler_params=_cparams("parallel"),
        name="inproj",
    )(x, gain, w)


_IN_OFFS = tuple(int(v) for v in np.cumsum(
    (0, NSA_WIDTH) + (KV_WIDTH,) * 6 + (3 * NSA_HEADS, SSM_INNER, CONV_CH, SSM_HEADS)))
PREP_ROWS = 256


def _prep_inproj_kernel(w_ref, o_ref):
    o = _IN_OFFS
    o_ref[:, _C_KC[0]:_C_KV[1]] = w_ref[:, o[1]:o[7]].astype(BF16)
    o_ref[:, _C_Z[0]:_C_XBC[1]] = w_ref[:, o[8]:o[10]].astype(BF16)
    for pos, h in enumerate(HEAD_ORDER):
        o_ref[:, pos * HEAD_DIM:(pos + 1) * HEAD_DIM] = w_ref[:, h * HEAD_DIM:(h + 1) * HEAD_DIM].astype(BF16)
    ngate, ndt = o[8] - o[7], o[11] - o[10]
    m0 = _C_MISC[0]
    o_ref[:, m0:m0 + ngate] = w_ref[:, o[7]:o[8]].astype(BF16)
    o_ref[:, m0 + ngate:m0 + ngate + ndt] = w_ref[:, o[10]:o[11]].astype(BF16)
    o_ref[:, m0 + ngate + ndt:_C_MISC[1]] = jnp.zeros((o_ref.shape[0], LANES - ngate - ndt), BF16)


def _prep_inproj_weight(w_mix_in):
    nl, d, win = w_mix_in.shape
    return pl.pallas_call(
        _prep_inproj_kernel,
        grid=(nl, d // PREP_ROWS),
        in_specs=[pl.BlockSpec((None, PREP_ROWS, win), lambda l, r: (l, r, 0))],
        out_specs=pl.BlockSpec((None, PREP_ROWS, PROJ_COLS), lambda l, r: (l, r, 0)),
        out_shape=jax.ShapeDtypeStruct((nl, d, PROJ_COLS), BF16),
        compiler_params=_cparams("parallel", "parallel"),
        name="inproj_weight_layout",
    )(w_mix_in)


def _compress_kernel(kc_ref, vc_ref, pos_ref, w1a_ref, w1b_ref, b1_ref, w2_ref, b2_ref,
                     kcmp_ref, vcmpt_ref):
    nrow = kc_ref.shape[0]
    for which, src in enumerate((kc_ref, vc_ref)):
        r = src[...]
        ra = (r + pos_ref[which, 0]).astype(BF16)
        rb = (r + pos_ref[which, 1]).astype(BF16)
        ha = _dot(ra, w1a_ref[which])
        hb = _dot(rb, w1b_ref[which])
        h = ha + pltpu.roll(hb, nrow - 1, 0) + b1_ref[which]
        out = _dot(_silu(h).astype(BF16), w2_ref[which]) + b2_ref[which]
        if which == 0:
            kcmp_ref[...] = out.astype(BF16)
        else:
            vcmpt_ref[...] = out.T.astype(BF16)


def _compress(kc, vc, prep, layer, batch, seq):
    nrow = seq // CMP_STRIDE
    wide = CMP_STRIDE * KV_WIDTH
    kcr = kc.reshape(batch, nrow, wide)
    vcr = vc.reshape(batch, nrow, wide)
    lay = lambda a: pl.BlockSpec((None,) + a.shape[1:], lambda b: (layer,) + (0,) * (a.ndim - 1))
    consts = (prep["pos"], prep["w1a"], prep["w1b"], prep["b1"], prep["w2"], prep["b2"])
    return pl.pallas_call(
        _compress_kernel,
        grid=(batch,),
        in_specs=[pl.BlockSpec((None, nrow, wide), lambda b: (b, 0, 0))] * 2 + [lay(a) for a in consts],
        out_specs=[pl.BlockSpec((None, nrow, KV_WIDTH), lambda b: (b, 0, 0)),
                   pl.BlockSpec((None, KV_WIDTH, nrow), lambda b: (b, 0, 0))],
        out_shape=[jax.ShapeDtypeStruct((batch, nrow, KV_WIDTH), BF16),
                   jax.ShapeDtypeStruct((batch, KV_WIDTH, nrow), BF16)],
        compiler_params=_cparams("parallel"),
        name="nsa_compress",
    )(kcr, vcr, *consts)


def _prep_compress(cmp_pos, cmp_w1, cmp_b1, cmp_w2, cmp_b2):
    assert NSA_KV_GROUPS == 2
    nl = cmp_w1.shape[0]
    half = CMP_BLOCK // 2
    w1 = cmp_w1.astype(BF16).reshape(nl, 2, CMP_BLOCK, HEAD_DIM, CMP_HIDDEN)

    def block_diag(w, axis):
        z = jnp.zeros_like(w)
        return jnp.stack([jnp.concatenate([w, z], axis=-1), jnp.concatenate([z, w], axis=-1)], axis=axis)

    def expand(w):
        return block_diag(w, 3).reshape(nl, 2, half * 2 * HEAD_DIM, 2 * CMP_HIDDEN)

    def pos_rows(p):
        return jnp.broadcast_to(p[:, :, :, None, :], (nl, 2, half, 2, HEAD_DIM)).reshape(nl, 2, 1, half * 2 * HEAD_DIM)

    w2 = block_diag(cmp_w2.astype(BF16), 2).reshape(nl, 2, 2 * CMP_HIDDEN, 2 * HEAD_DIM)
    return {
        "pos": jnp.stack([pos_rows(cmp_pos[:, :, :half]), pos_rows(cmp_pos[:, :, half:])], axis=2),
        "w1a": expand(w1[:, :, :half]),
        "w1b": expand(w1[:, :, half:]),
        "b1": jnp.tile(cmp_b1, (1, 1, 2))[:, :, None, :],
        "w2": w2,
        "b2": jnp.tile(cmp_b2, (1, 1, 2))[:, :, None, :],
    }


def _t5_bucket_np(dist):
    n = np.maximum(dist, 0)
    exact = REL_BUCKETS // 2
    nf = np.maximum(n, exact).astype(np.float64)
    large = exact + (np.log(nf / exact) / math.log(REL_MAX_DIST / exact) * (REL_BUCKETS - exact)).astype(np.int64)
    return np.where(n < exact, n, np.minimum(large, REL_BUCKETS - 1)).astype(np.int32)


MASKED_BUCKET = REL_BUCKETS


def _bucket_maps(seq):
    j = np.arange(K_TILE)[:, None]
    i = np.arange(Q_TILE)[None, :]
    tiles = []
    for delta in range(N_BIAS_TABLES):
        d = Q_TILE * delta + i - j
        valid = (d >= 0) & ((d < WINDOW) if delta >= 4 else True)
        tiles.append(np.where(valid, _t5_bucket_np(d), MASKED_BUCKET))
    c = np.arange(seq // CMP_STRIDE)[:, None]
    t = np.arange(seq)[None, :]
    dc = t - (c * CMP_STRIDE + CMP_BLOCK - 1)
    cmp_map = np.where(dc >= 0, _t5_bucket_np(dc), MASKED_BUCKET)
    return np.stack(tiles).astype(np.int32), cmp_map.astype(np.int32)


def _tables_kernel(tab_ref, bkt_ref, bkc_ref, tb_ref, bc_ref):
    h = pl.program_id(0)
    for src, dst in ((bkt_ref, tb_ref), (bkc_ref, bc_ref)):
        bk = src[...]
        out = jnp.zeros(bk.shape, F32)
        for b in range(REL_BUCKETS + 1):
            out = jnp.where(bk == b, tab_ref[b, h], out)
        dst[...] = (out * LOG2E).astype(dst.dtype)


def _nsa_tables(rel_table, seq):
    bkt, bkc = _bucket_maps(seq)
    tab = jnp.concatenate([rel_table[:, np.asarray(HEAD_ORDER)],
                           jnp.full((1, NSA_HEADS), NEG_INF, F32)], axis=0)
    nc = seq // CMP_STRIDE
    return pl.pallas_call(
        _tables_kernel,
        grid=(NSA_HEADS,),
        in_specs=[
            pl.BlockSpec(memory_space=pltpu.SMEM),
            pl.BlockSpec(bkt.shape, lambda h: (0, 0, 0)),
            pl.BlockSpec(bkc.shape, lambda h: (0, 0)),
        ],
        out_specs=[pl.BlockSpec((N_BIAS_TABLES, None, K_TILE, Q_TILE), lambda h: (0, h, 0, 0)),
                   pl.BlockSpec((None, nc, seq), lambda h: (h, 0, 0))],
        out_shape=[jax.ShapeDtypeStruct((N_BIAS_TABLES, NSA_HEADS, K_TILE, Q_TILE), BF16),
                   jax.ShapeDtypeStruct((NSA_HEADS, nc, seq), F32)],
        compiler_params=_cparams("parallel"),
        name="nsa_bias_tables",
    )(tab, jnp.asarray(bkt), jnp.asarray(bkc))


def _nsa_consts(seq):
    nc = seq // CMP_STRIDE
    nb = seq // SEL_BLOCK
    c = np.arange(nc)
    blk = np.arange(nb)
    c_lo, c_hi = c * CMP_STRIDE, c * CMP_STRIDE + CMP_BLOCK - 1
    s_lo, s_hi = blk * SEL_BLOCK, blk * SEL_BLOCK + SEL_BLOCK - 1
    ovt = (c_lo[None, :] <= s_hi[:, None]) & (c_hi[None, :] >= s_lo[:, None])
    ovt[:, nc - 1] = False
    return jnp.asarray(ovt, BF16)


def _nsa_kernel(q_ref, kcmp_ref, vcmpt_ref, ks_ref, vs_ref, kw_ref, vw_ref, misc_ref, biasc_ref, tb_ref,
                ovt_ref, o_ref, qpt_ref, vst_ref, vwt_ref, madd_ref, m_ref, acc_ref, mw_ref, accw_ref, ocmp_ref):
    n = pl.program_id(1)
    nkt = ks_ref.shape[0] // K_TILE
    nsel_blocks = ovt_ref.shape[0]
    nhb = NSA_HEADS
    cols = lambda hb: slice(hb * Q_TILE, (hb + 1) * Q_TILE)
    frow = lax.broadcasted_iota(jnp.int32, (KV_WIDTH, Q_TILE), 0)
    low = frow < HEAD_DIM

    @pl.when(n == 0)
    def _():
        arow = lax.broadcasted_iota(jnp.int32, (BF16_SUBLANES, K_TILE), 0)
        ones_row = jnp.where(arow == 0, 1.0, 0.0).astype(BF16)
        for t2 in range(nkt):
            rows = slice(t2 * K_TILE, (t2 + 1) * K_TILE)
            for src, dst in ((vs_ref, vst_ref), (vw_ref, vwt_ref)):
                dst[t2, 0:KV_WIDTH, :] = src[rows, :].astype(F32).T.astype(BF16)
                dst[t2, KV_WIDTH:, :] = ones_row

    scale = HEAD_DIM ** -0.5 * LOG2E
    for j in range(nhb // 2):
        slab = (q_ref[:, j * LANES:(j + 1) * LANES].T * scale).astype(BF16)
        zero = jnp.zeros_like(slab)
        qpt_ref[:, cols(2 * j)] = jnp.where(low, slab, zero)
        qpt_ref[:, cols(2 * j + 1)] = jnp.where(low, zero, slab)
    qpt = qpt_ref[...]

    pair_cols = lambda pair: slice(2 * pair * Q_TILE, (2 * pair + 2) * Q_TILE)

    def branch(k_ref, vt_ref, mx_ref, ac_ref, selected):
        def scores(t2):
            start = pl.multiple_of(t2 * K_TILE, K_TILE)
            k = k_ref[pl.ds(start, K_TILE), :]
            return [_dot(k, qpt_ref[:, pair_cols(pair)]) for pair in range(nhb // 2)]

        def softmax_pv(t2, s_pairs):
            vt = vt_ref[t2]
            delta = n - 2 * t2
            ti = jnp.minimum(delta, 3) if selected else delta
            blk0 = t2 * (K_TILE // SEL_BLOCK)
            for pair in range(nhb // 2):
                c2 = pair_cols(pair)
                s2 = s_pairs[pair]
                ps, alphas = [], []
                for half in range(2):
                    hb = 2 * pair + half
                    s = s2[:, half * Q_TILE:(half + 1) * Q_TILE].astype(BF16) + tb_ref[ti, hb]
                    if selected:
                        s = jnp.concatenate(
                            [s[j * SEL_BLOCK:(j + 1) * SEL_BLOCK]
                             + madd_ref[half, pl.ds(blk0 + j, 1), :].astype(BF16)
                             for j in range(K_TILE // SEL_BLOCK)], axis=0)
                    m_prev = mx_ref[:, cols(hb)]
                    m_new = jnp.maximum(m_prev, jnp.max(s, axis=0, keepdims=True).astype(F32))
                    mx_ref[:, cols(hb)] = m_new
                    alphas.append(jnp.exp2(m_prev - m_new))
                    ps.append(jnp.exp2(s - m_new.astype(BF16)))
                pv = _dot(vt, jnp.concatenate(ps, axis=1))
                ac_ref[:, c2] = ac_ref[:, c2] * jnp.concatenate(alphas, axis=1) + pv

        def issue(first, k):
            return [scores(first + i) for i in range(k)]

        def finish(first, ss):
            for i, s_pairs in enumerate(ss):
                softmax_pv(first + i, s_pairs)

        def init():
            mx_ref[...] = jnp.full_like(mx_ref, -jnp.inf)
            ac_ref[...] = jnp.zeros_like(ac_ref)

        def result():
            acc = ac_ref[...]
            return acc[0:KV_WIDTH] / acc[KV_WIDTH:KV_WIDTH + 1]

        return init, issue, finish, result

    sel_init, sel_issue, sel_finish, sel_result = branch(ks_ref, vst_ref, m_ref, acc_ref, True)
    win_init, win_issue, win_finish, win_result = branch(kw_ref, vwt_ref, mw_ref, accw_ref, False)

    tq = n * Q_TILE + lax.broadcasted_iota(jnp.int32, (1, Q_TILE), 1)

    def compressed_and_select(win_first=None, win_k=0):
        has_block = (tq >= CMP_BLOCK - 1).astype(F32)
        sc = _dot(kcmp_ref[...], qpt)
        win_scores = win_issue(win_first, win_k)
        psum = [None, None]
        pcs = []
        for hb in range(nhb):
            s = sc[:, cols(hb)] + biasc_ref[hb]
            m = jnp.max(s, axis=0, keepdims=True)
            e = jnp.exp2(s - m)
            p = e / jnp.sum(e, axis=0, keepdims=True) * has_block
            pcs.append(p.astype(BF16))
            g = hb % 2
            psum[g] = p if psum[g] is None else psum[g] + p
        ocmp_ref[...] = _dot(vcmpt_ref[...], jnp.concatenate(pcs, axis=1))

        blk = lax.broadcasted_iota(jnp.int32, (nsel_blocks, Q_TILE), 0)
        blk_f = blk.astype(F32)
        cur = tq // SEL_BLOCK
        forced = ((blk == 0) | (blk == cur) | (blk == cur - 1)).astype(F32)
        for g in range(NSA_KV_GROUPS):
            imp = _dot_f32_rhs(ovt_ref[...], psum[g])
            score = jnp.where(blk <= cur, imp + FORCE_SCORE * forced, -FORCE_SCORE)
            sel = jnp.zeros(score.shape, F32)
            for _ in range(SEL_TOPK):
                mx = jnp.max(score, axis=0, keepdims=True)
                first = jnp.min(jnp.where(score == mx, blk_f, float(nsel_blocks)), axis=0, keepdims=True)
                hit = blk_f == first
                sel = jnp.where(hit, 1.0, sel)
                score = jnp.where(hit, -jnp.inf, score)
            madd_ref[g] = (sel - 1.0) * (-NEG_INF)
        win_finish(win_first, win_scores)

    t2_diag = n // 2
    win_full = WINDOW // K_TILE + 1
    win_init()

    @pl.when(t2_diag >= win_full - 1)
    def _():
        compressed_and_select(t2_diag - (win_full - 1), win_full)

    @pl.when(t2_diag < win_full - 1)
    def _():
        compressed_and_select()
        for k in range(1, win_full):
            @pl.when(t2_diag + 1 == k)
            def _(k=k):
                win_finish(0, win_issue(0, k))

    sel_init()
    count = t2_diag + 1

    def full_blocks(i, carry):
        sel_finish(ATTN_UNROLL * i, sel_issue(ATTN_UNROLL * i, ATTN_UNROLL))
        return carry

    lax.fori_loop(0, count // ATTN_UNROLL, full_blocks, 0)
    rest_first = (count // ATTN_UNROLL) * ATTN_UNROLL
    for k in range(1, ATTN_UNROLL):
        @pl.when(count % ATTN_UNROLL == k)
        def _(k=k):
            sel_finish(rest_first, sel_issue(rest_first, k))

    o_cmp = ocmp_ref[...]
    o_sel = sel_result()
    o_win = win_result()

    gates = jax.nn.sigmoid(misc_ref[...]).T

    def gate(branch, hb):
        col = GATE_LANE0 + branch * NSA_HEADS + HEAD_ORDER[hb]
        return gates[col:col + 1, :]

    for j in range(nhb // 2):
        outs = []
        for hb in (2 * j, 2 * j + 1):
            outs.append(gate(0, hb) * o_cmp[:, cols(hb)] + gate(1, hb) * o_sel[:, cols(hb)]
                        + gate(2, hb) * o_win[:, cols(hb)])
        o_ref[:, j * LANES:(j + 1) * LANES] = jnp.where(low, outs[0], outs[1]).T


def _nsa_attention(q, kcmp, vcmpt, kv, misc, tb, biasc, ovt, batch, seq):
    nq = seq // Q_TILE
    nc = seq // CMP_STRIDE
    nkt = seq // K_TILE
    cols_all = NSA_HEADS * Q_TILE
    vrows = KV_WIDTH + BF16_SUBLANES
    kvspec = lambda c: pl.BlockSpec((seq, KV_WIDTH), lambda b, n: (b, c))
    return pl.pallas_call(
        _nsa_kernel,
        grid=(batch, nq),
        in_specs=[
            pl.BlockSpec((Q_TILE, NSA_WIDTH), lambda b, n: (b * nq + n, 0)),
            pl.BlockSpec((None, nc, KV_WIDTH), lambda b, n: (b, 0, 0)),
            pl.BlockSpec((None, KV_WIDTH, nc), lambda b, n: (b, 0, 0)),
            kvspec(0), kvspec(1), kvspec(2), kvspec(3),
            pl.BlockSpec((Q_TILE, LANES), lambda b, n: (b * nq + n, 0)),
            pl.BlockSpec((NSA_HEADS, nc, Q_TILE), lambda b, n: (0, 0, n)),
            pl.BlockSpec(tb.shape, lambda b, n: (0, 0, 0, 0)),
            pl.BlockSpec(ovt.shape, lambda b, n: (0, 0)),
        ],
        out_specs=pl.BlockSpec((Q_TILE, NSA_WIDTH), lambda b, n: (b * nq + n, 0)),
        out_shape=jax.ShapeDtypeStruct((batch * seq, NSA_WIDTH), F32),
        scratch_shapes=[
            pltpu.VMEM((KV_WIDTH, cols_all), BF16),
            pltpu.VMEM((nkt, vrows, K_TILE), BF16),
            pltpu.VMEM((nkt, vrows, K_TILE), BF16),
            pltpu.VMEM((NSA_KV_GROUPS, seq // SEL_BLOCK, Q_TILE), F32),
            pltpu.VMEM((1, cols_all), F32),
            pltpu.VMEM((vrows, cols_all), F32),
            pltpu.VMEM((1, cols_all), F32),
            pltpu.VMEM((vrows, cols_all), F32),
            pltpu.VMEM((KV_WIDTH, cols_all), F32),
        ],
        compiler_params=_cparams("arbitrary", "arbitrary"),
        name="nsa_attention",
    )(q, kcmp, vcmpt, kv, kv, kv, kv, misc, biasc, tb, ovt)


def _softplus(x):
    return jnp.maximum(x, 0.0) + jnp.log1p(jnp.exp(-jnp.abs(x)))


def _ssd_kernel(xbc_ref, z_ref, misc_ref, convw_ref, convb_ref, dtb_ref, alog_ref, dskip_ref, gain_ref,
                tri_ref, e1_ref, o_ref, prev_ref, h_ref):
    c = pl.program_id(1)
    L = SSD_CHUNK
    gw = SSM_INNER // SSM_GROUPS
    hpg = SSM_HEADS // SSM_GROUPS

    @pl.when(c == 0)
    def _():
        prev_ref[...] = jnp.zeros_like(prev_ref)
        h_ref[...] = jnp.zeros_like(h_ref)

    x = xbc_ref[...]
    prev_ref[CONV_TAIL:, :] = x
    acc = convb_ref[...] + x * convw_ref[CONV_WIDTH - 1:CONV_WIDTH, :]
    for k in range(1, CONV_WIDTH):
        xk = prev_ref[CONV_TAIL - k:CONV_TAIL - k + L, :]
        acc = acc + xk * convw_ref[CONV_WIDTH - 1 - k:CONV_WIDTH - k, :]
    prev_ref[0:CONV_TAIL, :] = x[L - CONV_TAIL:L]
    xa = _silu(acc)
    xs = xa[:, :SSM_INNER]
    bm = xa[:, SSM_INNER:SSM_INNER + SSM_GROUPS * SSM_STATE]
    cm = xa[:, SSM_INNER + SSM_GROUPS * SSM_STATE:]

    dt = _softplus(misc_ref[...] + dtb_ref[...])
    da = dt * (-jnp.exp(alog_ref[...]))
    cs = _dot_f32_rhs(tri_ref[...], da) * LOG2E
    cs_t = cs.T
    dt_t = dt.T
    ecs = _dot_f32x2_lhs(jnp.exp2(cs), e1_ref[...])
    to_end = dt * jnp.exp2(cs[L - 1:L, :] - cs)
    xw_b = (xs * _dot_f32x2_lhs(to_end, e1_ref[...])).astype(BF16)
    xs_b = xs.astype(BF16)
    state_decay = ecs[L - 1:L, :]

    li = lax.broadcasted_iota(jnp.int32, (L, L), 0)
    si = lax.broadcasted_iota(jnp.int32, (L, L), 1)
    causal = li >= si
    low = si < SSM_HEAD_DIM

    ys = []
    for g in range(SSM_GROUPS):
        bg = bm[:, g * SSM_STATE:(g + 1) * SSM_STATE]
        cg = cm[:, g * SSM_STATE:(g + 1) * SSM_STATE].astype(BF16)
        cb = _dot_nt(cg, bg.astype(BF16))
        h_g = h_ref[:, g * gw:(g + 1) * gw]
        y_off = _dot(cg, h_g.astype(BF16)) * ecs[:, g * gw:(g + 1) * gw]
        for pr in range(hpg // 2):
            h0 = g * hpg + 2 * pr
            gs = []
            for hh in (h0, h0 + 1):
                ln = DT_LANE0 + hh
                col = jnp.broadcast_to(cs[:, ln:ln + 1], (L, L))
                dec = jnp.exp2(jnp.where(causal, col - cs_t[ln:ln + 1, :], NEG_INF))
                gs.append((cb * dec * dt_t[ln:ln + 1, :]).astype(BF16))
            ch = slice(h0 * SSM_HEAD_DIM, (h0 + 2) * SSM_HEAD_DIM)
            xpair = xs_b[:, ch]
            zero = jnp.zeros_like(xpair)
            rhs = jnp.concatenate([jnp.where(low, xpair, zero), jnp.where(low, zero, xpair)], axis=0)
            y_diag = _dot(jnp.concatenate(gs, axis=1), rhs)
            off = slice(2 * pr * SSM_HEAD_DIM, (2 * pr + 2) * SSM_HEAD_DIM)
            ys.append(y_diag + y_off[:, off] + xs[:, ch] * dskip_ref[:, ch])
        st = _dot(bg.T.astype(BF16), xw_b[:, g * gw:(g + 1) * gw])
        h_ref[:, g * gw:(g + 1) * gw] = h_g * state_decay[:, g * gw:(g + 1) * gw] + st

    y = jnp.concatenate(ys, axis=1) * _silu(z_ref[...])
    outs = []
    for g in range(SSM_GROUPS):
        outs.append(_rms(y[:, g * gw:(g + 1) * gw], gain_ref[:, g * gw:(g + 1) * gw]))
    o_ref[...] = jnp.concatenate(outs, axis=1).astype(BF16)


def _ssd_consts():
    lane = np.arange(LANES)
    tri = (np.arange(SSD_CHUNK)[:, None] >= np.arange(SSD_CHUNK)[None, :])
    head1 = np.arange(SSM_INNER) // SSM_HEAD_DIM
    e1 = (lane[:, None] - DT_LANE0) == head1[None, :]
    return jnp.asarray(tri, BF16), jnp.asarray(e1, BF16)


def _ssd(xbc, z, misc, prm, consts, layer, batch, seq):
    nch = seq // SSD_CHUNK
    tok = lambda n: pl.BlockSpec((SSD_CHUNK, n), lambda b, c: (b * nch + c, 0))
    lay = lambda a: pl.BlockSpec((None,) + a.shape[1:], lambda b, c: (layer,) + (0,) * (a.ndim - 1))
    full = lambda a: pl.BlockSpec(a.shape, lambda b, c: (0,) * a.ndim)
    params = (prm["conv_w"], prm["conv_b"], prm["dt_bias"], prm["a_log"], prm["d_skip"], prm["ssm_gain"])
    return pl.pallas_call(
        _ssd_kernel,
        grid=(batch, nch),
        in_specs=[tok(CONV_CH), tok(SSM_INNER), tok(LANES)] + [lay(a) for a in params] + [full(a) for a in consts],
        out_specs=tok(SSM_INNER),
        out_shape=jax.ShapeDtypeStruct((batch * seq, SSM_INNER), BF16),
        scratch_shapes=[pltpu.VMEM((CONV_TAIL + SSD_CHUNK, CONV_CH), F32), pltpu.VMEM((SSM_STATE, SSM_INNER), F32)],
        compiler_params=_cparams("parallel", "arbitrary"),
        name="ssd",
    )(xbc, z, misc, *params, *consts)


def _prep_ssd_params(conv_w, conv_b, dt_bias, a_log, d_skip, ssm_out_norm):
    nl = conv_w.shape[0]

    def dt_lanes(v):
        out = jnp.zeros((nl, 1, LANES), F32)
        return out.at[:, 0, DT_LANE0:DT_LANE0 + SSM_HEADS].set(v)

    return {
        "conv_w": conv_w,
        "conv_b": conv_b[:, None, :],
        "dt_bias": dt_lanes(dt_bias),
        "a_log": dt_lanes(a_log),
        "d_skip": jnp.repeat(d_skip, SSM_HEAD_DIM, axis=-1)[:, None, :],
        "ssm_gain": ssm_out_norm[:, None, :],
    }


def _outproj_kernel(x_ref, oa_ref, os_ref, g_ref, wa_ref, ws_ref, o_ref):
    an = _rms(oa_ref[...], g_ref[...]).astype(BF16)
    o_ref[...] = x_ref[...] + _dot(an, wa_ref[...]) + _dot(os_ref[...], ws_ref[...])


def _outproj(x, o_attn, o_ssm, gain, wa, ws, layer):
    t, d = x.shape
    tm = PROJ_TM
    return pl.pallas_call(
        _outproj_kernel,
        grid=(t // tm,),
        in_specs=[
            pl.BlockSpec((tm, d), lambda i: (i, 0)),
            pl.BlockSpec((tm, NSA_WIDTH), lambda i: (i, 0)),
            pl.BlockSpec((tm, SSM_INNER), lambda i: (i, 0)),
            pl.BlockSpec((None, 1, NSA_WIDTH), lambda i: (layer, 0, 0)),
            pl.BlockSpec((None, NSA_WIDTH, d), lambda i: (layer, 0, 0)),
            pl.BlockSpec((None, SSM_INNER, d), lambda i: (layer, 0, 0)),
        ],
        out_specs=pl.BlockSpec((tm, d), lambda i: (i, 0)),
        out_shape=jax.ShapeDtypeStruct((t, d), F32),
        compiler_params=_cparams("parallel"),
        name="outproj",
    )(x, o_attn, o_ssm, gain, wa, ws)


def _perm_heads(a, axis):
    idx = np.concatenate([np.arange(h * HEAD_DIM, (h + 1) * HEAD_DIM) for h in HEAD_ORDER])
    return jnp.take(a, jnp.asarray(idx), axis=axis)


def _ple_kernel(x_ref, p_ref, g_ref, wg_ref, wp_ref, fg_ref, o_ref, *, final):
    x = x_ref[...]
    xn = _rms(x, g_ref[...]).astype(BF16)
    gate = jax.nn.sigmoid(_dot(xn, wg_ref[...]))
    y = x + gate * _dot(p_ref[...].astype(BF16), wp_ref[...])
    if final:
        y = _rms(y, fg_ref[...])
    o_ref[...] = y


def _ple(x, p, gain, wg, wp, final_gain, layer, final):
    t, d = x.shape
    tm = PROJ_TM
    return pl.pallas_call(
        functools.partial(_ple_kernel, final=final),
        grid=(t // tm,),
        in_specs=[
            pl.BlockSpec((tm, d), lambda i: (i, 0)),
            pl.BlockSpec((None, tm, PLE_DIM), lambda i: (layer, i, 0)),
            pl.BlockSpec((None, 1, d), lambda i: (layer, 0, 0)),
            pl.BlockSpec((None, d, d), lambda i: (layer, 0, 0)),
            pl.BlockSpec((None, PLE_DIM, d), lambda i: (layer, 0, 0)),
            pl.BlockSpec((1, d), lambda i: (0, 0)),
        ],
        out_specs=pl.BlockSpec((tm, d), lambda i: (i, 0)),
        out_shape=jax.ShapeDtypeStruct((t, d), F32),
        compiler_params=_cparams("parallel"),
        name="ple",
    )(x, p, gain, wg, wp, final_gain)


def kernel(x, p, ffn1_norm, ffn1_w_in, ffn1_w_out, mix_norm, w_mix_in, cmp_pos, cmp_w1, cmp_b1, cmp_w2,
           cmp_b2, rel_table, nsa_out_norm, conv_w, conv_b, dt_bias, a_log, d_skip, ssm_out_norm, w_mix_out,
           ffn2_norm, ffn2_w_in, ffn2_w_out, ple_norm, ple_gate_w, ple_proj_w, final_norm):
    batch, seq, d = x.shape
    depth = p.shape[0]
    t = batch * seq
    bf = lambda a: a.astype(BF16)
    row = lambda a: a[:, None, :]

    ffn1_in, ffn1_out, ffn2_in, ffn2_out = bf(ffn1_w_in), bf(ffn1_w_out), bf(ffn2_w_in), bf(ffn2_w_out)
    w_proj = _prep_inproj_weight(w_mix_in)
    wo_attn = bf(_perm_heads(w_mix_out[:, :NSA_WIDTH], axis=1))
    wo_ssm = bf(w_mix_out[:, NSA_WIDTH:])
    nsa_gain = row(_perm_heads(nsa_out_norm, axis=1))
    ssd_prm = _prep_ssd_params(conv_w, conv_b, dt_bias, a_log, d_skip, ssm_out_norm)
    ssd_consts = _ssd_consts()
    tb, biasc = _nsa_tables(rel_table, seq)
    ovt = _nsa_consts(seq)
    cprep = _prep_compress(cmp_pos, cmp_w1, cmp_b1, cmp_w2, cmp_b2)
    wg, wp = bf(ple_gate_w), bf(ple_proj_w)
    p2 = p.reshape(depth, t, PLE_DIM)
    fgain = final_norm[None, :]

    h = x.reshape(t, d)
    for i in range(depth):
        h = _ffn(h, row(ffn1_norm), ffn1_in, ffn1_out, i)
        q, kc, vc, kv, misc, z, xbc = _inproj(h, row(mix_norm), w_proj, i)
        kcmp, vcmp = _compress(kc, vc, cprep, i, batch, seq)
        o_attn = _nsa_attention(q, kcmp, vcmp, kv, misc, tb, biasc, ovt, batch, seq)
        o_ssm = _ssd(xbc, z, misc, ssd_prm, ssd_consts, i, batch, seq)
        h = _outproj(h, o_attn, o_ssm, nsa_gain, wo_attn, wo_ssm, i)
        h = _ffn(h, row(ffn2_norm), ffn2_in, ffn2_out, i)
        h = _ple(h, p2, row(ple_norm), wg, wp, fgain, i, final=(i == depth - 1))
    return h.reshape(batch, seq, d)
```

```python
import functools
import math

import numpy as np
import jax
import jax.numpy as jnp
from jax import lax
from jax.experimental import pallas as pl
from jax.experimental.pallas import tpu as pltpu

F32 = jnp.float32
BF16 = jnp.bfloat16

D_MODEL = 1024
DEPTH = 4
PLE_DIM = 256
D_FF = 2816
EPS = 1e-6
NEG_INF = -1e30
FORCE_SCORE = 1e4

NSA_HEADS = 8
NSA_KV_GROUPS = 2
NSA_REP = NSA_HEADS // NSA_KV_GROUPS
HEAD_DIM = 64
NSA_WIDTH = NSA_HEADS * HEAD_DIM
KV_WIDTH = NSA_KV_GROUPS * HEAD_DIM
CMP_BLOCK = 32
CMP_STRIDE = 16
CMP_HIDDEN = 256
SEL_BLOCK = 64
SEL_TOPK = 8
WINDOW = 512
REL_BUCKETS = 32
REL_MAX_DIST = 128

SSM_HEADS = 16
SSM_HEAD_DIM = 64
SSM_INNER = SSM_HEADS * SSM_HEAD_DIM
SSM_GROUPS = 2
SSM_STATE = 128
CONV_WIDTH = 4
SSD_CHUNK = 128
CONV_CH = SSM_INNER + 2 * SSM_GROUPS * SSM_STATE

LANES = 128
VMEM_LIMIT_BYTES = 48 * 1024 * 1024

FFN_TM = 1024
FFN_TF = 256
PROJ_TM = 512
RESID_TM = 1024
Q_TILE = 128
K_TILE = 256
ATTN_UNROLL = 4
HEAD_ORDER = (0, 4, 1, 5, 2, 6, 3, 7)
GATE_LANE0 = 0
DT_LANE0 = 3 * NSA_HEADS
N_BIAS_TABLES = 6
FAR_TABLE = 3
LOG2E = math.log2(math.e)
BF16_SUBLANES = 16
CONV_TAIL = 8


def _dot(a, b):
    return jnp.dot(a, b, preferred_element_type=F32)


def _dot_nt(a, b):
    return lax.dot_general(a, b, (((1,), (1,)), ((), ())), preferred_element_type=F32)


def _split3(v):
    hi = v.astype(BF16)
    r = v - hi.astype(F32)
    mid = r.astype(BF16)
    lo = (r - mid.astype(F32)).astype(BF16)
    return hi, mid, lo


def _dot_f32x2_lhs(v, e):
    hi = v.astype(BF16)
    lo = (v - hi.astype(F32)).astype(BF16)
    return _dot(hi, e) + _dot(lo, e)


def _dot_f32_rhs(e, v):
    hi, mid, lo = _split3(v)
    return _dot(e, hi) + _dot(e, mid) + _dot(e, lo)


def _rms(x, g):
    ms = jnp.mean(x * x, axis=-1, keepdims=True)
    return x * lax.rsqrt(ms + EPS) * g


def _silu(x):
    return x * jax.nn.sigmoid(x)


def _cparams(*sem):
    return pltpu.CompilerParams(dimension_semantics=sem, vmem_limit_bytes=VMEM_LIMIT_BYTES)


def _ffn_kernel(x_ref, g_ref, wg_ref, wu_ref, wo_ref, o_ref, xn_ref, acc_ref):
    j = pl.program_id(1)

    @pl.when(j == 0)
    def _():
        xn_ref[...] = _rms(x_ref[...], g_ref[...]).astype(BF16)
        acc_ref[...] = jnp.zeros_like(acc_ref)

    xn = xn_ref[...]
    gate = _dot(xn, wg_ref[...])
    up = _dot(xn, wu_ref[...])
    h = (_silu(gate) * up).astype(BF16)
    acc_ref[...] += _dot(h, wo_ref[...])

    @pl.when(j == pl.num_programs(1) - 1)
    def _():
        o_ref[...] = x_ref[...] + 0.5 * acc_ref[...]


def _ffn(x, gain, w_in, w_out, layer):
    t, d = x.shape
    nf = D_FF // FFN_TF
    return pl.pallas_call(
        _ffn_kernel,
        grid=(t // FFN_TM, nf),
        in_specs=[
            pl.BlockSpec((FFN_TM, d), lambda i, j: (i, 0)),
            pl.BlockSpec((None, 1, d), lambda i, j: (layer, 0, 0)),
            pl.BlockSpec((None, d, FFN_TF), lambda i, j: (layer, 0, j)),
            pl.BlockSpec((None, d, FFN_TF), lambda i, j: (layer, 0, j + nf)),
            pl.BlockSpec((None, FFN_TF, d), lambda i, j: (layer, j, 0)),
        ],
        out_specs=pl.BlockSpec((FFN_TM, d), lambda i, j: (i, 0)),
        out_shape=jax.ShapeDtypeStruct((t, d), F32),
        scratch_shapes=[pltpu.VMEM((FFN_TM, d), BF16), pltpu.VMEM((FFN_TM, d), F32)],
        compiler_params=_cparams("parallel", "arbitrary"),
        name="ffn",
    )(x, gain, w_in, w_in, w_out)


_C_Q = (0, NSA_WIDTH)
_C_KC = (_C_Q[1], _C_Q[1] + KV_WIDTH)
_C_VC = (_C_KC[1], _C_KC[1] + KV_WIDTH)
_C_KV = (_C_VC[1], _C_VC[1] + 4 * KV_WIDTH)
_C_MISC = (_C_KV[1], _C_KV[1] + LANES)
_C_Z = (_C_MISC[1], _C_MISC[1] + SSM_INNER)
_C_XBC = (_C_Z[1], _C_Z[1] + CONV_CH)
PROJ_COLS = _C_XBC[1]


def _inproj_kernel(x_ref, g_ref, w_ref, q_ref, kc_ref, vc_ref, kv_ref, misc_ref, z_ref, xbc_ref, rows_ref):
    xn = _rms(x_ref[...], g_ref[...]).astype(BF16)
    q_ref[...] = _dot(xn, w_ref[:, _C_Q[0]:_C_Q[1]])
    nrow = rows_ref.shape[0] // CMP_STRIDE
    for cols, dst in ((_C_KC, kc_ref), (_C_VC, vc_ref)):
        rows_ref[...] = _dot(xn, w_ref[:, cols[0]:cols[1]])
        for l in range(CMP_STRIDE):
            dst[:, l * KV_WIDTH:(l + 1) * KV_WIDTH] = rows_ref[pl.ds(l, nrow, stride=CMP_STRIDE), :]
    kv_ref[...] = _dot(xn, w_ref[:, _C_KV[0]:_C_KV[1]]).astype(BF16)
    misc_ref[...] = _dot(xn, w_ref[:, _C_MISC[0]:_C_MISC[1]])
    z_ref[...] = _dot(xn, w_ref[:, _C_Z[0]:_C_Z[1]])
    xbc_ref[...] = _dot(xn, w_ref[:, _C_XBC[0]:_C_XBC[1]])


def _inproj(x, gain, w, layer):
    t, d = x.shape
    tm = PROJ_TM
    grouped = (tm // CMP_STRIDE, t // CMP_STRIDE, CMP_STRIDE * KV_WIDTH, F32)
    outs = ((tm, t, NSA_WIDTH, F32), grouped, grouped, (tm, t, 4 * KV_WIDTH, BF16), (tm, t, LANES, F32),
            (tm, t, SSM_INNER, F32), (tm, t, CONV_CH, F32))
    return pl.pallas_call(
        _inproj_kernel,
        grid=(t // tm,),
        in_specs=[
            pl.BlockSpec((tm, d), lambda i: (i, 0)),
            pl.BlockSpec((None, 1, d), lambda i: (layer, 0, 0)),
            pl.BlockSpec((None, d, PROJ_COLS), lambda i: (layer, 0, 0)),
        ],
        out_specs=[pl.BlockSpec((rows, n), lambda i: (i, 0)) for rows, _, n, _ in outs],
        out_shape=[jax.ShapeDtypeStruct((total, n), dt) for _, total, n, dt in outs],
        scratch_shapes=[pltpu.VMEM((tm, KV_WIDTH), F32)],
        compiler_params=_cparams("parallel"),
        name="inproj",
    )(x, gain, w)


_IN_OFFS = tuple(int(v) for v in np.cumsum(
    (0, NSA_WIDTH) + (KV_WIDTH,) * 6 + (3 * NSA_HEADS, SSM_INNER, CONV_CH, SSM_HEADS)))
PREP_ROWS = 256


def _prep_inproj_kernel(w_ref, o_ref):
    o = _IN_OFFS
    o_ref[:, _C_KC[0]:_C_KV[1]] = w_ref[:, o[1]:o[7]].astype(BF16)
    o_ref[:, _C_Z[0]:_C_XBC[1]] = w_ref[:, o[8]:o[10]].astype(BF16)
    for pos, h in enumerate(HEAD_ORDER):
        o_ref[:, pos * HEAD_DIM:(pos + 1) * HEAD_DIM] = w_ref[:, h * HEAD_DIM:(h + 1) * HEAD_DIM].astype(BF16)
    ngate, ndt = o[8] - o[7], o[11] - o[10]
    m0 = _C_MISC[0]
    o_ref[:, m0:m0 + ngate] = w_ref[:, o[7]:o[8]].astype(BF16)
    o_ref[:, m0 + ngate:m0 + ngate + ndt] = w_ref[:, o[10]:o[11]].astype(BF16)
    o_ref[:, m0 + ngate + ndt:_C_MISC[1]] = jnp.zeros((o_ref.shape[0], LANES - ngate - ndt), BF16)


def _prep_inproj_weight(w_mix_in):
    nl, d, win = w_mix_in.shape
    return pl.pallas_call(
        _prep_inproj_kernel,
        grid=(nl, d // PREP_ROWS),
        in_specs=[pl.BlockSpec((None, PREP_ROWS, win), lambda l, r: (l, r, 0))],
        out_specs=pl.BlockSpec((None, PREP_ROWS, PROJ_COLS), lambda l, r: (l, r, 0)),
        out_shape=jax.ShapeDtypeStruct((nl, d, PROJ_COLS), BF16),
        compiler_params=_cparams("parallel", "parallel"),
        name="inproj_weight_layout",
    )(w_mix_in)


def _compress_kernel(kc_ref, vc_ref, pos_ref, w1a_ref, w1b_ref, b1_ref, w2_ref, b2_ref,
                     kcmp_ref, vcmpt_ref):
    nrow = kc_ref.shape[0]
    for which, src in enumerate((kc_ref, vc_ref)):
        r = src[...]
        ra = (r + pos_ref[which, 0]).astype(BF16)
        rb = (r + pos_ref[which, 1]).astype(BF16)
        ha = _dot(ra, w1a_ref[which])
        hb = _dot(rb, w1b_ref[which])
        h = ha + pltpu.roll(hb, nrow - 1, 0) + b1_ref[which]
        out = _dot(_silu(h).astype(BF16), w2_ref[which]) + b2_ref[which]
        if which == 0:
            kcmp_ref[...] = out.astype(BF16)
        else:
            vcmpt_ref[...] = out.T.astype(BF16)


def _compress(kc, vc, prep, layer, batch, seq):
    nrow = seq // CMP_STRIDE
    wide = CMP_STRIDE * KV_WIDTH
    kcr = kc.reshape(batch, nrow, wide)
    vcr = vc.reshape(batch, nrow, wide)
    lay = lambda a: pl.BlockSpec((None,) + a.shape[1:], lambda b: (layer,) + (0,) * (a.ndim - 1))
    consts = (prep["pos"], prep["w1a"], prep["w1b"], prep["b1"], prep["w2"], prep["b2"])
    return pl.pallas_call(
        _compress_kernel,
        grid=(batch,),
        in_specs=[pl.BlockSpec((None, nrow, wide), lambda b: (b, 0, 0))] * 2 + [lay(a) for a in consts],
        out_specs=[pl.BlockSpec((None, nrow, KV_WIDTH), lambda b: (b, 0, 0)),
                   pl.BlockSpec((None, KV_WIDTH, nrow), lambda b: (b, 0, 0))],
        out_shape=[jax.ShapeDtypeStruct((batch, nrow, KV_WIDTH), BF16),
                   jax.ShapeDtypeStruct((batch, KV_WIDTH, nrow), BF16)],
        compiler_params=_cparams("parallel"),
        name="nsa_compress",
    )(kcr, vcr, *consts)


def _prep_compress(cmp_pos, cmp_w1, cmp_b1, cmp_w2, cmp_b2):
    assert NSA_KV_GROUPS == 2
    nl = cmp_w1.shape[0]
    half = CMP_BLOCK // 2
    w1 = cmp_w1.astype(BF16).reshape(nl, 2, CMP_BLOCK, HEAD_DIM, CMP_HIDDEN)

    def block_diag(w, axis):
        z = jnp.zeros_like(w)
        return jnp.stack([jnp.concatenate([w, z], axis=-1), jnp.concatenate([z, w], axis=-1)], axis=axis)

    def expand(w):
        return block_diag(w, 3).reshape(nl, 2, half * 2 * HEAD_DIM, 2 * CMP_HIDDEN)

    def pos_rows(p):
        return jnp.broadcast_to(p[:, :, :, None, :], (nl, 2, half, 2, HEAD_DIM)).reshape(nl, 2, 1, half * 2 * HEAD_DIM)

    w2 = block_diag(cmp_w2.astype(BF16), 2).reshape(nl, 2, 2 * CMP_HIDDEN, 2 * HEAD_DIM)
    return {
        "pos": jnp.stack([pos_rows(cmp_pos[:, :, :half]), pos_rows(cmp_pos[:, :, half:])], axis=2),
        "w1a": expand(w1[:, :, :half]),
        "w1b": expand(w1[:, :, half:]),
        "b1": jnp.tile(cmp_b1, (1, 1, 2))[:, :, None, :],
        "w2": w2,
        "b2": jnp.tile(cmp_b2, (1, 1, 2))[:, :, None, :],
    }


def _t5_bucket_np(dist):
    n = np.maximum(dist, 0)
    exact = REL_BUCKETS // 2
    nf = np.maximum(n, exact).astype(np.float64)
    large = exact + (np.log(nf / exact) / math.log(REL_MAX_DIST / exact) * (REL_BUCKETS - exact)).astype(np.int64)
    return np.where(n < exact, n, np.minimum(large, REL_BUCKETS - 1)).astype(np.int32)


MASKED_BUCKET = REL_BUCKETS


def _bucket_maps(seq):
    j = np.arange(K_TILE)[:, None]
    i = np.arange(Q_TILE)[None, :]
    tiles = []
    for delta in range(N_BIAS_TABLES):
        d = Q_TILE * delta + i - j
        valid = (d >= 0) & ((d < WINDOW) if delta >= 4 else True)
        tiles.append(np.where(valid, _t5_bucket_np(d), MASKED_BUCKET))
    c = np.arange(seq // CMP_STRIDE)[:, None]
    t = np.arange(seq)[None, :]
    dc = t - (c * CMP_STRIDE + CMP_BLOCK - 1)
    cmp_map = np.where(dc >= 0, _t5_bucket_np(dc), MASKED_BUCKET)
    return np.stack(tiles).astype(np.int32), cmp_map.astype(np.int32)


def _tables_kernel(tab_ref, bkt_ref, bkc_ref, tb_ref, bc_ref):
    h = pl.program_id(0)
    for src, dst in ((bkt_ref, tb_ref), (bkc_ref, bc_ref)):
        bk = src[...]
        out = jnp.zeros(bk.shape, F32)
        for b in range(REL_BUCKETS + 1):
            out = jnp.where(bk == b, tab_ref[b, h], out)
        dst[...] = (out * LOG2E).astype(dst.dtype)


def _nsa_tables(rel_table, seq):
    bkt, bkc = _bucket_maps(seq)
    tab = jnp.concatenate([rel_table[:, np.asarray(HEAD_ORDER)],
                           jnp.full((1, NSA_HEADS), NEG_INF, F32)], axis=0)
    nc = seq // CMP_STRIDE
    return pl.pallas_call(
        _tables_kernel,
        grid=(NSA_HEADS,),
        in_specs=[
            pl.BlockSpec(memory_space=pltpu.SMEM),
            pl.BlockSpec(bkt.shape, lambda h: (0, 0, 0)),
            pl.BlockSpec(bkc.shape, lambda h: (0, 0)),
        ],
        out_specs=[pl.BlockSpec((N_BIAS_TABLES, None, K_TILE, Q_TILE), lambda h: (0, h, 0, 0)),
                   pl.BlockSpec((None, nc, seq), lambda h: (h, 0, 0))],
        out_shape=[jax.ShapeDtypeStruct((N_BIAS_TABLES, NSA_HEADS, K_TILE, Q_TILE), BF16),
                   jax.ShapeDtypeStruct((NSA_HEADS, nc, seq), F32)],
        compiler_params=_cparams("parallel"),
        name="nsa_bias_tables",
    )(tab, jnp.asarray(bkt), jnp.asarray(bkc))


def _nsa_consts(seq):
    nc = seq // CMP_STRIDE
    nb = seq // SEL_BLOCK
    c = np.arange(nc)
    blk = np.arange(nb)
    c_lo, c_hi = c * CMP_STRIDE, c * CMP_STRIDE + CMP_BLOCK - 1
    s_lo, s_hi = blk * SEL_BLOCK, blk * SEL_BLOCK + SEL_BLOCK - 1
    ovt = (c_lo[None, :] <= s_hi[:, None]) & (c_hi[None, :] >= s_lo[:, None])
    ovt[:, nc - 1] = False
    return jnp.asarray(ovt, BF16)


def _nsa_kernel(q_ref, kcmp_ref, vcmpt_ref, ks_ref, vs_ref, kw_ref, vw_ref, misc_ref, biasc_ref, tb_ref,
                ovt_ref, o_ref, qpt_ref, vst_ref, vwt_ref, madd_ref, m_ref, acc_ref, mw_ref, accw_ref, ocmp_ref):
    n = pl.program_id(1)
    nkt = ks_ref.shape[0] // K_TILE
    nsel_blocks = ovt_ref.shape[0]
    nhb = NSA_HEADS
    cols = lambda hb: slice(hb * Q_TILE, (hb + 1) * Q_TILE)
    frow = lax.broadcasted_iota(jnp.int32, (KV_WIDTH, Q_TILE), 0)
    low = frow < HEAD_DIM

    @pl.when(n == 0)
    def _():
        arow = lax.broadcasted_iota(jnp.int32, (BF16_SUBLANES, K_TILE), 0)
        ones_row = jnp.where(arow == 0, 1.0, 0.0).astype(BF16)
        for t2 in range(nkt):
            rows = slice(t2 * K_TILE, (t2 + 1) * K_TILE)
            for src, dst in ((vs_ref, vst_ref), (vw_ref, vwt_ref)):
                dst[t2, 0:KV_WIDTH, :] = src[rows, :].astype(F32).T.astype(BF16)
                dst[t2, KV_WIDTH:, :] = ones_row

    scale = HEAD_DIM ** -0.5 * LOG2E
    for j in range(nhb // 2):
        slab = (q_ref[:, j * LANES:(j + 1) * LANES].T * scale).astype(BF16)
        zero = jnp.zeros_like(slab)
        qpt_ref[:, cols(2 * j)] = jnp.where(low, slab, zero)
        qpt_ref[:, cols(2 * j + 1)] = jnp.where(low, zero, slab)
    qpt = qpt_ref[...]

    pair_cols = lambda pair: slice(2 * pair * Q_TILE, (2 * pair + 2) * Q_TILE)

    def branch(k_ref, vt_ref, mx_ref, ac_ref, selected):
        def scores(t2):
            start = pl.multiple_of(t2 * K_TILE, K_TILE)
            k = k_ref[pl.ds(start, K_TILE), :]
            return [_dot(k, qpt_ref[:, pair_cols(pair)]) for pair in range(nhb // 2)]

        def softmax_pv(t2, s_pairs):
            vt = vt_ref[t2]
            delta = n - 2 * t2
            ti = jnp.minimum(delta, FAR_TABLE) if selected else delta
            blk0 = t2 * (K_TILE // SEL_BLOCK)
            for pair in range(nhb // 2):
                c2 = pair_cols(pair)
                s2 = s_pairs[pair]
                ps, alphas = [], []
                for half in range(2):
                    hb = 2 * pair + half
                    s = s2[:, half * Q_TILE:(half + 1) * Q_TILE].astype(BF16) + tb_ref[ti, hb]
                    if selected:
                        s = jnp.concatenate(
                            [s[j * SEL_BLOCK:(j + 1) * SEL_BLOCK]
                             + madd_ref[half, pl.ds(blk0 + j, 1), :].astype(BF16)
                             for j in range(K_TILE // SEL_BLOCK)], axis=0)
                    m_prev = mx_ref[:, cols(hb)]
                    m_new = jnp.maximum(m_prev, jnp.max(s, axis=0, keepdims=True).astype(F32))
                    mx_ref[:, cols(hb)] = m_new
                    alphas.append(jnp.exp2(m_prev - m_new))
                    ps.append(jnp.exp2(s - m_new.astype(BF16)))
                pv = _dot(vt, jnp.concatenate(ps, axis=1))
                ac_ref[:, c2] = ac_ref[:, c2] * jnp.concatenate(alphas, axis=1) + pv

        def issue(first, k):
            return [scores(first + i) for i in range(k)]

        def finish(first, ss):
            for i, s_pairs in enumerate(ss):
                softmax_pv(first + i, s_pairs)

        def init():
            mx_ref[...] = jnp.full_like(mx_ref, -jnp.inf)
            ac_ref[...] = jnp.zeros_like(ac_ref)

        def result():
            acc = ac_ref[...]
            return acc[0:KV_WIDTH] / acc[KV_WIDTH:KV_WIDTH + 1]

        return init, issue, finish, result

    sel_init, sel_issue, sel_finish, sel_result = branch(ks_ref, vst_ref, m_ref, acc_ref, True)
    win_init, win_issue, win_finish, win_result = branch(kw_ref, vwt_ref, mw_ref, accw_ref, False)

    tq = n * Q_TILE + lax.broadcasted_iota(jnp.int32, (1, Q_TILE), 1)

    def compressed_and_select(win_first, win_k):
        has_block = (tq >= CMP_BLOCK - 1).astype(F32)
        sc = _dot(kcmp_ref[...], qpt)
        win_scores = win_issue(win_first, win_k)
        psum = [None, None]
        pcs = []
        for hb in range(nhb):
            s = sc[:, cols(hb)] + biasc_ref[hb]
            m = jnp.max(s, axis=0, keepdims=True)
            e = jnp.exp2(s - m)
            p = e / jnp.sum(e, axis=0, keepdims=True) * has_block
            pcs.append(p.astype(BF16))
            g = hb % 2
            psum[g] = p if psum[g] is None else psum[g] + p
        ocmp_ref[...] = _dot(vcmpt_ref[...], jnp.concatenate(pcs, axis=1))

        blk = lax.broadcasted_iota(jnp.int32, (nsel_blocks, Q_TILE), 0)
        blk_f = blk.astype(F32)
        cur = tq // SEL_BLOCK
        forced = ((blk == 0) | (blk == cur) | (blk == cur - 1)).astype(F32)
        for g in range(NSA_KV_GROUPS):
            imp = _dot_f32_rhs(ovt_ref[...], psum[g])
            score = jnp.where(blk <= cur, imp + FORCE_SCORE * forced, -FORCE_SCORE)
            sel = jnp.zeros(score.shape, F32)
            for _ in range(SEL_TOPK):
                mx = jnp.max(score, axis=0, keepdims=True)
                first = jnp.min(jnp.where(score == mx, blk_f, float(nsel_blocks)), axis=0, keepdims=True)
                hit = blk_f == first
                sel = jnp.where(hit, 1.0, sel)
                score = jnp.where(hit, -jnp.inf, score)
            madd_ref[g] = (sel - 1.0) * (-NEG_INF)
        win_finish(win_first, win_scores)

    t2_diag = n // 2
    win_full = WINDOW // K_TILE + 1
    win_init()

    @pl.when(t2_diag >= win_full - 1)
    def _():
        compressed_and_select(t2_diag - (win_full - 1), win_full)

    for k in range(1, win_full):
        @pl.when(t2_diag + 1 == k)
        def _(k=k):
            compressed_and_select(0, k)

    sel_init()
    count = t2_diag + 1

    def full_blocks(i, carry):
        sel_finish(ATTN_UNROLL * i, sel_issue(ATTN_UNROLL * i, ATTN_UNROLL))
        return carry

    lax.fori_loop(0, count // ATTN_UNROLL, full_blocks, 0)
    rest_first = (count // ATTN_UNROLL) * ATTN_UNROLL
    for k in range(1, ATTN_UNROLL):
        @pl.when(count % ATTN_UNROLL == k)
        def _(k=k):
            sel_finish(rest_first, sel_issue(rest_first, k))

    o_cmp = ocmp_ref[...]
    o_sel = sel_result()
    o_win = win_result()

    gates = jax.nn.sigmoid(misc_ref[...]).T

    def gate(branch, hb):
        col = GATE_LANE0 + branch * NSA_HEADS + HEAD_ORDER[hb]
        return gates[col:col + 1, :]

    for j in range(nhb // 2):
        outs = []
        for hb in (2 * j, 2 * j + 1):
            outs.append(gate(0, hb) * o_cmp[:, cols(hb)] + gate(1, hb) * o_sel[:, cols(hb)]
                        + gate(2, hb) * o_win[:, cols(hb)])
        o_ref[:, j * LANES:(j + 1) * LANES] = jnp.where(low, outs[0], outs[1]).T


def _nsa_attention(q, kcmp, vcmpt, kv, misc, tb, biasc, ovt, batch, seq):
    nq = seq // Q_TILE
    nc = seq // CMP_STRIDE
    nkt = seq // K_TILE
    cols_all = NSA_HEADS * Q_TILE
    vrows = KV_WIDTH + BF16_SUBLANES
    kvspec = lambda c: pl.BlockSpec((seq, KV_WIDTH), lambda b, n: (b, c))
    return pl.pallas_call(
        _nsa_kernel,
        grid=(batch, nq),
        in_specs=[
            pl.BlockSpec((Q_TILE, NSA_WIDTH), lambda b, n: (b * nq + n, 0)),
            pl.BlockSpec((None, nc, KV_WIDTH), lambda b, n: (b, 0, 0)),
            pl.BlockSpec((None, KV_WIDTH, nc), lambda b, n: (b, 0, 0)),
            kvspec(0), kvspec(1), kvspec(2), kvspec(3),
            pl.BlockSpec((Q_TILE, LANES), lambda b, n: (b * nq + n, 0)),
            pl.BlockSpec((NSA_HEADS, nc, Q_TILE), lambda b, n: (0, 0, n)),
            pl.BlockSpec(tb.shape, lambda b, n: (0, 0, 0, 0)),
            pl.BlockSpec(ovt.shape, lambda b, n: (0, 0)),
        ],
        out_specs=pl.BlockSpec((Q_TILE, NSA_WIDTH), lambda b, n: (b * nq + n, 0)),
        out_shape=jax.ShapeDtypeStruct((batch * seq, NSA_WIDTH), F32),
        scratch_shapes=[
            pltpu.VMEM((KV_WIDTH, cols_all), BF16),
            pltpu.VMEM((nkt, vrows, K_TILE), BF16),
            pltpu.VMEM((nkt, vrows, K_TILE), BF16),
            pltpu.VMEM((NSA_KV_GROUPS, seq // SEL_BLOCK, Q_TILE), F32),
            pltpu.VMEM((1, cols_all), F32),
            pltpu.VMEM((vrows, cols_all), F32),
            pltpu.VMEM((1, cols_all), F32),
            pltpu.VMEM((vrows, cols_all), F32),
            pltpu.VMEM((KV_WIDTH, cols_all), F32),
        ],
        compiler_params=_cparams("arbitrary", "arbitrary"),
        name="nsa_attention",
    )(q, kcmp, vcmpt, kv, kv, kv, kv, misc, biasc, tb, ovt)


def _softplus(x):
    return jnp.maximum(x, 0.0) + jnp.log1p(jnp.exp(-jnp.abs(x)))


def _ssd_kernel(xbc_ref, z_ref, misc_ref, convw_ref, convb_ref, dtb_ref, alog_ref, dskip_ref, gain_ref,
                tri_ref, e1_ref, o_ref, prev_ref, h_ref):
    c = pl.program_id(1)
    L = SSD_CHUNK
    gw = SSM_INNER // SSM_GROUPS
    hpg = SSM_HEADS // SSM_GROUPS

    @pl.when(c == 0)
    def _():
        prev_ref[...] = jnp.zeros_like(prev_ref)
        h_ref[...] = jnp.zeros_like(h_ref)

    x = xbc_ref[...]
    prev_ref[CONV_TAIL:, :] = x
    acc = convb_ref[...] + x * convw_ref[CONV_WIDTH - 1:CONV_WIDTH, :]
    for k in range(1, CONV_WIDTH):
        xk = prev_ref[CONV_TAIL - k:CONV_TAIL - k + L, :]
        acc = acc + xk * convw_ref[CONV_WIDTH - 1 - k:CONV_WIDTH - k, :]
    prev_ref[0:CONV_TAIL, :] = x[L - CONV_TAIL:L]
    xa = _silu(acc)
    xs = xa[:, :SSM_INNER]
    bm = xa[:, SSM_INNER:SSM_INNER + SSM_GROUPS * SSM_STATE]
    cm = xa[:, SSM_INNER + SSM_GROUPS * SSM_STATE:]

    dt = _softplus(misc_ref[...] + dtb_ref[...])
    da = dt * (-jnp.exp(alog_ref[...]))
    cs = _dot_f32_rhs(tri_ref[...], da) * LOG2E
    cs_t = cs.T
    dt_t = dt.T
    ecs = _dot_f32x2_lhs(jnp.exp2(cs), e1_ref[...])
    to_end = dt * jnp.exp2(cs[L - 1:L, :] - cs)
    xw_b = (xs * _dot_f32x2_lhs(to_end, e1_ref[...])).astype(BF16)
    xs_b = xs.astype(BF16)
    state_decay = ecs[L - 1:L, :]

    li = lax.broadcasted_iota(jnp.int32, (L, L), 0)
    si = lax.broadcasted_iota(jnp.int32, (L, L), 1)
    causal = li >= si
    low = si < SSM_HEAD_DIM

    ys = []
    for g in range(SSM_GROUPS):
        bg = bm[:, g * SSM_STATE:(g + 1) * SSM_STATE]
        cg = cm[:, g * SSM_STATE:(g + 1) * SSM_STATE].astype(BF16)
        cb = _dot_nt(cg, bg.astype(BF16))
        h_g = h_ref[:, g * gw:(g + 1) * gw]
        y_off = _dot(cg, h_g.astype(BF16)) * ecs[:, g * gw:(g + 1) * gw]
        for pr in range(hpg // 2):
            h0 = g * hpg + 2 * pr
            gs = []
            for hh in (h0, h0 + 1):
                ln = DT_LANE0 + hh
                col = jnp.broadcast_to(cs[:, ln:ln + 1], (L, L))
                dec = jnp.exp2(jnp.where(causal, col - cs_t[ln:ln + 1, :], NEG_INF))
                gs.append((cb * dec * dt_t[ln:ln + 1, :]).astype(BF16))
            ch = slice(h0 * SSM_HEAD_DIM, (h0 + 2) * SSM_HEAD_DIM)
            xpair = xs_b[:, ch]
            zero = jnp.zeros_like(xpair)
            rhs = jnp.concatenate([jnp.where(low, xpair, zero), jnp.where(low, zero, xpair)], axis=0)
            y_diag = _dot(jnp.concatenate(gs, axis=1), rhs)
            off = slice(2 * pr * SSM_HEAD_DIM, (2 * pr + 2) * SSM_HEAD_DIM)
            ys.append(y_diag + y_off[:, off] + xs[:, ch] * dskip_ref[:, ch])
        st = _dot(bg.T.astype(BF16), xw_b[:, g * gw:(g + 1) * gw])
        h_ref[:, g * gw:(g + 1) * gw] = h_g * state_decay[:, g * gw:(g + 1) * gw] + st

    y = jnp.concatenate(ys, axis=1) * _silu(z_ref[...])
    outs = []
    for g in range(SSM_GROUPS):
        outs.append(_rms(y[:, g * gw:(g + 1) * gw], gain_ref[:, g * gw:(g + 1) * gw]))
    o_ref[...] = jnp.concatenate(outs, axis=1).astype(BF16)


def _ssd_consts():
    lane = np.arange(LANES)
    tri = (np.arange(SSD_CHUNK)[:, None] >= np.arange(SSD_CHUNK)[None, :])
    head1 = np.arange(SSM_INNER) // SSM_HEAD_DIM
    e1 = (lane[:, None] - DT_LANE0) == head1[None, :]
    return jnp.asarray(tri, BF16), jnp.asarray(e1, BF16)


def _ssd(xbc, z, misc, prm, consts, layer, batch, seq):
    nch = seq // SSD_CHUNK
    tok = lambda n: pl.BlockSpec((SSD_CHUNK, n), lambda b, c: (b * nch + c, 0))
    lay = lambda a: pl.BlockSpec((None,) + a.shape[1:], lambda b, c: (layer,) + (0,) * (a.ndim - 1))
    full = lambda a: pl.BlockSpec(a.shape, lambda b, c: (0,) * a.ndim)
    params = (prm["conv_w"], prm["conv_b"], prm["dt_bias"], prm["a_log"], prm["d_skip"], prm["ssm_gain"])
    return pl.pallas_call(
        _ssd_kernel,
        grid=(batch, nch),
        in_specs=[tok(CONV_CH), tok(SSM_INNER), tok(LANES)] + [lay(a) for a in params] + [full(a) for a in consts],
        out_specs=tok(SSM_INNER),
        out_shape=jax.ShapeDtypeStruct((batch * seq, SSM_INNER), BF16),
        scratch_shapes=[pltpu.VMEM((CONV_TAIL + SSD_CHUNK, CONV_CH), F32), pltpu.VMEM((SSM_STATE, SSM_INNER), F32)],
        compiler_params=_cparams("parallel", "arbitrary"),
        name="ssd",
    )(xbc, z, misc, *params, *consts)


def _prep_ssd_params(conv_w, conv_b, dt_bias, a_log, d_skip, ssm_out_norm):
    nl = conv_w.shape[0]

    def dt_lanes(v):
        out = jnp.zeros((nl, 1, LANES), F32)
        return out.at[:, 0, DT_LANE0:DT_LANE0 + SSM_HEADS].set(v)

    return {
        "conv_w": conv_w,
        "conv_b": conv_b[:, None, :],
        "dt_bias": dt_lanes(dt_bias),
        "a_log": dt_lanes(a_log),
        "d_skip": jnp.repeat(d_skip, SSM_HEAD_DIM, axis=-1)[:, None, :],
        "ssm_gain": ssm_out_norm[:, None, :],
    }


def _outproj_kernel(x_ref, oa_ref, os_ref, g_ref, wa_ref, ws_ref, o_ref):
    an = _rms(oa_ref[...], g_ref[...]).astype(BF16)
    o_ref[...] = x_ref[...] + _dot(an, wa_ref[...]) + _dot(os_ref[...], ws_ref[...])


def _outproj(x, o_attn, o_ssm, gain, wa, ws, layer):
    t, d = x.shape
    tm = RESID_TM
    return pl.pallas_call(
        _outproj_kernel,
        grid=(t // tm,),
        in_specs=[
            pl.BlockSpec((tm, d), lambda i: (i, 0)),
            pl.BlockSpec((tm, NSA_WIDTH), lambda i: (i, 0)),
            pl.BlockSpec((tm, SSM_INNER), lambda i: (i, 0)),
            pl.BlockSpec((None, 1, NSA_WIDTH), lambda i: (layer, 0, 0)),
            pl.BlockSpec((None, NSA_WIDTH, d), lambda i: (layer, 0, 0)),
            pl.BlockSpec((None, SSM_INNER, d), lambda i: (layer, 0, 0)),
        ],
        out_specs=pl.BlockSpec((tm, d), lambda i: (i, 0)),
        out_shape=jax.ShapeDtypeStruct((t, d), F32),
        compiler_params=_cparams("parallel"),
        name="outproj",
    )(x, o_attn, o_ssm, gain, wa, ws)


def _perm_heads(a, axis):
    idx = np.concatenate([np.arange(h * HEAD_DIM, (h + 1) * HEAD_DIM) for h in HEAD_ORDER])
    return jnp.take(a, jnp.asarray(idx), axis=axis)


def _ple_kernel(x_ref, p_ref, g_ref, wg_ref, wp_ref, fg_ref, o_ref, *, final):
    x = x_ref[...]
    xn = _rms(x, g_ref[...]).astype(BF16)
    gate = jax.nn.sigmoid(_dot(xn, wg_ref[...]))
    y = x + gate * _dot(p_ref[...].astype(BF16), wp_ref[...])
    if final:
        y = _rms(y, fg_ref[...])
    o_ref[...] = y


def _ple(x, p, gain, wg, wp, final_gain, layer, final):
    t, d = x.shape
    tm = RESID_TM
    return pl.pallas_call(
        functools.partial(_ple_kernel, final=final),
        grid=(t // tm,),
        in_specs=[
            pl.BlockSpec((tm, d), lambda i: (i, 0)),
            pl.BlockSpec((None, tm, PLE_DIM), lambda i: (layer, i, 0)),
            pl.BlockSpec((None, 1, d), lambda i: (layer, 0, 0)),
            pl.BlockSpec((None, d, d), lambda i: (layer, 0, 0)),
            pl.BlockSpec((None, PLE_DIM, d), lambda i: (layer, 0, 0)),
            pl.BlockSpec((1, d), lambda i: (0, 0)),
        ],
        out_specs=pl.BlockSpec((tm, d), lambda i: (i, 0)),
        out_shape=jax.ShapeDtypeStruct((t, d), F32),
        compiler_params=_cparams("parallel"),
        name="ple",
    )(x, p, gain, wg, wp, final_gain)


def kernel(x, p, ffn1_norm, ffn1_w_in, ffn1_w_out, mix_norm, w_mix_in, cmp_pos, cmp_w1, cmp_b1, cmp_w2,
           cmp_b2, rel_table, nsa_out_norm, conv_w, conv_b, dt_bias, a_log, d_skip, ssm_out_norm, w_mix_out,
           ffn2_norm, ffn2_w_in, ffn2_w_out, ple_norm, ple_gate_w, ple_proj_w, final_norm):
    batch, seq, d = x.shape
    depth = p.shape[0]
    t = batch * seq
    bf = lambda a: a.astype(BF16)
    row = lambda a: a[:, None, :]

    ffn1_in, ffn1_out, ffn2_in, ffn2_out = bf(ffn1_w_in), bf(ffn1_w_out), bf(ffn2_w_in), bf(ffn2_w_out)
    w_proj = _prep_inproj_weight(w_mix_in)
    wo_attn = bf(_perm_heads(w_mix_out[:, :NSA_WIDTH], axis=1))
    wo_ssm = bf(w_mix_out[:, NSA_WIDTH:])
    nsa_gain = row(_perm_heads(nsa_out_norm, axis=1))
    ssd_prm = _prep_ssd_params(conv_w, conv_b, dt_bias, a_log, d_skip, ssm_out_norm)
    ssd_consts = _ssd_consts()
    tb, biasc = _nsa_tables(rel_table, seq)
    ovt = _nsa_consts(seq)
    cprep = _prep_compress(cmp_pos, cmp_w1, cmp_b1, cmp_w2, cmp_b2)
    wg, wp = bf(ple_gate_w), bf(ple_proj_w)
    p2 = p.reshape(depth, t, PLE_DIM)
    fgain = final_norm[None, :]

    h = x.reshape(t, d)
    for i in range(depth):
        h = _ffn(h, row(ffn1_norm), ffn1_in, ffn1_out, i)
        q, kc, vc, kv, misc, z, xbc = _inproj(h, row(mix_norm), w_proj, i)
        kcmp, vcmp = _compress(kc, vc, cprep, i, batch, seq)
        o_attn = _nsa_attention(q, kcmp, vcmp, kv, misc, tb, biasc, ovt, batch, seq)
        o_ssm = _ssd(xbc, z, misc, ssd_prm, ssd_consts, i, batch, seq)
        h = _outproj(h, o_attn, o_ssm, nsa_gain, wo_attn, wo_ssm, i)
        h = _ffn(h, row(ffn2_norm), ffn2_in, ffn2_out, i)
        h = _ple(h, p2, row(ple_norm), wg, wp, fgain, i, final=(i == depth - 1))
    return h.reshape(batch, seq, d)
```

```python
import functools
import math

import numpy as np
import jax
import jax.numpy as jnp
from jax import lax
from jax.experimental import pallas as pl
from jax.experimental.pallas import tpu as pltpu

F32 = jnp.float32
BF16 = jnp.bfloat16

D_MODEL = 1024
DEPTH = 4
PLE_DIM = 256
D_FF = 2816
EPS = 1e-6
NEG_INF = -1e30
FORCE_SCORE = 1e4

NSA_HEADS = 8
NSA_KV_GROUPS = 2
NSA_REP = NSA_HEADS // NSA_KV_GROUPS
HEAD_DIM = 64
NSA_WIDTH = NSA_HEADS * HEAD_DIM
KV_WIDTH = NSA_KV_GROUPS * HEAD_DIM
CMP_BLOCK = 32
CMP_STRIDE = 16
CMP_HIDDEN = 256
SEL_BLOCK = 64
SEL_TOPK = 8
WINDOW = 512
REL_BUCKETS = 32
REL_MAX_DIST = 128

SSM_HEADS = 16
SSM_HEAD_DIM = 64
SSM_INNER = SSM_HEADS * SSM_HEAD_DIM
SSM_GROUPS = 2
SSM_STATE = 128
CONV_WIDTH = 4
SSD_CHUNK = 128
CONV_CH = SSM_INNER + 2 * SSM_GROUPS * SSM_STATE

LANES = 128
VMEM_LIMIT_BYTES = 48 * 1024 * 1024

FFN_TM = 1024
FFN_TF = 256
PROJ_TM = 512
RESID_TM = 1024
Q_TILE = 128
K_TILE = 256
ATTN_UNROLL = 8
HEAD_ORDER = (0, 4, 1, 5, 2, 6, 3, 7)
GATE_LANE0 = 0
DT_LANE0 = 3 * NSA_HEADS
N_BIAS_TABLES = 6
FAR_TABLE = 3
LOG2E = math.log2(math.e)
BF16_SUBLANES = 16
CONV_TAIL = 8


def _dot(a, b):
    return jnp.dot(a, b, preferred_element_type=F32)


def _dot_nt(a, b):
    return lax.dot_general(a, b, (((1,), (1,)), ((), ())), preferred_element_type=F32)


def _split3(v):
    hi = v.astype(BF16)
    r = v - hi.astype(F32)
    mid = r.astype(BF16)
    lo = (r - mid.astype(F32)).astype(BF16)
    return hi, mid, lo


def _dot_f32x2_lhs(v, e):
    hi = v.astype(BF16)
    lo = (v - hi.astype(F32)).astype(BF16)
    return _dot(hi, e) + _dot(lo, e)


def _dot_f32_rhs(e, v):
    hi, mid, lo = _split3(v)
    return _dot(e, hi) + _dot(e, mid) + _dot(e, lo)


def _rms(x, g):
    ms = jnp.mean(x * x, axis=-1, keepdims=True)
    return x * lax.rsqrt(ms + EPS) * g


def _silu(x):
    return x * jax.nn.sigmoid(x)


def _cparams(*sem):
    return pltpu.CompilerParams(dimension_semantics=sem, vmem_limit_bytes=VMEM_LIMIT_BYTES)


def _ffn_kernel(x_ref, g_ref, wg_ref, wu_ref, wo_ref, o_ref, xn_ref, acc_ref):
    j = pl.program_id(1)

    @pl.when(j == 0)
    def _():
        xn_ref[...] = _rms(x_ref[...], g_ref[...]).astype(BF16)
        acc_ref[...] = jnp.zeros_like(acc_ref)

    xn = xn_ref[...]
    gate = _dot(xn, wg_ref[...])
    up = _dot(xn, wu_ref[...])
    h = (_silu(gate) * up).astype(BF16)
    acc_ref[...] += _dot(h, wo_ref[...])

    @pl.when(j == pl.num_programs(1) - 1)
    def _():
        o_ref[...] = x_ref[...] + 0.5 * acc_ref[...]


def _ffn(x, gain, w_in, w_out, layer):
    t, d = x.shape
    nf = D_FF // FFN_TF
    return pl.pallas_call(
        _ffn_kernel,
        grid=(t // FFN_TM, nf),
        in_specs=[
            pl.BlockSpec((FFN_TM, d), lambda i, j: (i, 0)),
            pl.BlockSpec((None, 1, d), lambda i, j: (layer, 0, 0)),
            pl.BlockSpec((None, d, FFN_TF), lambda i, j: (layer, 0, j)),
            pl.BlockSpec((None, d, FFN_TF), lambda i, j: (layer, 0, j + nf)),
            pl.BlockSpec((None, FFN_TF, d), lambda i, j: (layer, j, 0)),
        ],
        out_specs=pl.BlockSpec((FFN_TM, d), lambda i, j: (i, 0)),
        out_shape=jax.ShapeDtypeStruct((t, d), F32),
        scratch_shapes=[pltpu.VMEM((FFN_TM, d), BF16), pltpu.VMEM((FFN_TM, d), F32)],
        compiler_params=_cparams("parallel", "arbitrary"),
        name="ffn",
    )(x, gain, w_in, w_in, w_out)


_C_Q = (0, NSA_WIDTH)
_C_KC = (_C_Q[1], _C_Q[1] + KV_WIDTH)
_C_VC = (_C_KC[1], _C_KC[1] + KV_WIDTH)
_C_KV = (_C_VC[1], _C_VC[1] + 4 * KV_WIDTH)
_C_MISC = (_C_KV[1], _C_KV[1] + LANES)
_C_Z = (_C_MISC[1], _C_MISC[1] + SSM_INNER)
_C_XBC = (_C_Z[1], _C_Z[1] + CONV_CH)
PROJ_COLS = _C_XBC[1]


def _inproj_kernel(x_ref, g_ref, w_ref, q_ref, kc_ref, vc_ref, kv_ref, misc_ref, z_ref, xbc_ref, rows_ref):
    xn = _rms(x_ref[...], g_ref[...]).astype(BF16)
    q_ref[...] = _dot(xn, w_ref[:, _C_Q[0]:_C_Q[1]])
    nrow = rows_ref.shape[0] // CMP_STRIDE
    for cols, dst in ((_C_KC, kc_ref), (_C_VC, vc_ref)):
        rows_ref[...] = _dot(xn, w_ref[:, cols[0]:cols[1]])
        for l in range(CMP_STRIDE):
            dst[:, l * KV_WIDTH:(l + 1) * KV_WIDTH] = rows_ref[pl.ds(l, nrow, stride=CMP_STRIDE), :]
    kv_ref[...] = _dot(xn, w_ref[:, _C_KV[0]:_C_KV[1]]).astype(BF16)
    misc_ref[...] = _dot(xn, w_ref[:, _C_MISC[0]:_C_MISC[1]])
    z_ref[...] = _dot(xn, w_ref[:, _C_Z[0]:_C_Z[1]])
    xbc_ref[...] = _dot(xn, w_ref[:, _C_XBC[0]:_C_XBC[1]])


def _inproj(x, gain, w, layer):
    t, d = x.shape
    tm = PROJ_TM
    grouped = (tm // CMP_STRIDE, t // CMP_STRIDE, CMP_STRIDE * KV_WIDTH, F32)
    outs = ((tm, t, NSA_WIDTH, F32), grouped, grouped, (tm, t, 4 * KV_WIDTH, BF16), (tm, t, LANES, F32),
            (tm, t, SSM_INNER, F32), (tm, t, CONV_CH, F32))
    return pl.pallas_call(
        _inproj_kernel,
        grid=(t // tm,),
        in_specs=[
            pl.BlockSpec((tm, d), lambda i: (i, 0)),
            pl.BlockSpec((None, 1, d), lambda i: (layer, 0, 0)),
            pl.BlockSpec((None, d, PROJ_COLS), lambda i: (layer, 0, 0)),
        ],
        out_specs=[pl.BlockSpec((rows, n), lambda i: (i, 0)) for rows, _, n, _ in outs],
        out_shape=[jax.ShapeDtypeStruct((total, n), dt) for _, total, n, dt in outs],
        scratch_shapes=[pltpu.VMEM((tm, KV_WIDTH), F32)],
        compiler_params=_cparams("parallel"),
        name="inproj",
    )(x, gain, w)


_IN_OFFS = tuple(int(v) for v in np.cumsum(
    (0, NSA_WIDTH) + (KV_WIDTH,) * 6 + (3 * NSA_HEADS, SSM_INNER, CONV_CH, SSM_HEADS)))
PREP_ROWS = 256


def _prep_inproj_kernel(w_ref, o_ref):
    o = _IN_OFFS
    o_ref[:, _C_KC[0]:_C_KV[1]] = w_ref[:, o[1]:o[7]].astype(BF16)
    o_ref[:, _C_Z[0]:_C_XBC[1]] = w_ref[:, o[8]:o[10]].astype(BF16)
    for pos, h in enumerate(HEAD_ORDER):
        o_ref[:, pos * HEAD_DIM:(pos + 1) * HEAD_DIM] = w_ref[:, h * HEAD_DIM:(h + 1) * HEAD_DIM].astype(BF16)
    ngate, ndt = o[8] - o[7], o[11] - o[10]
    m0 = _C_MISC[0]
    o_ref[:, m0:m0 + ngate] = w_ref[:, o[7]:o[8]].astype(BF16)
    o_ref[:, m0 + ngate:m0 + ngate + ndt] = w_ref[:, o[10]:o[11]].astype(BF16)
    o_ref[:, m0 + ngate + ndt:_C_MISC[1]] = jnp.zeros((o_ref.shape[0], LANES - ngate - ndt), BF16)


def _prep_inproj_weight(w_mix_in):
    nl, d, win = w_mix_in.shape
    return pl.pallas_call(
        _prep_inproj_kernel,
        grid=(nl, d // PREP_ROWS),
        in_specs=[pl.BlockSpec((None, PREP_ROWS, win), lambda l, r: (l, r, 0))],
        out_specs=pl.BlockSpec((None, PREP_ROWS, PROJ_COLS), lambda l, r: (l, r, 0)),
        out_shape=jax.ShapeDtypeStruct((nl, d, PROJ_COLS), BF16),
        compiler_params=_cparams("parallel", "parallel"),
        name="inproj_weight_layout",
    )(w_mix_in)


def _compress_kernel(kc_ref, vc_ref, pos_ref, w1a_ref, w1b_ref, b1_ref, w2_ref, b2_ref,
                     kcmp_ref, vcmpt_ref):
    nrow = kc_ref.shape[0]
    for which, src in enumerate((kc_ref, vc_ref)):
        r = src[...]
        ra = (r + pos_ref[which, 0]).astype(BF16)
        rb = (r + pos_ref[which, 1]).astype(BF16)
        ha = _dot(ra, w1a_ref[which])
        hb = _dot(rb, w1b_ref[which])
        h = ha + pltpu.roll(hb, nrow - 1, 0) + b1_ref[which]
        out = _dot(_silu(h).astype(BF16), w2_ref[which]) + b2_ref[which]
        if which == 0:
            kcmp_ref[...] = out.astype(BF16)
        else:
            vcmpt_ref[...] = out.T.astype(BF16)


def _compress(kc, vc, prep, layer, batch, seq):
    nrow = seq // CMP_STRIDE
    wide = CMP_STRIDE * KV_WIDTH
    kcr = kc.reshape(batch, nrow, wide)
    vcr = vc.reshape(batch, nrow, wide)
    lay = lambda a: pl.BlockSpec((None,) + a.shape[1:], lambda b: (layer,) + (0,) * (a.ndim - 1))
    consts = (prep["pos"], prep["w1a"], prep["w1b"], prep["b1"], prep["w2"], prep["b2"])
    return pl.pallas_call(
        _compress_kernel,
        grid=(batch,),
        in_specs=[pl.BlockSpec((None, nrow, wide), lambda b: (b, 0, 0))] * 2 + [lay(a) for a in consts],
        out_specs=[pl.BlockSpec((None, nrow, KV_WIDTH), lambda b: (b, 0, 0)),
                   pl.BlockSpec((None, KV_WIDTH, nrow), lambda b: (b, 0, 0))],
        out_shape=[jax.ShapeDtypeStruct((batch, nrow, KV_WIDTH), BF16),
                   jax.ShapeDtypeStruct((batch, KV_WIDTH, nrow), BF16)],
        compiler_params=_cparams("parallel"),
        name="nsa_compress",
    )(kcr, vcr, *consts)


def _prep_compress(cmp_pos, cmp_w1, cmp_b1, cmp_w2, cmp_b2):
    assert NSA_KV_GROUPS == 2
    nl = cmp_w1.shape[0]
    half = CMP_BLOCK // 2
    w1 = cmp_w1.astype(BF16).reshape(nl, 2, CMP_BLOCK, HEAD_DIM, CMP_HIDDEN)

    def block_diag(w, axis):
        z = jnp.zeros_like(w)
        return jnp.stack([jnp.concatenate([w, z], axis=-1), jnp.concatenate([z, w], axis=-1)], axis=axis)

    def expand(w):
        return block_diag(w, 3).reshape(nl, 2, half * 2 * HEAD_DIM, 2 * CMP_HIDDEN)

    def pos_rows(p):
        return jnp.broadcast_to(p[:, :, :, None, :], (nl, 2, half, 2, HEAD_DIM)).reshape(nl, 2, 1, half * 2 * HEAD_DIM)

    w2 = block_diag(cmp_w2.astype(BF16), 2).reshape(nl, 2, 2 * CMP_HIDDEN, 2 * HEAD_DIM)
    return {
        "pos": jnp.stack([pos_rows(cmp_pos[:, :, :half]), pos_rows(cmp_pos[:, :, half:])], axis=2),
        "w1a": expand(w1[:, :, :half]),
        "w1b": expand(w1[:, :, half:]),
        "b1": jnp.tile(cmp_b1, (1, 1, 2))[:, :, None, :],
        "w2": w2,
        "b2": jnp.tile(cmp_b2, (1, 1, 2))[:, :, None, :],
    }


def _t5_bucket_np(dist):
    n = np.maximum(dist, 0)
    exact = REL_BUCKETS // 2
    nf = np.maximum(n, exact).astype(np.float64)
    large = exact + (np.log(nf / exact) / math.log(REL_MAX_DIST / exact) * (REL_BUCKETS - exact)).astype(np.int64)
    return np.where(n < exact, n, np.minimum(large, REL_BUCKETS - 1)).astype(np.int32)


MASKED_BUCKET = REL_BUCKETS


def _bucket_maps(seq):
    j = np.arange(K_TILE)[:, None]
    i = np.arange(Q_TILE)[None, :]
    tiles = []
    for delta in range(N_BIAS_TABLES):
        d = Q_TILE * delta + i - j
        valid = (d >= 0) & ((d < WINDOW) if delta >= 4 else True)
        tiles.append(np.where(valid, _t5_bucket_np(d), MASKED_BUCKET))
    c = np.arange(seq // CMP_STRIDE)[:, None]
    t = np.arange(seq)[None, :]
    dc = t - (c * CMP_STRIDE + CMP_BLOCK - 1)
    cmp_map = np.where(dc >= 0, _t5_bucket_np(dc), MASKED_BUCKET)
    return np.stack(tiles).astype(np.int32), cmp_map.astype(np.int32)


def _tables_kernel(tab_ref, bkt_ref, bkc_ref, tb_ref, bc_ref):
    h = pl.program_id(0)
    for src, dst in ((bkt_ref, tb_ref), (bkc_ref, bc_ref)):
        bk = src[...]
        out = jnp.zeros(bk.shape, F32)
        for b in range(REL_BUCKETS + 1):
            out = jnp.where(bk == b, tab_ref[b, h], out)
        dst[...] = (out * LOG2E).astype(dst.dtype)


def _nsa_tables(rel_table, seq):
    bkt, bkc = _bucket_maps(seq)
    tab = jnp.concatenate([rel_table[:, np.asarray(HEAD_ORDER)],
                           jnp.full((1, NSA_HEADS), NEG_INF, F32)], axis=0)
    nc = seq // CMP_STRIDE
    return pl.pallas_call(
        _tables_kernel,
        grid=(NSA_HEADS,),
        in_specs=[
            pl.BlockSpec(memory_space=pltpu.SMEM),
            pl.BlockSpec(bkt.shape, lambda h: (0, 0, 0)),
            pl.BlockSpec(bkc.shape, lambda h: (0, 0)),
        ],
        out_specs=[pl.BlockSpec((N_BIAS_TABLES, None, K_TILE, Q_TILE), lambda h: (0, h, 0, 0)),
                   pl.BlockSpec((None, nc, seq), lambda h: (h, 0, 0))],
        out_shape=[jax.ShapeDtypeStruct((N_BIAS_TABLES, NSA_HEADS, K_TILE, Q_TILE), BF16),
                   jax.ShapeDtypeStruct((NSA_HEADS, nc, seq), F32)],
        compiler_params=_cparams("parallel"),
        name="nsa_bias_tables",
    )(tab, jnp.asarray(bkt), jnp.asarray(bkc))


def _nsa_consts(seq):
    nc = seq // CMP_STRIDE
    nb = seq // SEL_BLOCK
    c = np.arange(nc)
    blk = np.arange(nb)
    c_lo, c_hi = c * CMP_STRIDE, c * CMP_STRIDE + CMP_BLOCK - 1
    s_lo, s_hi = blk * SEL_BLOCK, blk * SEL_BLOCK + SEL_BLOCK - 1
    ovt = (c_lo[None, :] <= s_hi[:, None]) & (c_hi[None, :] >= s_lo[:, None])
    ovt[:, nc - 1] = False
    return jnp.asarray(ovt, BF16)


def _nsa_kernel(q_ref, kcmp_ref, vcmpt_ref, ks_ref, vs_ref, kw_ref, vw_ref, misc_ref, biasc_ref, tb_ref,
                ovt_ref, o_ref, qpt_ref, vst_ref, vwt_ref, madd_ref, m_ref, acc_ref, mw_ref, accw_ref, ocmp_ref):
    n = pl.program_id(1)
    nkt = ks_ref.shape[0] // K_TILE
    nsel_blocks = ovt_ref.shape[0]
    nhb = NSA_HEADS
    cols = lambda hb: slice(hb * Q_TILE, (hb + 1) * Q_TILE)
    frow = lax.broadcasted_iota(jnp.int32, (KV_WIDTH, Q_TILE), 0)
    low = frow < HEAD_DIM

    @pl.when(n == 0)
    def _():
        arow = lax.broadcasted_iota(jnp.int32, (BF16_SUBLANES, K_TILE), 0)
        ones_row = jnp.where(arow == 0, 1.0, 0.0).astype(BF16)
        for t2 in range(nkt):
            rows = slice(t2 * K_TILE, (t2 + 1) * K_TILE)
            for src, dst in ((vs_ref, vst_ref), (vw_ref, vwt_ref)):
                dst[t2, 0:KV_WIDTH, :] = src[rows, :].astype(F32).T.astype(BF16)
                dst[t2, KV_WIDTH:, :] = ones_row

    scale = HEAD_DIM ** -0.5 * LOG2E
    for j in range(nhb // 2):
        slab = (q_ref[:, j * LANES:(j + 1) * LANES].T * scale).astype(BF16)
        zero = jnp.zeros_like(slab)
        qpt_ref[:, cols(2 * j)] = jnp.where(low, slab, zero)
        qpt_ref[:, cols(2 * j + 1)] = jnp.where(low, zero, slab)
    qpt = qpt_ref[...]

    pair_cols = lambda pair: slice(2 * pair * Q_TILE, (2 * pair + 2) * Q_TILE)

    def branch(k_ref, vt_ref, mx_ref, ac_ref, selected):
        def scores(t2):
            start = pl.multiple_of(t2 * K_TILE, K_TILE)
            k = k_ref[pl.ds(start, K_TILE), :]
            return [_dot(k, qpt_ref[:, pair_cols(pair)]) for pair in range(nhb // 2)]

        def softmax_pv(t2, s_pairs):
            vt = vt_ref[t2]
            delta = n - 2 * t2
            ti = jnp.minimum(delta, FAR_TABLE) if selected else delta
            blk0 = t2 * (K_TILE // SEL_BLOCK)
            for pair in range(nhb // 2):
                c2 = pair_cols(pair)
                s2 = s_pairs[pair]
                ps, alphas = [], []
                for half in range(2):
                    hb = 2 * pair + half
                    s = s2[:, half * Q_TILE:(half + 1) * Q_TILE].astype(BF16) + tb_ref[ti, hb]
                    if selected:
                        s = jnp.concatenate(
                            [s[j * SEL_BLOCK:(j + 1) * SEL_BLOCK]
                             + madd_ref[half, pl.ds(blk0 + j, 1), :].astype(BF16)
                             for j in range(K_TILE // SEL_BLOCK)], axis=0)
                    m_prev = mx_ref[:, cols(hb)]
                    m_new = jnp.maximum(m_prev, jnp.max(s, axis=0, keepdims=True).astype(F32))
                    mx_ref[:, cols(hb)] = m_new
                    alphas.append(jnp.exp2(m_prev - m_new))
                    ps.append(jnp.exp2(s - m_new.astype(BF16)))
                pv = _dot(vt, jnp.concatenate(ps, axis=1))
                ac_ref[:, c2] = ac_ref[:, c2] * jnp.concatenate(alphas, axis=1) + pv

        def issue(first, k):
            return [scores(first + i) for i in range(k)]

        def finish(first, ss):
            for i, s_pairs in enumerate(ss):
                softmax_pv(first + i, s_pairs)

        def init():
            mx_ref[...] = jnp.full_like(mx_ref, -jnp.inf)
            ac_ref[...] = jnp.zeros_like(ac_ref)

        def result():
            acc = ac_ref[...]
            return acc[0:KV_WIDTH] / acc[KV_WIDTH:KV_WIDTH + 1]

        return init, issue, finish, result

    sel_init, sel_issue, sel_finish, sel_result = branch(ks_ref, vst_ref, m_ref, acc_ref, True)
    win_init, win_issue, win_finish, win_result = branch(kw_ref, vwt_ref, mw_ref, accw_ref, False)

    tq = n * Q_TILE + lax.broadcasted_iota(jnp.int32, (1, Q_TILE), 1)

    def compressed_and_select(win_first, win_k):
        has_block = (tq >= CMP_BLOCK - 1).astype(F32)
        sc = _dot(kcmp_ref[...], qpt)
        win_scores = win_issue(win_first, win_k)
        psum = [None, None]
        pcs = []
        for hb in range(nhb):
            s = sc[:, cols(hb)] + biasc_ref[hb]
            m = jnp.max(s, axis=0, keepdims=True)
            e = jnp.exp2(s - m)
            p = e / jnp.sum(e, axis=0, keepdims=True) * has_block
            pcs.append(p.astype(BF16))
            g = hb % 2
            psum[g] = p if psum[g] is None else psum[g] + p
        ocmp_ref[...] = _dot(vcmpt_ref[...], jnp.concatenate(pcs, axis=1))

        blk = lax.broadcasted_iota(jnp.int32, (nsel_blocks, Q_TILE), 0)
        blk_f = blk.astype(F32)
        cur = tq // SEL_BLOCK
        forced = ((blk == 0) | (blk == cur) | (blk == cur - 1)).astype(F32)
        for g in range(NSA_KV_GROUPS):
            imp = _dot_f32_rhs(ovt_ref[...], psum[g])
            score = jnp.where(blk <= cur, imp + FORCE_SCORE * forced, -FORCE_SCORE)
            sel = jnp.zeros(score.shape, F32)
            for _ in range(SEL_TOPK):
                mx = jnp.max(score, axis=0, keepdims=True)
                first = jnp.min(jnp.where(score == mx, blk_f, float(nsel_blocks)), axis=0, keepdims=True)
                hit = blk_f == first
                sel = jnp.where(hit, 1.0, sel)
                score = jnp.where(hit, -jnp.inf, score)
            madd_ref[g] = (sel - 1.0) * (-NEG_INF)
        win_finish(win_first, win_scores)

    t2_diag = n // 2
    win_full = WINDOW // K_TILE + 1
    win_init()

    @pl.when(t2_diag >= win_full - 1)
    def _():
        compressed_and_select(t2_diag - (win_full - 1), win_full)

    for k in range(1, win_full):
        @pl.when(t2_diag + 1 == k)
        def _(k=k):
            compressed_and_select(0, k)

    sel_init()
    count = t2_diag + 1

    def full_blocks(i, carry):
        sel_finish(ATTN_UNROLL * i, sel_issue(ATTN_UNROLL * i, ATTN_UNROLL))
        return carry

    lax.fori_loop(0, count // ATTN_UNROLL, full_blocks, 0)
    rest_first = (count // ATTN_UNROLL) * ATTN_UNROLL
    for k in range(1, ATTN_UNROLL):
        @pl.when(count % ATTN_UNROLL == k)
        def _(k=k):
            sel_finish(rest_first, sel_issue(rest_first, k))

    o_cmp = ocmp_ref[...]
    o_sel = sel_result()
    o_win = win_result()

    gates = jax.nn.sigmoid(misc_ref[...]).T

    def gate(branch, hb):
        col = GATE_LANE0 + branch * NSA_HEADS + HEAD_ORDER[hb]
        return gates[col:col + 1, :]

    for j in range(nhb // 2):
        outs = []
        for hb in (2 * j, 2 * j + 1):
            outs.append(gate(0, hb) * o_cmp[:, cols(hb)] + gate(1, hb) * o_sel[:, cols(hb)]
                        + gate(2, hb) * o_win[:, cols(hb)])
        o_ref[:, j * LANES:(j + 1) * LANES] = jnp.where(low, outs[0], outs[1]).T


def _nsa_attention(q, kcmp, vcmpt, kv, misc, tb, biasc, ovt, batch, seq):
    nq = seq // Q_TILE
    nc = seq // CMP_STRIDE
    nkt = seq // K_TILE
    cols_all = NSA_HEADS * Q_TILE
    vrows = KV_WIDTH + BF16_SUBLANES
    kvspec = lambda c: pl.BlockSpec((seq, KV_WIDTH), lambda b, n: (b, c))
    return pl.pallas_call(
        _nsa_kernel,
        grid=(batch, nq),
        in_specs=[
            pl.BlockSpec((Q_TILE, NSA_WIDTH), lambda b, n: (b * nq + n, 0)),
            pl.BlockSpec((None, nc, KV_WIDTH), lambda b, n: (b, 0, 0)),
            pl.BlockSpec((None, KV_WIDTH, nc), lambda b, n: (b, 0, 0)),
            kvspec(0), kvspec(1), kvspec(2), kvspec(3),
            pl.BlockSpec((Q_TILE, LANES), lambda b, n: (b * nq + n, 0)),
            pl.BlockSpec((NSA_HEADS, nc, Q_TILE), lambda b, n: (0, 0, n)),
            pl.BlockSpec(tb.shape, lambda b, n: (0, 0, 0, 0)),
            pl.BlockSpec(ovt.shape, lambda b, n: (0, 0)),
        ],
        out_specs=pl.BlockSpec((Q_TILE, NSA_WIDTH), lambda b, n: (b * nq + n, 0)),
        out_shape=jax.ShapeDtypeStruct((batch * seq, NSA_WIDTH), F32),
        scratch_shapes=[
            pltpu.VMEM((KV_WIDTH, cols_all), BF16),
            pltpu.VMEM((nkt, vrows, K_TILE), BF16),
            pltpu.VMEM((nkt, vrows, K_TILE), BF16),
            pltpu.VMEM((NSA_KV_GROUPS, seq // SEL_BLOCK, Q_TILE), F32),
            pltpu.VMEM((1, cols_all), F32),
            pltpu.VMEM((vrows, cols_all), F32),
            pltpu.VMEM((1, cols_all), F32),
            pltpu.VMEM((vrows, cols_all), F32),
            pltpu.VMEM((KV_WIDTH, cols_all), F32),
        ],
        compiler_params=_cparams("arbitrary", "arbitrary"),
        name="nsa_attention",
    )(q, kcmp, vcmpt, kv, kv, kv, kv, misc, biasc, tb, ovt)


def _softplus(x):
    return jnp.maximum(x, 0.0) + jnp.log1p(jnp.exp(-jnp.abs(x)))


def _ssd_kernel(xbc_ref, z_ref, misc_ref, convw_ref, convb_ref, dtb_ref, alog_ref, dskip_ref, gain_ref,
                tri_ref, e1_ref, o_ref, prev_ref, h_ref):
    c = pl.program_id(1)
    L = SSD_CHUNK
    gw = SSM_INNER // SSM_GROUPS
    hpg = SSM_HEADS // SSM_GROUPS

    @pl.when(c == 0)
    def _():
        prev_ref[...] = jnp.zeros_like(prev_ref)
        h_ref[...] = jnp.zeros_like(h_ref)

    x = xbc_ref[...]
    ng = L // CONV_TAIL
    x3 = x.reshape(ng, CONV_TAIL, CONV_CH)
    sub = lax.broadcasted_iota(jnp.int32, (1, CONV_TAIL, 1), 1)
    acc3 = convb_ref[...] + x3 * convw_ref[CONV_WIDTH - 1:CONV_WIDTH, :]
    for k in range(1, CONV_WIDTH):
        rot = pltpu.roll(x3, k, 1)
        before = jnp.concatenate([pltpu.roll(prev_ref[...], k, 0)[None], rot[:ng - 1]], axis=0)
        acc3 = acc3 + jnp.where(sub >= k, rot, before) * convw_ref[CONV_WIDTH - 1 - k:CONV_WIDTH - k, :]
    prev_ref[...] = x[L - CONV_TAIL:L]
    xa = _silu(acc3).reshape(L, CONV_CH)
    xs = xa[:, :SSM_INNER]
    bm = xa[:, SSM_INNER:SSM_INNER + SSM_GROUPS * SSM_STATE]
    cm = xa[:, SSM_INNER + SSM_GROUPS * SSM_STATE:]

    dt = _softplus(misc_ref[...] + dtb_ref[...])
    da = dt * (-jnp.exp(alog_ref[...]))
    cs = _dot_f32_rhs(tri_ref[...], da) * LOG2E
    cs_t = cs.T
    dt_t = dt.T
    ecs = _dot_f32x2_lhs(jnp.exp2(cs), e1_ref[...])
    to_end = dt * jnp.exp2(cs[L - 1:L, :] - cs)
    xw_b = (xs * _dot_f32x2_lhs(to_end, e1_ref[...])).astype(BF16)
    xs_b = xs.astype(BF16)
    state_decay = ecs[L - 1:L, :]

    li = lax.broadcasted_iota(jnp.int32, (L, L), 0)
    si = lax.broadcasted_iota(jnp.int32, (L, L), 1)
    causal = li >= si
    low = si < SSM_HEAD_DIM

    ys = []
    for g in range(SSM_GROUPS):
        bg = bm[:, g * SSM_STATE:(g + 1) * SSM_STATE]
        cg = cm[:, g * SSM_STATE:(g + 1) * SSM_STATE].astype(BF16)
        cb = _dot_nt(cg, bg.astype(BF16))
        h_g = h_ref[:, g * gw:(g + 1) * gw]
        y_off = _dot(cg, h_g.astype(BF16)) * ecs[:, g * gw:(g + 1) * gw]
        for pr in range(hpg // 2):
            h0 = g * hpg + 2 * pr
            gs = []
            for hh in (h0, h0 + 1):
                ln = DT_LANE0 + hh
                col = jnp.broadcast_to(cs[:, ln:ln + 1], (L, L))
                dec = jnp.exp2(jnp.where(causal, col - cs_t[ln:ln + 1, :], NEG_INF))
                gs.append((cb * dec * dt_t[ln:ln + 1, :]).astype(BF16))
            ch = slice(h0 * SSM_HEAD_DIM, (h0 + 2) * SSM_HEAD_DIM)
            xpair = xs_b[:, ch]
            zero = jnp.zeros_like(xpair)
            rhs = jnp.concatenate([jnp.where(low, xpair, zero), jnp.where(low, zero, xpair)], axis=0)
            y_diag = _dot(jnp.concatenate(gs, axis=1), rhs)
            off = slice(2 * pr * SSM_HEAD_DIM, (2 * pr + 2) * SSM_HEAD_DIM)
            ys.append(y_diag + y_off[:, off] + xs[:, ch] * dskip_ref[:, ch])
        st = _dot(bg.T.astype(BF16), xw_b[:, g * gw:(g + 1) * gw])
        h_ref[:, g * gw:(g + 1) * gw] = h_g * state_decay[:, g * gw:(g + 1) * gw] + st

    y = jnp.concatenate(ys, axis=1) * _silu(z_ref[...])
    outs = []
    for g in range(SSM_GROUPS):
        outs.append(_rms(y[:, g * gw:(g + 1) * gw], gain_ref[:, g * gw:(g + 1) * gw]))
    o_ref[...] = jnp.concatenate(outs, axis=1).astype(BF16)


def _ssd_consts():
    lane = np.arange(LANES)
    tri = (np.arange(SSD_CHUNK)[:, None] >= np.arange(SSD_CHUNK)[None, :])
    head1 = np.arange(SSM_INNER) // SSM_HEAD_DIM
    e1 = (lane[:, None] - DT_LANE0) == head1[None, :]
    return jnp.asarray(tri, BF16), jnp.asarray(e1, BF16)


def _ssd(xbc, z, misc, prm, consts, layer, batch, seq):
    nch = seq // SSD_CHUNK
    tok = lambda n: pl.BlockSpec((SSD_CHUNK, n), lambda b, c: (b * nch + c, 0))
    lay = lambda a: pl.BlockSpec((None,) + a.shape[1:], lambda b, c: (layer,) + (0,) * (a.ndim - 1))
    full = lambda a: pl.BlockSpec(a.shape, lambda b, c: (0,) * a.ndim)
    params = (prm["conv_w"], prm["conv_b"], prm["dt_bias"], prm["a_log"], prm["d_skip"], prm["ssm_gain"])
    return pl.pallas_call(
        _ssd_kernel,
        grid=(batch, nch),
        in_specs=[tok(CONV_CH), tok(SSM_INNER), tok(LANES)] + [lay(a) for a in params] + [full(a) for a in consts],
        out_specs=tok(SSM_INNER),
        out_shape=jax.ShapeDtypeStruct((batch * seq, SSM_INNER), BF16),
        scratch_shapes=[pltpu.VMEM((CONV_TAIL, CONV_CH), F32), pltpu.VMEM((SSM_STATE, SSM_INNER), F32)],
        compiler_params=_cparams("parallel", "arbitrary"),
        name="ssd",
    )(xbc, z, misc, *params, *consts)


def _prep_ssd_params(conv_w, conv_b, dt_bias, a_log, d_skip, ssm_out_norm):
    nl = conv_w.shape[0]

    def dt_lanes(v):
        out = jnp.zeros((nl, 1, LANES), F32)
        return out.at[:, 0, DT_LANE0:DT_LANE0 + SSM_HEADS].set(v)

    return {
        "conv_w": conv_w,
        "conv_b": conv_b[:, None, :],
        "dt_bias": dt_lanes(dt_bias),
        "a_log": dt_lanes(a_log),
        "d_skip": jnp.repeat(d_skip, SSM_HEAD_DIM, axis=-1)[:, None, :],
        "ssm_gain": ssm_out_norm[:, None, :],
    }


def _outproj_kernel(x_ref, oa_ref, os_ref, g_ref, wa_ref, ws_ref, o_ref):
    an = _rms(oa_ref[...], g_ref[...]).astype(BF16)
    o_ref[...] = x_ref[...] + _dot(an, wa_ref[...]) + _dot(os_ref[...], ws_ref[...])


def _outproj(x, o_attn, o_ssm, gain, wa, ws, layer):
    t, d = x.shape
    tm = RESID_TM
    return pl.pallas_call(
        _outproj_kernel,
        grid=(t // tm,),
        in_specs=[
            pl.BlockSpec((tm, d), lambda i: (i, 0)),
            pl.BlockSpec((tm, NSA_WIDTH), lambda i: (i, 0)),
            pl.BlockSpec((tm, SSM_INNER), lambda i: (i, 0)),
            pl.BlockSpec((None, 1, NSA_WIDTH), lambda i: (layer, 0, 0)),
            pl.BlockSpec((None, NSA_WIDTH, d), lambda i: (layer, 0, 0)),
            pl.BlockSpec((None, SSM_INNER, d), lambda i: (layer, 0, 0)),
        ],
        out_specs=pl.BlockSpec((tm, d), lambda i: (i, 0)),
        out_shape=jax.ShapeDtypeStruct((t, d), F32),
        compiler_params=_cparams("parallel"),
        name="outproj",
    )(x, o_attn, o_ssm, gain, wa, ws)


def _perm_heads(a, axis):
    idx = np.concatenate([np.arange(h * HEAD_DIM, (h + 1) * HEAD_DIM) for h in HEAD_ORDER])
    return jnp.take(a, jnp.asarray(idx), axis=axis)


def _ple_kernel(x_ref, p_ref, g_ref, wg_ref, wp_ref, fg_ref, o_ref, *, final):
    x = x_ref[...]
    xn = _rms(x, g_ref[...]).astype(BF16)
    gate = jax.nn.sigmoid(_dot(xn, wg_ref[...]))
    y = x + gate * _dot(p_ref[...].astype(BF16), wp_ref[...])
    if final:
        y = _rms(y, fg_ref[...])
    o_ref[...] = y


def _ple(x, p, gain, wg, wp, final_gain, layer, final):
    t, d = x.shape
    tm = RESID_TM
    return pl.pallas_call(
        functools.partial(_ple_kernel, final=final),
        grid=(t // tm,),
        in_specs=[
            pl.BlockSpec((tm, d), lambda i: (i, 0)),
            pl.BlockSpec((None, tm, PLE_DIM), lambda i: (layer, i, 0)),
            pl.BlockSpec((None, 1, d), lambda i: (layer, 0, 0)),
            pl.BlockSpec((None, d, d), lambda i: (layer, 0, 0)),
            pl.BlockSpec((None, PLE_DIM, d), lambda i: (layer, 0, 0)),
            pl.BlockSpec((1, d), lambda i: (0, 0)),
        ],
        out_specs=pl.BlockSpec((tm, d), lambda i: (i, 0)),
        out_shape=jax.ShapeDtypeStruct((t, d), F32),
        compiler_params=_cparams("parallel"),
        name="ple",
    )(x, p, gain, wg, wp, final_gain)


def kernel(x, p, ffn1_norm, ffn1_w_in, ffn1_w_out, mix_norm, w_mix_in, cmp_pos, cmp_w1, cmp_b1, cmp_w2,
           cmp_b2, rel_table, nsa_out_norm, conv_w, conv_b, dt_bias, a_log, d_skip, ssm_out_norm, w_mix_out,
           ffn2_norm, ffn2_w_in, ffn2_w_out, ple_norm, ple_gate_w, ple_proj_w, final_norm):
    batch, seq, d = x.shape
    depth = p.shape[0]
    t = batch * seq
    bf = lambda a: a.astype(BF16)
    row = lambda a: a[:, None, :]

    ffn1_in, ffn1_out, ffn2_in, ffn2_out = bf(ffn1_w_in), bf(ffn1_w_out), bf(ffn2_w_in), bf(ffn2_w_out)
    w_proj = _prep_inproj_weight(w_mix_in)
    wo_attn = bf(_perm_heads(w_mix_out[:, :NSA_WIDTH], axis=1))
    wo_ssm = bf(w_mix_out[:, NSA_WIDTH:])
    nsa_gain = row(_perm_heads(nsa_out_norm, axis=1))
    ssd_prm = _prep_ssd_params(conv_w, conv_b, dt_bias, a_log, d_skip, ssm_out_norm)
    ssd_consts = _ssd_consts()
    tb, biasc = _nsa_tables(rel_table, seq)
    ovt = _nsa_consts(seq)
    cprep = _prep_compress(cmp_pos, cmp_w1, cmp_b1, cmp_w2, cmp_b2)
    wg, wp = bf(ple_gate_w), bf(ple_proj_w)
    p2 = p.reshape(depth, t, PLE_DIM)
    fgain = final_norm[None, :]

    h = x.reshape(t, d)
    for i in range(depth):
        h = _ffn(h, row(ffn1_norm), ffn1_in, ffn1_out, i)
        q, kc, vc, kv, misc, z, xbc = _inproj(h, row(mix_norm), w_proj, i)
        kcmp, vcmp = _compress(kc, vc, cprep, i, batch, seq)
        o_attn = _nsa_attention(q, kcmp, vcmp, kv, misc, tb, biasc, ovt, batch, seq)
        o_ssm = _ssd(xbc, z, misc, ssd_prm, ssd_consts, i, batch, seq)
        h = _outproj(h, o_attn, o_ssm, nsa_gain, wo_attn, wo_ssm, i)
        h = _ffn(h, row(ffn2_norm), ffn2_in, ffn2_out, i)
        h = _ple(h, p2, row(ple_norm), wg, wp, fgain, i, final=(i == depth - 1))
    return h.reshape(batch, seq, d)
```

```python
import functools
import math

import numpy as np
import jax
import jax.numpy as jnp
from jax import lax
from jax.experimental import pallas as pl
from jax.experimental.pallas import tpu as pltpu

F32 = jnp.float32
BF16 = jnp.bfloat16

D_MODEL = 1024
DEPTH = 4
PLE_DIM = 256
D_FF = 2816
EPS = 1e-6
NEG_INF = -1e30
FORCE_SCORE = 1e4

NSA_HEADS = 8
NSA_KV_GROUPS = 2
NSA_REP = NSA_HEADS // NSA_KV_GROUPS
HEAD_DIM = 64
NSA_WIDTH = NSA_HEADS * HEAD_DIM
KV_WIDTH = NSA_KV_GROUPS * HEAD_DIM
CMP_BLOCK = 32
CMP_STRIDE = 16
CMP_HIDDEN = 256
SEL_BLOCK = 64
SEL_TOPK = 8
WINDOW = 512
REL_BUCKETS = 32
REL_MAX_DIST = 128

SSM_HEADS = 16
SSM_HEAD_DIM = 64
SSM_INNER = SSM_HEADS * SSM_HEAD_DIM
SSM_GROUPS = 2
SSM_STATE = 128
CONV_WIDTH = 4
SSD_CHUNK = 128
CONV_CH = SSM_INNER + 2 * SSM_GROUPS * SSM_STATE

LANES = 128
VMEM_LIMIT_BYTES = 48 * 1024 * 1024

FFN_TM = 1024
FFN_TF = 256
PROJ_TM = 512
RESID_TM = 1024
Q_TILE = 128
K_TILE = 256
HEAD_ORDER = (0, 4, 1, 5, 2, 6, 3, 7)
GATE_LANE0 = 0
DT_LANE0 = 3 * NSA_HEADS
N_BIAS_TABLES = 6
FAR_TABLE = 3
LOG2E = math.log2(math.e)
BF16_SUBLANES = 16
CONV_TAIL = 8


def _dot(a, b):
    return jnp.dot(a, b, preferred_element_type=F32)


def _dot_nt(a, b):
    return lax.dot_general(a, b, (((1,), (1,)), ((), ())), preferred_element_type=F32)


def _split3(v):
    hi = v.astype(BF16)
    r = v - hi.astype(F32)
    mid = r.astype(BF16)
    lo = (r - mid.astype(F32)).astype(BF16)
    return hi, mid, lo


def _dot_f32x2_lhs(v, e):
    hi = v.astype(BF16)
    lo = (v - hi.astype(F32)).astype(BF16)
    return _dot(hi, e) + _dot(lo, e)


def _dot_f32_rhs(e, v):
    hi, mid, lo = _split3(v)
    return _dot(e, hi) + _dot(e, mid) + _dot(e, lo)


def _rms(x, g):
    ms = jnp.mean(x * x, axis=-1, keepdims=True)
    return x * lax.rsqrt(ms + EPS) * g


def _silu(x):
    return x * jax.nn.sigmoid(x)


def _cparams(*sem):
    return pltpu.CompilerParams(dimension_semantics=sem, vmem_limit_bytes=VMEM_LIMIT_BYTES)


def _ffn_kernel(x_ref, g_ref, wg_ref, wu_ref, wo_ref, o_ref, xn_ref, acc_ref):
    j = pl.program_id(1)

    @pl.when(j == 0)
    def _():
        xn_ref[...] = _rms(x_ref[...], g_ref[...]).astype(BF16)
        acc_ref[...] = jnp.zeros_like(acc_ref)

    xn = xn_ref[...]
    gate = _dot(xn, wg_ref[...])
    up = _dot(xn, wu_ref[...])
    h = (_silu(gate) * up).astype(BF16)
    acc_ref[...] += _dot(h, wo_ref[...])

    @pl.when(j == pl.num_programs(1) - 1)
    def _():
        o_ref[...] = x_ref[...] + 0.5 * acc_ref[...]


def _ffn(x, gain, w_in, w_out, layer):
    t, d = x.shape
    nf = D_FF // FFN_TF
    return pl.pallas_call(
        _ffn_kernel,
        grid=(t // FFN_TM, nf),
        in_specs=[
            pl.BlockSpec((FFN_TM, d), lambda i, j: (i, 0)),
            pl.BlockSpec((None, 1, d), lambda i, j: (layer, 0, 0)),
            pl.BlockSpec((None, d, FFN_TF), lambda i, j: (layer, 0, j)),
            pl.BlockSpec((None, d, FFN_TF), lambda i, j: (layer, 0, j + nf)),
            pl.BlockSpec((None, FFN_TF, d), lambda i, j: (layer, j, 0)),
        ],
        out_specs=pl.BlockSpec((FFN_TM, d), lambda i, j: (i, 0)),
        out_shape=jax.ShapeDtypeStruct((t, d), F32),
        scratch_shapes=[pltpu.VMEM((FFN_TM, d), BF16), pltpu.VMEM((FFN_TM, d), F32)],
        compiler_params=_cparams("parallel", "arbitrary"),
        name="ffn",
    )(x, gain, w_in, w_in, w_out)


_C_Q = (0, NSA_WIDTH)
_C_KC = (_C_Q[1], _C_Q[1] + KV_WIDTH)
_C_VC = (_C_KC[1], _C_KC[1] + KV_WIDTH)
_C_KV = (_C_VC[1], _C_VC[1] + 4 * KV_WIDTH)
_C_MISC = (_C_KV[1], _C_KV[1] + LANES)
_C_Z = (_C_MISC[1], _C_MISC[1] + SSM_INNER)
_C_XBC = (_C_Z[1], _C_Z[1] + CONV_CH)
PROJ_COLS = _C_XBC[1]


def _inproj_kernel(x_ref, g_ref, w_ref, q_ref, kc_ref, vc_ref, kv_ref, misc_ref, z_ref, xbc_ref, rows_ref):
    xn = _rms(x_ref[...], g_ref[...]).astype(BF16)
    q_ref[...] = _dot(xn, w_ref[:, _C_Q[0]:_C_Q[1]])
    nrow = rows_ref.shape[0] // CMP_STRIDE
    for cols, dst in ((_C_KC, kc_ref), (_C_VC, vc_ref)):
        rows_ref[...] = _dot(xn, w_ref[:, cols[0]:cols[1]])
        for l in range(CMP_STRIDE):
            dst[:, l * KV_WIDTH:(l + 1) * KV_WIDTH] = rows_ref[pl.ds(l, nrow, stride=CMP_STRIDE), :]
    kv_ref[...] = _dot(xn, w_ref[:, _C_KV[0]:_C_KV[1]]).astype(BF16)
    misc_ref[...] = _dot(xn, w_ref[:, _C_MISC[0]:_C_MISC[1]])
    z_ref[...] = _dot(xn, w_ref[:, _C_Z[0]:_C_Z[1]])
    xbc_ref[...] = _dot(xn, w_ref[:, _C_XBC[0]:_C_XBC[1]])


def _inproj(x, gain, w, layer):
    t, d = x.shape
    tm = PROJ_TM
    grouped = (tm // CMP_STRIDE, t // CMP_STRIDE, CMP_STRIDE * KV_WIDTH, F32)
    outs = ((tm, t, NSA_WIDTH, F32), grouped, grouped, (tm, t, 4 * KV_WIDTH, BF16), (tm, t, LANES, F32),
            (tm, t, SSM_INNER, F32), (tm, t, CONV_CH, F32))
    return pl.pallas_call(
        _inproj_kernel,
        grid=(t // tm,),
        in_specs=[
            pl.BlockSpec((tm, d), lambda i: (i, 0)),
            pl.BlockSpec((None, 1, d), lambda i: (layer, 0, 0)),
            pl.BlockSpec((None, d, PROJ_COLS), lambda i: (layer, 0, 0)),
        ],
        out_specs=[pl.BlockSpec((rows, n), lambda i: (i, 0)) for rows, _, n, _ in outs],
        out_shape=[jax.ShapeDtypeStruct((total, n), dt) for _, total, n, dt in outs],
        scratch_shapes=[pltpu.VMEM((tm, KV_WIDTH), F32)],
        compiler_params=_cparams("parallel"),
        name="inproj",
    )(x, gain, w)


_IN_OFFS = tuple(int(v) for v in np.cumsum(
    (0, NSA_WIDTH) + (KV_WIDTH,) * 6 + (3 * NSA_HEADS, SSM_INNER, CONV_CH, SSM_HEADS)))
PREP_ROWS = 256


def _prep_inproj_kernel(w_ref, o_ref):
    o = _IN_OFFS
    o_ref[:, _C_KC[0]:_C_KV[1]] = w_ref[:, o[1]:o[7]].astype(BF16)
    o_ref[:, _C_Z[0]:_C_XBC[1]] = w_ref[:, o[8]:o[10]].astype(BF16)
    for pos, h in enumerate(HEAD_ORDER):
        o_ref[:, pos * HEAD_DIM:(pos + 1) * HEAD_DIM] = w_ref[:, h * HEAD_DIM:(h + 1) * HEAD_DIM].astype(BF16)
    ngate, ndt = o[8] - o[7], o[11] - o[10]
    m0 = _C_MISC[0]
    o_ref[:, m0:m0 + ngate] = w_ref[:, o[7]:o[8]].astype(BF16)
    o_ref[:, m0 + ngate:m0 + ngate + ndt] = w_ref[:, o[10]:o[11]].astype(BF16)
    o_ref[:, m0 + ngate + ndt:_C_MISC[1]] = jnp.zeros((o_ref.shape[0], LANES - ngate - ndt), BF16)


def _prep_inproj_weight(w_mix_in):
    nl, d, win = w_mix_in.shape
    return pl.pallas_call(
        _prep_inproj_kernel,
        grid=(nl, d // PREP_ROWS),
        in_specs=[pl.BlockSpec((None, PREP_ROWS, win), lambda l, r: (l, r, 0))],
        out_specs=pl.BlockSpec((None, PREP_ROWS, PROJ_COLS), lambda l, r: (l, r, 0)),
        out_shape=jax.ShapeDtypeStruct((nl, d, PROJ_COLS), BF16),
        compiler_params=_cparams("parallel", "parallel"),
        name="inproj_weight_layout",
    )(w_mix_in)


def _compress_kernel(kc_ref, vc_ref, pos_ref, w1a_ref, w1b_ref, b1_ref, w2_ref, b2_ref,
                     kcmp_ref, vcmpt_ref):
    nrow = kc_ref.shape[0]
    for which, src in enumerate((kc_ref, vc_ref)):
        r = src[...]
        ra = (r + pos_ref[which, 0]).astype(BF16)
        rb = (r + pos_ref[which, 1]).astype(BF16)
        ha = _dot(ra, w1a_ref[which])
        hb = _dot(rb, w1b_ref[which])
        h = ha + pltpu.roll(hb, nrow - 1, 0) + b1_ref[which]
        out = _dot(_silu(h).astype(BF16), w2_ref[which]) + b2_ref[which]
        if which == 0:
            kcmp_ref[...] = out.astype(BF16)
        else:
            vcmpt_ref[...] = out.T.astype(BF16)


def _compress(kc, vc, prep, layer, batch, seq):
    nrow = seq // CMP_STRIDE
    wide = CMP_STRIDE * KV_WIDTH
    kcr = kc.reshape(batch, nrow, wide)
    vcr = vc.reshape(batch, nrow, wide)
    lay = lambda a: pl.BlockSpec((None,) + a.shape[1:], lambda b: (layer,) + (0,) * (a.ndim - 1))
    consts = (prep["pos"], prep["w1a"], prep["w1b"], prep["b1"], prep["w2"], prep["b2"])
    return pl.pallas_call(
        _compress_kernel,
        grid=(batch,),
        in_specs=[pl.BlockSpec((None, nrow, wide), lambda b: (b, 0, 0))] * 2 + [lay(a) for a in consts],
        out_specs=[pl.BlockSpec((None, nrow, KV_WIDTH), lambda b: (b, 0, 0)),
                   pl.BlockSpec((None, KV_WIDTH, nrow), lambda b: (b, 0, 0))],
        out_shape=[jax.ShapeDtypeStruct((batch, nrow, KV_WIDTH), BF16),
                   jax.ShapeDtypeStruct((batch, KV_WIDTH, nrow), BF16)],
        compiler_params=_cparams("parallel"),
        name="nsa_compress",
    )(kcr, vcr, *consts)


def _prep_compress(cmp_pos, cmp_w1, cmp_b1, cmp_w2, cmp_b2):
    assert NSA_KV_GROUPS == 2
    nl = cmp_w1.shape[0]
    half = CMP_BLOCK // 2
    w1 = cmp_w1.astype(BF16).reshape(nl, 2, CMP_BLOCK, HEAD_DIM, CMP_HIDDEN)

    def block_diag(w, axis):
        z = jnp.zeros_like(w)
        return jnp.stack([jnp.concatenate([w, z], axis=-1), jnp.concatenate([z, w], axis=-1)], axis=axis)

    def expand(w):
        return block_diag(w, 3).reshape(nl, 2, half * 2 * HEAD_DIM, 2 * CMP_HIDDEN)

    def pos_rows(p):
        return jnp.broadcast_to(p[:, :, :, None, :], (nl, 2, half, 2, HEAD_DIM)).reshape(nl, 2, 1, half * 2 * HEAD_DIM)

    w2 = block_diag(cmp_w2.astype(BF16), 2).reshape(nl, 2, 2 * CMP_HIDDEN, 2 * HEAD_DIM)
    return {
        "pos": jnp.stack([pos_rows(cmp_pos[:, :, :half]), pos_rows(cmp_pos[:, :, half:])], axis=2),
        "w1a": expand(w1[:, :, :half]),
        "w1b": expand(w1[:, :, half:]),
        "b1": jnp.tile(cmp_b1, (1, 1, 2))[:, :, None, :],
        "w2": w2,
        "b2": jnp.tile(cmp_b2, (1, 1, 2))[:, :, None, :],
    }


def _t5_bucket_np(dist):
    n = np.maximum(dist, 0)
    exact = REL_BUCKETS // 2
    nf = np.maximum(n, exact).astype(np.float64)
    large = exact + (np.log(nf / exact) / math.log(REL_MAX_DIST / exact) * (REL_BUCKETS - exact)).astype(np.int64)
    return np.where(n < exact, n, np.minimum(large, REL_BUCKETS - 1)).astype(np.int32)


MASKED_BUCKET = REL_BUCKETS


def _bucket_maps(seq):
    j = np.arange(K_TILE)[:, None]
    i = np.arange(Q_TILE)[None, :]
    tiles = []
    for delta in range(N_BIAS_TABLES):
        d = Q_TILE * delta + i - j
        valid = (d >= 0) & ((d < WINDOW) if delta >= 4 else True)
        tiles.append(np.where(valid, _t5_bucket_np(d), MASKED_BUCKET))
    c = np.arange(seq // CMP_STRIDE)[:, None]
    t = np.arange(seq)[None, :]
    dc = t - (c * CMP_STRIDE + CMP_BLOCK - 1)
    cmp_map = np.where(dc >= 0, _t5_bucket_np(dc), MASKED_BUCKET)
    return np.stack(tiles).astype(np.int32), cmp_map.astype(np.int32)


def _tables_kernel(tab_ref, bkt_ref, bkc_ref, tb_ref, bc_ref):
    h = pl.program_id(0)
    for src, dst in ((bkt_ref, tb_ref), (bkc_ref, bc_ref)):
        bk = src[...]
        out = jnp.zeros(bk.shape, F32)
        for b in range(REL_BUCKETS + 1):
            out = jnp.where(bk == b, tab_ref[b, h], out)
        dst[...] = (out * LOG2E).astype(dst.dtype)


def _nsa_tables(rel_table, seq):
    bkt, bkc = _bucket_maps(seq)
    tab = jnp.concatenate([rel_table[:, np.asarray(HEAD_ORDER)],
                           jnp.full((1, NSA_HEADS), NEG_INF, F32)], axis=0)
    nc = seq // CMP_STRIDE
    return pl.pallas_call(
        _tables_kernel,
        grid=(NSA_HEADS,),
        in_specs=[
            pl.BlockSpec(memory_space=pltpu.SMEM),
            pl.BlockSpec(bkt.shape, lambda h: (0, 0, 0)),
            pl.BlockSpec(bkc.shape, lambda h: (0, 0)),
        ],
        out_specs=[pl.BlockSpec((N_BIAS_TABLES, None, K_TILE, Q_TILE), lambda h: (0, h, 0, 0)),
                   pl.BlockSpec((None, nc, seq), lambda h: (h, 0, 0))],
        out_shape=[jax.ShapeDtypeStruct((N_BIAS_TABLES, NSA_HEADS, K_TILE, Q_TILE), BF16),
                   jax.ShapeDtypeStruct((NSA_HEADS, nc, seq), F32)],
        compiler_params=_cparams("parallel"),
        name="nsa_bias_tables",
    )(tab, jnp.asarray(bkt), jnp.asarray(bkc))


def _nsa_consts(seq):
    nc = seq // CMP_STRIDE
    nb = seq // SEL_BLOCK
    c = np.arange(nc)
    blk = np.arange(nb)
    c_lo, c_hi = c * CMP_STRIDE, c * CMP_STRIDE + CMP_BLOCK - 1
    s_lo, s_hi = blk * SEL_BLOCK, blk * SEL_BLOCK + SEL_BLOCK - 1
    ovt = (c_lo[None, :] <= s_hi[:, None]) & (c_hi[None, :] >= s_lo[:, None])
    ovt[:, nc - 1] = False
    return jnp.asarray(ovt, BF16)


def _nsa_kernel(q_ref, kcmp_ref, vcmpt_ref, ks_ref, vs_ref, kw_ref, vw_ref, misc_ref, biasc_ref, tb_ref,
                ovt_ref, o_ref, qpt_ref, vst_ref, vwt_ref, madd_ref, m_ref, acc_ref, mw_ref, accw_ref):
    n = pl.program_id(1)
    nkt = ks_ref.shape[0] // K_TILE
    nsel_blocks = ovt_ref.shape[0]
    nhb = NSA_HEADS
    cols = lambda hb: slice(hb * Q_TILE, (hb + 1) * Q_TILE)
    frow = lax.broadcasted_iota(jnp.int32, (KV_WIDTH, Q_TILE), 0)
    low = frow < HEAD_DIM

    @pl.when(n == 0)
    def _():
        arow = lax.broadcasted_iota(jnp.int32, (BF16_SUBLANES, K_TILE), 0)
        ones_row = jnp.where(arow == 0, 1.0, 0.0).astype(BF16)
        for t2 in range(nkt):
            rows = slice(t2 * K_TILE, (t2 + 1) * K_TILE)
            for src, dst in ((vs_ref, vst_ref), (vw_ref, vwt_ref)):
                dst[t2, 0:KV_WIDTH, :] = src[rows, :].astype(F32).T.astype(BF16)
                dst[t2, KV_WIDTH:, :] = ones_row

    scale = HEAD_DIM ** -0.5 * LOG2E
    for j in range(nhb // 2):
        slab = (q_ref[:, j * LANES:(j + 1) * LANES].T * scale).astype(BF16)
        zero = jnp.zeros_like(slab)
        qpt_ref[:, cols(2 * j)] = jnp.where(low, slab, zero)
        qpt_ref[:, cols(2 * j + 1)] = jnp.where(low, zero, slab)
    qpt = qpt_ref[...]

    pair_cols = lambda pair: slice(2 * pair * Q_TILE, (2 * pair + 2) * Q_TILE)

    def branch(k_ref, vt_ref, mx_ref, ac_ref, selected):
        def scores(t2):
            start = pl.multiple_of(t2 * K_TILE, K_TILE)
            k = k_ref[pl.ds(start, K_TILE), :]
            return [_dot(k, qpt_ref[:, pair_cols(pair)]) for pair in range(nhb // 2)]

        def softmax_pv(t2, s_pairs):
            vt = vt_ref[t2]
            delta = n - 2 * t2
            ti = jnp.minimum(delta, FAR_TABLE) if selected else delta
            blk0 = t2 * (K_TILE // SEL_BLOCK)
            for pair in range(nhb // 2):
                c2 = pair_cols(pair)
                s2 = s_pairs[pair]
                ps, alphas = [], []
                for half in range(2):
                    hb = 2 * pair + half
                    s = s2[:, half * Q_TILE:(half + 1) * Q_TILE].astype(BF16) + tb_ref[ti, hb]
                    if selected:
                        s = jnp.concatenate(
                            [s[j * SEL_BLOCK:(j + 1) * SEL_BLOCK]
                             + madd_ref[half, pl.ds(blk0 + j, 1), :].astype(BF16)
                             for j in range(K_TILE // SEL_BLOCK)], axis=0)
                    m_prev = mx_ref[:, cols(hb)]
                    m_new = jnp.maximum(m_prev, jnp.max(s, axis=0, keepdims=True).astype(F32))
                    mx_ref[:, cols(hb)] = m_new
                    alphas.append(jnp.exp2(m_prev - m_new))
                    ps.append(jnp.exp2(s - m_new.astype(BF16)))
                pv = _dot(vt, jnp.concatenate(ps, axis=1))
                ac_ref[:, c2] = ac_ref[:, c2] * jnp.concatenate(alphas, axis=1) + pv

        def issue(first, k):
            return [scores(first + i) for i in range(k)]

        def finish(first, ss):
            for i, s_pairs in enumerate(ss):
                softmax_pv(first + i, s_pairs)

        def init():
            mx_ref[...] = jnp.full_like(mx_ref, -jnp.inf)
            ac_ref[...] = jnp.zeros_like(ac_ref)

        def result():
            acc = ac_ref[...]
            return acc[0:KV_WIDTH] / acc[KV_WIDTH:KV_WIDTH + 1]

        return init, issue, finish, result

    sel_init, sel_issue, sel_finish, sel_result = branch(ks_ref, vst_ref, m_ref, acc_ref, True)
    win_init, win_issue, win_finish, win_result = branch(kw_ref, vwt_ref, mw_ref, accw_ref, False)

    tq = n * Q_TILE + lax.broadcasted_iota(jnp.int32, (1, Q_TILE), 1)
    t2_diag = n // 2

    def q_tile(sel_k, win_k):
        win_first = t2_diag + 1 - win_k
        win_init()
        sel_init()
        has_block = (tq >= CMP_BLOCK - 1).astype(F32)
        sc = _dot(kcmp_ref[...], qpt)
        win_scores = win_issue(win_first, win_k)
        psum = [None, None]
        pcs = []
        for hb in range(nhb):
            s = sc[:, cols(hb)] + biasc_ref[hb]
            m = jnp.max(s, axis=0, keepdims=True)
            e = jnp.exp2(s - m)
            p = e / jnp.sum(e, axis=0, keepdims=True) * has_block
            pcs.append(p.astype(BF16))
            g = hb % 2
            psum[g] = p if psum[g] is None else psum[g] + p
        o_cmp = _dot(vcmpt_ref[...], jnp.concatenate(pcs, axis=1))

        blk = lax.broadcasted_iota(jnp.int32, (nsel_blocks, Q_TILE), 0)
        blk_f = blk.astype(F32)
        cur = tq // SEL_BLOCK
        forced = ((blk == 0) | (blk == cur) | (blk == cur - 1)).astype(F32)
        imps = [_dot_f32_rhs(ovt_ref[...], psum[g]) for g in range(NSA_KV_GROUPS)]
        sel_scores = sel_issue(0, sel_k)
        for g in range(NSA_KV_GROUPS):
            imp = imps[g]
            score = jnp.where(blk <= cur, imp + FORCE_SCORE * forced, -FORCE_SCORE)
            sel = jnp.zeros(score.shape, F32)
            for _ in range(SEL_TOPK):
                mx = jnp.max(score, axis=0, keepdims=True)
                first = jnp.min(jnp.where(score == mx, blk_f, float(nsel_blocks)), axis=0, keepdims=True)
                hit = blk_f == first
                sel = jnp.where(hit, 1.0, sel)
                score = jnp.where(hit, -jnp.inf, score)
            madd_ref[g] = (sel - 1.0) * (-NEG_INF)
        win_finish(win_first, win_scores)
        sel_finish(0, sel_scores)
        o_sel = sel_result()
        o_win = win_result()

        gates = jax.nn.sigmoid(misc_ref[...]).T

        def gate(branch, hb):
            col = GATE_LANE0 + branch * NSA_HEADS + HEAD_ORDER[hb]
            return gates[col:col + 1, :]

        for j in range(nhb // 2):
            outs = []
            for hb in (2 * j, 2 * j + 1):
                outs.append(gate(0, hb) * o_cmp[:, cols(hb)] + gate(1, hb) * o_sel[:, cols(hb)]
                            + gate(2, hb) * o_win[:, cols(hb)])
            o_ref[:, j * LANES:(j + 1) * LANES] = jnp.where(low, outs[0], outs[1]).T

    win_full = WINDOW // K_TILE + 1
    for sel_k in range(1, nkt + 1):
        @pl.when(t2_diag + 1 == sel_k)
        def _(sel_k=sel_k):
            q_tile(sel_k, min(sel_k, win_full))


def _nsa_attention(q, kcmp, vcmpt, kv, misc, tb, biasc, ovt, batch, seq):
    nq = seq // Q_TILE
    nc = seq // CMP_STRIDE
    nkt = seq // K_TILE
    cols_all = NSA_HEADS * Q_TILE
    vrows = KV_WIDTH + BF16_SUBLANES
    kvspec = lambda c: pl.BlockSpec((seq, KV_WIDTH), lambda b, n: (b, c))
    return pl.pallas_call(
        _nsa_kernel,
        grid=(batch, nq),
        in_specs=[
            pl.BlockSpec((Q_TILE, NSA_WIDTH), lambda b, n: (b * nq + n, 0)),
            pl.BlockSpec((None, nc, KV_WIDTH), lambda b, n: (b, 0, 0)),
            pl.BlockSpec((None, KV_WIDTH, nc), lambda b, n: (b, 0, 0)),
            kvspec(0), kvspec(1), kvspec(2), kvspec(3),
            pl.BlockSpec((Q_TILE, LANES), lambda b, n: (b * nq + n, 0)),
            pl.BlockSpec((NSA_HEADS, nc, Q_TILE), lambda b, n: (0, 0, n)),
            pl.BlockSpec(tb.shape, lambda b, n: (0, 0, 0, 0)),
            pl.BlockSpec(ovt.shape, lambda b, n: (0, 0)),
        ],
        out_specs=pl.BlockSpec((Q_TILE, NSA_WIDTH), lambda b, n: (b * nq + n, 0)),
        out_shape=jax.ShapeDtypeStruct((batch * seq, NSA_WIDTH), F32),
        scratch_shapes=[
            pltpu.VMEM((KV_WIDTH, cols_all), BF16),
            pltpu.VMEM((nkt, vrows, K_TILE), BF16),
            pltpu.VMEM((nkt, vrows, K_TILE), BF16),
            pltpu.VMEM((NSA_KV_GROUPS, seq // SEL_BLOCK, Q_TILE), F32),
            pltpu.VMEM((1, cols_all), F32),
            pltpu.VMEM((vrows, cols_all), F32),
            pltpu.VMEM((1, cols_all), F32),
            pltpu.VMEM((vrows, cols_all), F32),
        ],
        compiler_params=_cparams("arbitrary", "arbitrary"),
        name="nsa_attention",
    )(q, kcmp, vcmpt, kv, kv, kv, kv, misc, biasc, tb, ovt)


def _softplus(x):
    return jnp.maximum(x, 0.0) + jnp.log1p(jnp.exp(-jnp.abs(x)))


def _ssd_kernel(xbc_ref, z_ref, misc_ref, convw_ref, convb_ref, dtb_ref, alog_ref, dskip_ref, gain_ref,
                tri_ref, e1_ref, o_ref, prev_ref, h_ref):
    c = pl.program_id(1)
    L = SSD_CHUNK
    gw = SSM_INNER // SSM_GROUPS
    hpg = SSM_HEADS // SSM_GROUPS

    @pl.when(c == 0)
    def _():
        prev_ref[...] = jnp.zeros_like(prev_ref)
        h_ref[...] = jnp.zeros_like(h_ref)

    x = xbc_ref[...]
    ng = L // CONV_TAIL
    x3 = x.reshape(ng, CONV_TAIL, CONV_CH)
    sub = lax.broadcasted_iota(jnp.int32, (1, CONV_TAIL, 1), 1)
    acc3 = convb_ref[...] + x3 * convw_ref[CONV_WIDTH - 1:CONV_WIDTH, :]
    for k in range(1, CONV_WIDTH):
        rot = pltpu.roll(x3, k, 1)
        before = jnp.concatenate([pltpu.roll(prev_ref[...], k, 0)[None], rot[:ng - 1]], axis=0)
        acc3 = acc3 + jnp.where(sub >= k, rot, before) * convw_ref[CONV_WIDTH - 1 - k:CONV_WIDTH - k, :]
    prev_ref[...] = x[L - CONV_TAIL:L]
    xa = _silu(acc3).reshape(L, CONV_CH)
    xs = xa[:, :SSM_INNER]
    bm = xa[:, SSM_INNER:SSM_INNER + SSM_GROUPS * SSM_STATE]
    cm = xa[:, SSM_INNER + SSM_GROUPS * SSM_STATE:]

    dt = _softplus(misc_ref[...] + dtb_ref[...])
    da = dt * (-jnp.exp(alog_ref[...]))
    cs = _dot_f32_rhs(tri_ref[...], da) * LOG2E
    cs_t = cs.T
    dt_t = dt.T
    ecs = _dot_f32x2_lhs(jnp.exp2(cs), e1_ref[...])
    to_end = dt * jnp.exp2(cs[L - 1:L, :] - cs)
    xw_b = (xs * _dot_f32x2_lhs(to_end, e1_ref[...])).astype(BF16)
    xs_b = xs.astype(BF16)
    state_decay = ecs[L - 1:L, :]

    li = lax.broadcasted_iota(jnp.int32, (L, L), 0)
    si = lax.broadcasted_iota(jnp.int32, (L, L), 1)
    causal = li >= si
    low = si < SSM_HEAD_DIM

    ys = []
    for g in range(SSM_GROUPS):
        bg = bm[:, g * SSM_STATE:(g + 1) * SSM_STATE]
        cg = cm[:, g * SSM_STATE:(g + 1) * SSM_STATE].astype(BF16)
        cb = _dot_nt(cg, bg.astype(BF16))
        h_g = h_ref[:, g * gw:(g + 1) * gw]
        y_off = _dot(cg, h_g.astype(BF16)) * ecs[:, g * gw:(g + 1) * gw]
        for pr in range(hpg // 2):
            h0 = g * hpg + 2 * pr
            gs = []
            for hh in (h0, h0 + 1):
                ln = DT_LANE0 + hh
                col = jnp.broadcast_to(cs[:, ln:ln + 1], (L, L))
                dec = jnp.exp2(jnp.where(causal, col - cs_t[ln:ln + 1, :], NEG_INF))
                gs.append((cb * dec * dt_t[ln:ln + 1, :]).astype(BF16))
            ch = slice(h0 * SSM_HEAD_DIM, (h0 + 2) * SSM_HEAD_DIM)
            xpair = xs_b[:, ch]
            zero = jnp.zeros_like(xpair)
            rhs = jnp.concatenate([jnp.where(low, xpair, zero), jnp.where(low, zero, xpair)], axis=0)
            y_diag = _dot(jnp.concatenate(gs, axis=1), rhs)
            off = slice(2 * pr * SSM_HEAD_DIM, (2 * pr + 2) * SSM_HEAD_DIM)
            ys.append(y_diag + y_off[:, off] + xs[:, ch] * dskip_ref[:, ch])
        st = _dot(bg.T.astype(BF16), xw_b[:, g * gw:(g + 1) * gw])
        h_ref[:, g * gw:(g + 1) * gw] = h_g * state_decay[:, g * gw:(g + 1) * gw] + st

    y = jnp.concatenate(ys, axis=1) * _silu(z_ref[...])
    outs = []
    for g in range(SSM_GROUPS):
        outs.append(_rms(y[:, g * gw:(g + 1) * gw], gain_ref[:, g * gw:(g + 1) * gw]))
    o_ref[...] = jnp.concatenate(outs, axis=1).astype(BF16)


def _ssd_consts():
    lane = np.arange(LANES)
    tri = (np.arange(SSD_CHUNK)[:, None] >= np.arange(SSD_CHUNK)[None, :])
    head1 = np.arange(SSM_INNER) // SSM_HEAD_DIM
    e1 = (lane[:, None] - DT_LANE0) == head1[None, :]
    return jnp.asarray(tri, BF16), jnp.asarray(e1, BF16)


def _ssd(xbc, z, misc, prm, consts, layer, batch, seq):
    nch = seq // SSD_CHUNK
    tok = lambda n: pl.BlockSpec((SSD_CHUNK, n), lambda b, c: (b * nch + c, 0))
    lay = lambda a: pl.BlockSpec((None,) + a.shape[1:], lambda b, c: (layer,) + (0,) * (a.ndim - 1))
    full = lambda a: pl.BlockSpec(a.shape, lambda b, c: (0,) * a.ndim)
    params = (prm["conv_w"], prm["conv_b"], prm["dt_bias"], prm["a_log"], prm["d_skip"], prm["ssm_gain"])
    return pl.pallas_call(
        _ssd_kernel,
        grid=(batch, nch),
        in_specs=[tok(CONV_CH), tok(SSM_INNER), tok(LANES)] + [lay(a) for a in params] + [full(a) for a in consts],
        out_specs=tok(SSM_INNER),
        out_shape=jax.ShapeDtypeStruct((batch * seq, SSM_INNER), BF16),
        scratch_shapes=[pltpu.VMEM((CONV_TAIL, CONV_CH), F32), pltpu.VMEM((SSM_STATE, SSM_INNER), F32)],
        compiler_params=_cparams("parallel", "arbitrary"),
        name="ssd",
    )(xbc, z, misc, *params, *consts)


def _prep_ssd_params(conv_w, conv_b, dt_bias, a_log, d_skip, ssm_out_norm):
    nl = conv_w.shape[0]

    def dt_lanes(v):
        out = jnp.zeros((nl, 1, LANES), F32)
        return out.at[:, 0, DT_LANE0:DT_LANE0 + SSM_HEADS].set(v)

    return {
        "conv_w": conv_w,
        "conv_b": conv_b[:, None, :],
        "dt_bias": dt_lanes(dt_bias),
        "a_log": dt_lanes(a_log),
        "d_skip": jnp.repeat(d_skip, SSM_HEAD_DIM, axis=-1)[:, None, :],
        "ssm_gain": ssm_out_norm[:, None, :],
    }


def _outproj_kernel(x_ref, oa_ref, os_ref, g_ref, wa_ref, ws_ref, o_ref):
    an = _rms(oa_ref[...], g_ref[...]).astype(BF16)
    o_ref[...] = x_ref[...] + _dot(an, wa_ref[...]) + _dot(os_ref[...], ws_ref[...])


def _outproj(x, o_attn, o_ssm, gain, wa, ws, layer):
    t, d = x.shape
    tm = RESID_TM
    return pl.pallas_call(
        _outproj_kernel,
        grid=(t // tm,),
        in_specs=[
            pl.BlockSpec((tm, d), lambda i: (i, 0)),
            pl.BlockSpec((tm, NSA_WIDTH), lambda i: (i, 0)),
            pl.BlockSpec((tm, SSM_INNER), lambda i: (i, 0)),
            pl.BlockSpec((None, 1, NSA_WIDTH), lambda i: (layer, 0, 0)),
            pl.BlockSpec((None, NSA_WIDTH, d), lambda i: (layer, 0, 0)),
            pl.BlockSpec((None, SSM_INNER, d), lambda i: (layer, 0, 0)),
        ],
        out_specs=pl.BlockSpec((tm, d), lambda i: (i, 0)),
        out_shape=jax.ShapeDtypeStruct((t, d), F32),
        compiler_params=_cparams("parallel"),
        name="outproj",
    )(x, o_attn, o_ssm, gain, wa, ws)


def _perm_heads(a, axis):
    idx = np.concatenate([np.arange(h * HEAD_DIM, (h + 1) * HEAD_DIM) for h in HEAD_ORDER])
    return jnp.take(a, jnp.asarray(idx), axis=axis)


def _ple_kernel(x_ref, p_ref, g_ref, wg_ref, wp_ref, fg_ref, o_ref, *, final):
    x = x_ref[...]
    xn = _rms(x, g_ref[...]).astype(BF16)
    gate = jax.nn.sigmoid(_dot(xn, wg_ref[...]))
    y = x + gate * _dot(p_ref[...].astype(BF16), wp_ref[...])
    if final:
        y = _rms(y, fg_ref[...])
    o_ref[...] = y


def _ple(x, p, gain, wg, wp, final_gain, layer, final):
    t, d = x.shape
    tm = RESID_TM
    return pl.pallas_call(
        functools.partial(_ple_kernel, final=final),
        grid=(t // tm,),
        in_specs=[
            pl.BlockSpec((tm, d), lambda i: (i, 0)),
            pl.BlockSpec((None, tm, PLE_DIM), lambda i: (layer, i, 0)),
            pl.BlockSpec((None, 1, d), lambda i: (layer, 0, 0)),
            pl.BlockSpec((None, d, d), lambda i: (layer, 0, 0)),
            pl.BlockSpec((None, PLE_DIM, d), lambda i: (layer, 0, 0)),
            pl.BlockSpec((1, d), lambda i: (0, 0)),
        ],
        out_specs=pl.BlockSpec((tm, d), lambda i: (i, 0)),
        out_shape=jax.ShapeDtypeStruct((t, d), F32),
        compiler_params=_cparams("parallel"),
        name="ple",
    )(x, p, gain, wg, wp, final_gain)


def kernel(x, p, ffn1_norm, ffn1_w_in, ffn1_w_out, mix_norm, w_mix_in, cmp_pos, cmp_w1, cmp_b1, cmp_w2,
           cmp_b2, rel_table, nsa_out_norm, conv_w, conv_b, dt_bias, a_log, d_skip, ssm_out_norm, w_mix_out,
           ffn2_norm, ffn2_w_in, ffn2_w_out, ple_norm, ple_gate_w, ple_proj_w, final_norm):
    batch, seq, d = x.shape
    depth = p.shape[0]
    t = batch * seq
    bf = lambda a: a.astype(BF16)
    row = lambda a: a[:, None, :]

    ffn1_in, ffn1_out, ffn2_in, ffn2_out = bf(ffn1_w_in), bf(ffn1_w_out), bf(ffn2_w_in), bf(ffn2_w_out)
    w_proj = _prep_inproj_weight(w_mix_in)
    wo_attn = bf(_perm_heads(w_mix_out[:, :NSA_WIDTH], axis=1))
    wo_ssm = bf(w_mix_out[:, NSA_WIDTH:])
    nsa_gain = row(_perm_heads(nsa_out_norm, axis=1))
    ssd_prm = _prep_ssd_params(conv_w, conv_b, dt_bias, a_log, d_skip, ssm_out_norm)
    ssd_consts = _ssd_consts()
    tb, biasc = _nsa_tables(rel_table, seq)
    ovt = _nsa_consts(seq)
    cprep = _prep_compress(cmp_pos, cmp_w1, cmp_b1, cmp_w2, cmp_b2)
    wg, wp = bf(ple_gate_w), bf(ple_proj_w)
    p2 = p.reshape(depth, t, PLE_DIM)
    fgain = final_norm[None, :]

    h = x.reshape(t, d)
    for i in range(depth):
        h = _ffn(h, row(ffn1_norm), ffn1_in, ffn1_out, i)
        q, kc, vc, kv, misc, z, xbc = _inproj(h, row(mix_norm), w_proj, i)
        kcmp, vcmp = _compress(kc, vc, cprep, i, batch, seq)
        o_attn = _nsa_attention(q, kcmp, vcmp, kv, misc, tb, biasc, ovt, batch, seq)
        o_ssm = _ssd(xbc, z, misc, ssd_prm, ssd_consts, i, batch, seq)
        h = _outproj(h, o_attn, o_ssm, nsa_gain, wo_attn, wo_ssm, i)
        h = _ffn(h, row(ffn2_norm), ffn2_in, ffn2_out, i)
        h = _ple(h, p2, row(ple_norm), wg, wp, fgain, i, final=(i == depth - 1))
    return h.reshape(batch, seq, d)
```

```python
import functools
import math

import numpy as np
import jax
import jax.numpy as jnp
from jax import lax
from jax.experimental import pallas as pl
from jax.experimental.pallas import tpu as pltpu

F32 = jnp.float32
BF16 = jnp.bfloat16

D_MODEL = 1024
DEPTH = 4
PLE_DIM = 256
D_FF = 2816
EPS = 1e-6
NEG_INF = -1e30
FORCE_SCORE = 1e4

NSA_HEADS = 8
NSA_KV_GROUPS = 2
NSA_REP = NSA_HEADS // NSA_KV_GROUPS
HEAD_DIM = 64
NSA_WIDTH = NSA_HEADS * HEAD_DIM
KV_WIDTH = NSA_KV_GROUPS * HEAD_DIM
CMP_BLOCK = 32
CMP_STRIDE = 16
CMP_HIDDEN = 256
SEL_BLOCK = 64
SEL_TOPK = 8
WINDOW = 512
REL_BUCKETS = 32
REL_MAX_DIST = 128

SSM_HEADS = 16
SSM_HEAD_DIM = 64
SSM_INNER = SSM_HEADS * SSM_HEAD_DIM
SSM_GROUPS = 2
SSM_STATE = 128
CONV_WIDTH = 4
SSD_CHUNK = 128
CONV_CH = SSM_INNER + 2 * SSM_GROUPS * SSM_STATE

LANES = 128
VMEM_LIMIT_BYTES = 48 * 1024 * 1024

FFN_TM = 1024
FFN_TF = 256
PROJ_TM = 512
RESID_TM = 1024
Q_TILE = 128
K_TILE = 256
HEAD_ORDER = (0, 4, 1, 5, 2, 6, 3, 7)
GATE_LANE0 = 0
DT_LANE0 = 3 * NSA_HEADS
N_BIAS_TABLES = 6
FAR_TABLE = 3
LOG2E = math.log2(math.e)
BF16_SUBLANES = 16
CONV_TAIL = 8
SSD_STEP_CHUNKS = 4


def _dot(a, b):
    return jnp.dot(a, b, preferred_element_type=F32)


def _dot_nt(a, b):
    return lax.dot_general(a, b, (((1,), (1,)), ((), ())), preferred_element_type=F32)


def _split3(v):
    hi = v.astype(BF16)
    r = v - hi.astype(F32)
    mid = r.astype(BF16)
    lo = (r - mid.astype(F32)).astype(BF16)
    return hi, mid, lo


def _dot_f32x2_lhs(v, e):
    hi = v.astype(BF16)
    lo = (v - hi.astype(F32)).astype(BF16)
    return _dot(hi, e) + _dot(lo, e)


def _dot_f32_rhs(e, v):
    hi, mid, lo = _split3(v)
    return _dot(e, hi) + _dot(e, mid) + _dot(e, lo)


def _rms(x, g):
    ms = jnp.mean(x * x, axis=-1, keepdims=True)
    return x * lax.rsqrt(ms + EPS) * g


def _silu(x):
    return x * jax.nn.sigmoid(x)


def _cparams(*sem):
    return pltpu.CompilerParams(dimension_semantics=sem, vmem_limit_bytes=VMEM_LIMIT_BYTES)


def _ffn_kernel(x_ref, g_ref, wg_ref, wu_ref, wo_ref, o_ref, xn_ref, acc_ref):
    j = pl.program_id(1)

    @pl.when(j == 0)
    def _():
        xn_ref[...] = _rms(x_ref[...], g_ref[...]).astype(BF16)
        acc_ref[...] = jnp.zeros_like(acc_ref)

    xn = xn_ref[...]
    gate = _dot(xn, wg_ref[...])
    up = _dot(xn, wu_ref[...])
    h = (_silu(gate) * up).astype(BF16)
    acc_ref[...] += _dot(h, wo_ref[...])

    @pl.when(j == pl.num_programs(1) - 1)
    def _():
        o_ref[...] = x_ref[...] + 0.5 * acc_ref[...]


def _ffn(x, gain, w_in, w_out, layer):
    t, d = x.shape
    nf = D_FF // FFN_TF
    return pl.pallas_call(
        _ffn_kernel,
        grid=(t // FFN_TM, nf),
        in_specs=[
            pl.BlockSpec((FFN_TM, d), lambda i, j: (i, 0)),
            pl.BlockSpec((None, 1, d), lambda i, j: (layer, 0, 0)),
            pl.BlockSpec((None, d, FFN_TF), lambda i, j: (layer, 0, j)),
            pl.BlockSpec((None, d, FFN_TF), lambda i, j: (layer, 0, j + nf)),
            pl.BlockSpec((None, FFN_TF, d), lambda i, j: (layer, j, 0)),
        ],
        out_specs=pl.BlockSpec((FFN_TM, d), lambda i, j: (i, 0)),
        out_shape=jax.ShapeDtypeStruct((t, d), F32),
        scratch_shapes=[pltpu.VMEM((FFN_TM, d), BF16), pltpu.VMEM((FFN_TM, d), F32)],
        compiler_params=_cparams("parallel", "arbitrary"),
        name="ffn",
    )(x, gain, w_in, w_in, w_out)


_C_Q = (0, NSA_WIDTH)
_C_KC = (_C_Q[1], _C_Q[1] + KV_WIDTH)
_C_VC = (_C_KC[1], _C_KC[1] + KV_WIDTH)
_C_KV = (_C_VC[1], _C_VC[1] + 4 * KV_WIDTH)
_C_MISC = (_C_KV[1], _C_KV[1] + LANES)
_C_Z = (_C_MISC[1], _C_MISC[1] + SSM_INNER)
_C_XBC = (_C_Z[1], _C_Z[1] + CONV_CH)
PROJ_COLS = _C_XBC[1]


def _inproj_kernel(x_ref, g_ref, w_ref, q_ref, kc_ref, vc_ref, kv_ref, misc_ref, z_ref, xbc_ref, rows_ref):
    xn = _rms(x_ref[...], g_ref[...]).astype(BF16)
    q_ref[...] = _dot(xn, w_ref[:, _C_Q[0]:_C_Q[1]])
    nrow = rows_ref.shape[0] // CMP_STRIDE
    for cols, dst in ((_C_KC, kc_ref), (_C_VC, vc_ref)):
        rows_ref[...] = _dot(xn, w_ref[:, cols[0]:cols[1]])
        for l in range(CMP_STRIDE):
            dst[:, l * KV_WIDTH:(l + 1) * KV_WIDTH] = rows_ref[pl.ds(l, nrow, stride=CMP_STRIDE), :]
    kv_ref[...] = _dot(xn, w_ref[:, _C_KV[0]:_C_KV[1]]).astype(BF16)
    misc_ref[...] = _dot(xn, w_ref[:, _C_MISC[0]:_C_MISC[1]])
    z_ref[...] = _dot(xn, w_ref[:, _C_Z[0]:_C_Z[1]])
    xbc_ref[...] = _dot(xn, w_ref[:, _C_XBC[0]:_C_XBC[1]])


def _inproj(x, gain, w, layer):
    t, d = x.shape
    tm = PROJ_TM
    grouped = (tm // CMP_STRIDE, t // CMP_STRIDE, CMP_STRIDE * KV_WIDTH, F32)
    outs = ((tm, t, NSA_WIDTH, F32), grouped, grouped, (tm, t, 4 * KV_WIDTH, BF16), (tm, t, LANES, F32),
            (tm, t, SSM_INNER, F32), (tm, t, CONV_CH, F32))
    return pl.pallas_call(
        _inproj_kernel,
        grid=(t // tm,),
        in_specs=[
            pl.BlockSpec((tm, d), lambda i: (i, 0)),
            pl.BlockSpec((None, 1, d), lambda i: (layer, 0, 0)),
            pl.BlockSpec((None, d, PROJ_COLS), lambda i: (layer, 0, 0)),
        ],
        out_specs=[pl.BlockSpec((rows, n), lambda i: (i, 0)) for rows, _, n, _ in outs],
        out_shape=[jax.ShapeDtypeStruct((total, n), dt) for _, total, n, dt in outs],
        scratch_shapes=[pltpu.VMEM((tm, KV_WIDTH), F32)],
        compiler_params=_cparams("parallel"),
        name="inproj",
    )(x, gain, w)


_IN_OFFS = tuple(int(v) for v in np.cumsum(
    (0, NSA_WIDTH) + (KV_WIDTH,) * 6 + (3 * NSA_HEADS, SSM_INNER, CONV_CH, SSM_HEADS)))
PREP_ROWS = 256


def _prep_inproj_kernel(w_ref, o_ref):
    o = _IN_OFFS
    o_ref[:, _C_KC[0]:_C_KV[1]] = w_ref[:, o[1]:o[7]].astype(BF16)
    o_ref[:, _C_Z[0]:_C_XBC[1]] = w_ref[:, o[8]:o[10]].astype(BF16)
    for pos, h in enumerate(HEAD_ORDER):
        o_ref[:, pos * HEAD_DIM:(pos + 1) * HEAD_DIM] = w_ref[:, h * HEAD_DIM:(h + 1) * HEAD_DIM].astype(BF16)
    ngate, ndt = o[8] - o[7], o[11] - o[10]
    m0 = _C_MISC[0]
    o_ref[:, m0:m0 + ngate] = w_ref[:, o[7]:o[8]].astype(BF16)
    o_ref[:, m0 + ngate:m0 + ngate + ndt] = w_ref[:, o[10]:o[11]].astype(BF16)
    o_ref[:, m0 + ngate + ndt:_C_MISC[1]] = jnp.zeros((o_ref.shape[0], LANES - ngate - ndt), BF16)


def _prep_inproj_weight(w_mix_in):
    nl, d, win = w_mix_in.shape
    return pl.pallas_call(
        _prep_inproj_kernel,
        grid=(nl, d // PREP_ROWS),
        in_specs=[pl.BlockSpec((None, PREP_ROWS, win), lambda l, r: (l, r, 0))],
        out_specs=pl.BlockSpec((None, PREP_ROWS, PROJ_COLS), lambda l, r: (l, r, 0)),
        out_shape=jax.ShapeDtypeStruct((nl, d, PROJ_COLS), BF16),
        compiler_params=_cparams("parallel", "parallel"),
        name="inproj_weight_layout",
    )(w_mix_in)


def _compress_kernel(kc_ref, vc_ref, pos_ref, w1a_ref, w1b_ref, b1_ref, w2_ref, b2_ref,
                     kcmp_ref, vcmpt_ref):
    nrow = kc_ref.shape[0]
    for which, src in enumerate((kc_ref, vc_ref)):
        r = src[...]
        ra = (r + pos_ref[which, 0]).astype(BF16)
        rb = (r + pos_ref[which, 1]).astype(BF16)
        ha = _dot(ra, w1a_ref[which])
        hb = _dot(rb, w1b_ref[which])
        h = ha + pltpu.roll(hb, nrow - 1, 0) + b1_ref[which]
        out = _dot(_silu(h).astype(BF16), w2_ref[which]) + b2_ref[which]
        if which == 0:
            kcmp_ref[...] = out.astype(BF16)
        else:
            vcmpt_ref[...] = out.T.astype(BF16)


def _compress(kc, vc, prep, layer, batch, seq):
    nrow = seq // CMP_STRIDE
    wide = CMP_STRIDE * KV_WIDTH
    kcr = kc.reshape(batch, nrow, wide)
    vcr = vc.reshape(batch, nrow, wide)
    lay = lambda a: pl.BlockSpec((None,) + a.shape[1:], lambda b: (layer,) + (0,) * (a.ndim - 1))
    consts = (prep["pos"], prep["w1a"], prep["w1b"], prep["b1"], prep["w2"], prep["b2"])
    return pl.pallas_call(
        _compress_kernel,
        grid=(batch,),
        in_specs=[pl.BlockSpec((None, nrow, wide), lambda b: (b, 0, 0))] * 2 + [lay(a) for a in consts],
        out_specs=[pl.BlockSpec((None, nrow, KV_WIDTH), lambda b: (b, 0, 0)),
                   pl.BlockSpec((None, KV_WIDTH, nrow), lambda b: (b, 0, 0))],
        out_shape=[jax.ShapeDtypeStruct((batch, nrow, KV_WIDTH), BF16),
                   jax.ShapeDtypeStruct((batch, KV_WIDTH, nrow), BF16)],
        compiler_params=_cparams("parallel"),
        name="nsa_compress",
    )(kcr, vcr, *consts)


def _prep_compress(cmp_pos, cmp_w1, cmp_b1, cmp_w2, cmp_b2):
    assert NSA_KV_GROUPS == 2
    nl = cmp_w1.shape[0]
    half = CMP_BLOCK // 2
    w1 = cmp_w1.astype(BF16).reshape(nl, 2, CMP_BLOCK, HEAD_DIM, CMP_HIDDEN)

    def block_diag(w, axis):
        z = jnp.zeros_like(w)
        return jnp.stack([jnp.concatenate([w, z], axis=-1), jnp.concatenate([z, w], axis=-1)], axis=axis)

    def expand(w):
        return block_diag(w, 3).reshape(nl, 2, half * 2 * HEAD_DIM, 2 * CMP_HIDDEN)

    def pos_rows(p):
        return jnp.broadcast_to(p[:, :, :, None, :], (nl, 2, half, 2, HEAD_DIM)).reshape(nl, 2, 1, half * 2 * HEAD_DIM)

    w2 = block_diag(cmp_w2.astype(BF16), 2).reshape(nl, 2, 2 * CMP_HIDDEN, 2 * HEAD_DIM)
    return {
        "pos": jnp.stack([pos_rows(cmp_pos[:, :, :half]), pos_rows(cmp_pos[:, :, half:])], axis=2),
        "w1a": expand(w1[:, :, :half]),
        "w1b": expand(w1[:, :, half:]),
        "b1": jnp.tile(cmp_b1, (1, 1, 2))[:, :, None, :],
        "w2": w2,
        "b2": jnp.tile(cmp_b2, (1, 1, 2))[:, :, None, :],
    }


def _t5_bucket_np(dist):
    n = np.maximum(dist, 0)
    exact = REL_BUCKETS // 2
    nf = np.maximum(n, exact).astype(np.float64)
    large = exact + (np.log(nf / exact) / math.log(REL_MAX_DIST / exact) * (REL_BUCKETS - exact)).astype(np.int64)
    return np.where(n < exact, n, np.minimum(large, REL_BUCKETS - 1)).astype(np.int32)


MASKED_BUCKET = REL_BUCKETS


def _bucket_maps(seq):
    j = np.arange(K_TILE)[:, None]
    i = np.arange(Q_TILE)[None, :]
    tiles = []
    for delta in range(N_BIAS_TABLES):
        d = Q_TILE * delta + i - j
        valid = (d >= 0) & ((d < WINDOW) if delta >= 4 else True)
        tiles.append(np.where(valid, _t5_bucket_np(d), MASKED_BUCKET))
    c = np.arange(seq // CMP_STRIDE)[:, None]
    t = np.arange(seq)[None, :]
    dc = t - (c * CMP_STRIDE + CMP_BLOCK - 1)
    cmp_map = np.where(dc >= 0, _t5_bucket_np(dc), MASKED_BUCKET)
    return np.stack(tiles).astype(np.int32), cmp_map.astype(np.int32)


def _tables_kernel(tab_ref, bkt_ref, bkc_ref, tb_ref, bc_ref):
    h = pl.program_id(0)
    for src, dst in ((bkt_ref, tb_ref), (bkc_ref, bc_ref)):
        bk = src[...]
        out = jnp.zeros(bk.shape, F32)
        for b in range(REL_BUCKETS + 1):
            out = jnp.where(bk == b, tab_ref[b, h], out)
        dst[...] = (out * LOG2E).astype(dst.dtype)


def _nsa_tables(rel_table, seq):
    bkt, bkc = _bucket_maps(seq)
    tab = jnp.concatenate([rel_table[:, np.asarray(HEAD_ORDER)],
                           jnp.full((1, NSA_HEADS), NEG_INF, F32)], axis=0)
    nc = seq // CMP_STRIDE
    return pl.pallas_call(
        _tables_kernel,
        grid=(NSA_HEADS,),
        in_specs=[
            pl.BlockSpec(memory_space=pltpu.SMEM),
            pl.BlockSpec(bkt.shape, lambda h: (0, 0, 0)),
            pl.BlockSpec(bkc.shape, lambda h: (0, 0)),
        ],
        out_specs=[pl.BlockSpec((N_BIAS_TABLES, None, K_TILE, Q_TILE), lambda h: (0, h, 0, 0)),
                   pl.BlockSpec((None, nc, seq), lambda h: (h, 0, 0))],
        out_shape=[jax.ShapeDtypeStruct((N_BIAS_TABLES, NSA_HEADS, K_TILE, Q_TILE), BF16),
                   jax.ShapeDtypeStruct((NSA_HEADS, nc, seq), F32)],
        compiler_params=_cparams("parallel"),
        name="nsa_bias_tables",
    )(tab, jnp.asarray(bkt), jnp.asarray(bkc))


def _nsa_consts(seq):
    nc = seq // CMP_STRIDE
    nb = seq // SEL_BLOCK
    c = np.arange(nc)
    blk = np.arange(nb)
    c_lo, c_hi = c * CMP_STRIDE, c * CMP_STRIDE + CMP_BLOCK - 1
    s_lo, s_hi = blk * SEL_BLOCK, blk * SEL_BLOCK + SEL_BLOCK - 1
    ovt = (c_lo[None, :] <= s_hi[:, None]) & (c_hi[None, :] >= s_lo[:, None])
    ovt[:, nc - 1] = False
    return jnp.asarray(ovt, BF16)


def _nsa_kernel(q_ref, kcmp_ref, vcmpt_ref, ks_ref, vs_ref, kw_ref, vw_ref, misc_ref, biasc_ref, tb_ref,
                ovt_ref, o_ref, qpt_ref, vst_ref, vwt_ref, madd_ref, m_ref, acc_ref, mw_ref, accw_ref):
    n = pl.program_id(1)
    nkt = ks_ref.shape[0] // K_TILE
    nsel_blocks = ovt_ref.shape[0]
    nhb = NSA_HEADS
    cols = lambda hb: slice(hb * Q_TILE, (hb + 1) * Q_TILE)
    frow = lax.broadcasted_iota(jnp.int32, (KV_WIDTH, Q_TILE), 0)
    low = frow < HEAD_DIM

    @pl.when(n == 0)
    def _():
        arow = lax.broadcasted_iota(jnp.int32, (BF16_SUBLANES, K_TILE), 0)
        ones_row = jnp.where(arow == 0, 1.0, 0.0).astype(BF16)
        for t2 in range(nkt):
            rows = slice(t2 * K_TILE, (t2 + 1) * K_TILE)
            for src, dst in ((vs_ref, vst_ref), (vw_ref, vwt_ref)):
                dst[t2, 0:KV_WIDTH, :] = src[rows, :].astype(F32).T.astype(BF16)
                dst[t2, KV_WIDTH:, :] = ones_row

    scale = HEAD_DIM ** -0.5 * LOG2E
    for j in range(nhb // 2):
        slab = (q_ref[:, j * LANES:(j + 1) * LANES].T * scale).astype(BF16)
        zero = jnp.zeros_like(slab)
        qpt_ref[:, cols(2 * j)] = jnp.where(low, slab, zero)
        qpt_ref[:, cols(2 * j + 1)] = jnp.where(low, zero, slab)
    qpt = qpt_ref[...]

    pair_cols = lambda pair: slice(2 * pair * Q_TILE, (2 * pair + 2) * Q_TILE)

    def branch(k_ref, vt_ref, mx_ref, ac_ref, selected):
        def scores(t2):
            start = pl.multiple_of(t2 * K_TILE, K_TILE)
            k = k_ref[pl.ds(start, K_TILE), :]
            return [_dot(k, qpt_ref[:, pair_cols(pair)]) for pair in range(nhb // 2)]

        def softmax_pv(t2, s_pairs):
            vt = vt_ref[t2]
            delta = n - 2 * t2
            ti = jnp.minimum(delta, FAR_TABLE) if selected else delta
            blk0 = t2 * (K_TILE // SEL_BLOCK)
            for pair in range(nhb // 2):
                c2 = pair_cols(pair)
                s2 = s_pairs[pair]
                ps, alphas = [], []
                for half in range(2):
                    hb = 2 * pair + half
                    s = s2[:, half * Q_TILE:(half + 1) * Q_TILE].astype(BF16) + tb_ref[ti, hb]
                    if selected:
                        s = jnp.concatenate(
                            [s[j * SEL_BLOCK:(j + 1) * SEL_BLOCK]
                             + madd_ref[half, pl.ds(blk0 + j, 1), :].astype(BF16)
                             for j in range(K_TILE // SEL_BLOCK)], axis=0)
                    m_prev = mx_ref[:, cols(hb)]
                    m_new = jnp.maximum(m_prev, jnp.max(s, axis=0, keepdims=True).astype(F32))
                    mx_ref[:, cols(hb)] = m_new
                    alphas.append(jnp.exp2(m_prev - m_new))
                    ps.append(jnp.exp2(s - m_new.astype(BF16)))
                pv = _dot(vt, jnp.concatenate(ps, axis=1))
                ac_ref[:, c2] = ac_ref[:, c2] * jnp.concatenate(alphas, axis=1) + pv

        def issue(first, k):
            return [scores(first + i) for i in range(k)]

        def finish(first, ss):
            for i, s_pairs in enumerate(ss):
                softmax_pv(first + i, s_pairs)

        def init():
            mx_ref[...] = jnp.full_like(mx_ref, -jnp.inf)
            ac_ref[...] = jnp.zeros_like(ac_ref)

        def result():
            acc = ac_ref[...]
            return acc[0:KV_WIDTH] / acc[KV_WIDTH:KV_WIDTH + 1]

        return init, issue, finish, result

    sel_init, sel_issue, sel_finish, sel_result = branch(ks_ref, vst_ref, m_ref, acc_ref, True)
    win_init, win_issue, win_finish, win_result = branch(kw_ref, vwt_ref, mw_ref, accw_ref, False)

    tq = n * Q_TILE + lax.broadcasted_iota(jnp.int32, (1, Q_TILE), 1)
    t2_diag = n // 2

    def q_tile(sel_k, win_k):
        win_first = t2_diag + 1 - win_k
        win_init()
        sel_init()
        has_block = (tq >= CMP_BLOCK - 1).astype(F32)
        sc = _dot(kcmp_ref[...], qpt)
        win_scores = win_issue(win_first, win_k)
        psum = [None, None]
        pcs = []
        for hb in range(nhb):
            s = sc[:, cols(hb)] + biasc_ref[hb]
            m = jnp.max(s, axis=0, keepdims=True)
            e = jnp.exp2(s - m)
            p = e / jnp.sum(e, axis=0, keepdims=True) * has_block
            pcs.append(p.astype(BF16))
            g = hb % 2
            psum[g] = p if psum[g] is None else psum[g] + p
        o_cmp = _dot(vcmpt_ref[...], jnp.concatenate(pcs, axis=1))

        blk = lax.broadcasted_iota(jnp.int32, (nsel_blocks, Q_TILE), 0)
        blk_f = blk.astype(F32)
        cur = tq // SEL_BLOCK
        forced = ((blk == 0) | (blk == cur) | (blk == cur - 1)).astype(F32)
        imps = [_dot_f32_rhs(ovt_ref[...], psum[g]) for g in range(NSA_KV_GROUPS)]
        sel_scores = sel_issue(0, sel_k)
        for g in range(NSA_KV_GROUPS):
            imp = imps[g]
            score = jnp.where(blk <= cur, imp + FORCE_SCORE * forced, -FORCE_SCORE)
            sel = jnp.zeros(score.shape, F32)
            for _ in range(SEL_TOPK):
                mx = jnp.max(score, axis=0, keepdims=True)
                first = jnp.min(jnp.where(score == mx, blk_f, float(nsel_blocks)), axis=0, keepdims=True)
                hit = blk_f == first
                sel = jnp.where(hit, 1.0, sel)
                score = jnp.where(hit, -jnp.inf, score)
            madd_ref[g] = (sel - 1.0) * (-NEG_INF)
        win_finish(win_first, win_scores)
        sel_finish(0, sel_scores)
        o_sel = sel_result()
        o_win = win_result()

        gates = jax.nn.sigmoid(misc_ref[...]).T

        def gate(branch, hb):
            col = GATE_LANE0 + branch * NSA_HEADS + HEAD_ORDER[hb]
            return gates[col:col + 1, :]

        for j in range(nhb // 2):
            outs = []
            for hb in (2 * j, 2 * j + 1):
                outs.append(gate(0, hb) * o_cmp[:, cols(hb)] + gate(1, hb) * o_sel[:, cols(hb)]
                            + gate(2, hb) * o_win[:, cols(hb)])
            o_ref[:, j * LANES:(j + 1) * LANES] = jnp.where(low, outs[0], outs[1]).T

    win_full = WINDOW // K_TILE + 1
    for sel_k in range(1, nkt + 1):
        @pl.when(t2_diag + 1 == sel_k)
        def _(sel_k=sel_k):
            q_tile(sel_k, min(sel_k, win_full))


def _nsa_attention(q, kcmp, vcmpt, kv, misc, tb, biasc, ovt, batch, seq):
    nq = seq // Q_TILE
    nc = seq // CMP_STRIDE
    nkt = seq // K_TILE
    cols_all = NSA_HEADS * Q_TILE
    vrows = KV_WIDTH + BF16_SUBLANES
    kvspec = lambda c: pl.BlockSpec((seq, KV_WIDTH), lambda b, n: (b, c))
    return pl.pallas_call(
        _nsa_kernel,
        grid=(batch, nq),
        in_specs=[
            pl.BlockSpec((Q_TILE, NSA_WIDTH), lambda b, n: (b * nq + n, 0)),
            pl.BlockSpec((None, nc, KV_WIDTH), lambda b, n: (b, 0, 0)),
            pl.BlockSpec((None, KV_WIDTH, nc), lambda b, n: (b, 0, 0)),
            kvspec(0), kvspec(1), kvspec(2), kvspec(3),
            pl.BlockSpec((Q_TILE, LANES), lambda b, n: (b * nq + n, 0)),
            pl.BlockSpec((NSA_HEADS, nc, Q_TILE), lambda b, n: (0, 0, n)),
            pl.BlockSpec(tb.shape, lambda b, n: (0, 0, 0, 0)),
            pl.BlockSpec(ovt.shape, lambda b, n: (0, 0)),
        ],
        out_specs=pl.BlockSpec((Q_TILE, NSA_WIDTH), lambda b, n: (b * nq + n, 0)),
        out_shape=jax.ShapeDtypeStruct((batch * seq, NSA_WIDTH), F32),
        scratch_shapes=[
            pltpu.VMEM((KV_WIDTH, cols_all), BF16),
            pltpu.VMEM((nkt, vrows, K_TILE), BF16),
            pltpu.VMEM((nkt, vrows, K_TILE), BF16),
            pltpu.VMEM((NSA_KV_GROUPS, seq // SEL_BLOCK, Q_TILE), F32),
            pltpu.VMEM((1, cols_all), F32),
            pltpu.VMEM((vrows, cols_all), F32),
            pltpu.VMEM((1, cols_all), F32),
            pltpu.VMEM((vrows, cols_all), F32),
        ],
        compiler_params=_cparams("arbitrary", "arbitrary"),
        name="nsa_attention",
    )(q, kcmp, vcmpt, kv, kv, kv, kv, misc, biasc, tb, ovt)


def _softplus(x):
    return jnp.maximum(x, 0.0) + jnp.log1p(jnp.exp(-jnp.abs(x)))


def _ssd_kernel(xbc_ref, z_ref, misc_ref, convw_ref, convb_ref, dtb_ref, alog_ref, dskip_ref, gain_ref,
                tri_ref, e1_ref, o_ref, prev_ref, h_ref):
    L = SSD_CHUNK

    @pl.when(pl.program_id(1) == 0)
    def _():
        prev_ref[...] = jnp.zeros_like(prev_ref)
        h_ref[...] = jnp.zeros_like(h_ref)

    for s in range(SSD_STEP_CHUNKS):
        rows = pl.ds(s * L, L)
        tail = prev_ref[...] if s == 0 else xbc_ref[s * L - CONV_TAIL:s * L, :]
        _ssd_chunk(xbc_ref.at[rows], z_ref.at[rows], misc_ref.at[rows], convw_ref, convb_ref, dtb_ref, alog_ref,
                   dskip_ref, gain_ref, tri_ref, e1_ref, o_ref.at[rows], tail, h_ref)
    prev_ref[...] = xbc_ref[SSD_STEP_CHUNKS * L - CONV_TAIL:SSD_STEP_CHUNKS * L, :]


def _ssd_chunk(xbc_ref, z_ref, misc_ref, convw_ref, convb_ref, dtb_ref, alog_ref, dskip_ref, gain_ref,
               tri_ref, e1_ref, o_ref, tail, h_ref):
    L = SSD_CHUNK
    gw = SSM_INNER // SSM_GROUPS
    hpg = SSM_HEADS // SSM_GROUPS

    x = xbc_ref[...]
    ng = L // CONV_TAIL
    x3 = x.reshape(ng, CONV_TAIL, CONV_CH)
    sub = lax.broadcasted_iota(jnp.int32, (1, CONV_TAIL, 1), 1)
    acc3 = convb_ref[...] + x3 * convw_ref[CONV_WIDTH - 1:CONV_WIDTH, :]
    for k in range(1, CONV_WIDTH):
        rot = pltpu.roll(x3, k, 1)
        before = jnp.concatenate([pltpu.roll(tail, k, 0)[None], rot[:ng - 1]], axis=0)
        acc3 = acc3 + jnp.where(sub >= k, rot, before) * convw_ref[CONV_WIDTH - 1 - k:CONV_WIDTH - k, :]
    xa = _silu(acc3).reshape(L, CONV_CH)
    xs = xa[:, :SSM_INNER]
    bm = xa[:, SSM_INNER:SSM_INNER + SSM_GROUPS * SSM_STATE]
    cm = xa[:, SSM_INNER + SSM_GROUPS * SSM_STATE:]

    dt = _softplus(misc_ref[...] + dtb_ref[...])
    da = dt * (-jnp.exp(alog_ref[...]))
    cs = _dot_f32_rhs(tri_ref[...], da) * LOG2E
    cs_t = cs.T
    dt_t = dt.T
    ecs = _dot_f32x2_lhs(jnp.exp2(cs), e1_ref[...])
    to_end = dt * jnp.exp2(cs[L - 1:L, :] - cs)
    xw_b = (xs * _dot_f32x2_lhs(to_end, e1_ref[...])).astype(BF16)
    xs_b = xs.astype(BF16)
    state_decay = ecs[L - 1:L, :]

    li = lax.broadcasted_iota(jnp.int32, (L, L), 0)
    si = lax.broadcasted_iota(jnp.int32, (L, L), 1)
    causal = li >= si
    low = si < SSM_HEAD_DIM

    ys = []
    for g in range(SSM_GROUPS):
        bg = bm[:, g * SSM_STATE:(g + 1) * SSM_STATE]
        cg = cm[:, g * SSM_STATE:(g + 1) * SSM_STATE].astype(BF16)
        cb = _dot_nt(cg, bg.astype(BF16))
        h_g = h_ref[:, g * gw:(g + 1) * gw]
        y_off = _dot(cg, h_g.astype(BF16)) * ecs[:, g * gw:(g + 1) * gw]
        for pr in range(hpg // 2):
            h0 = g * hpg + 2 * pr
            gs = []
            for hh in (h0, h0 + 1):
                ln = DT_LANE0 + hh
                col = jnp.broadcast_to(cs[:, ln:ln + 1], (L, L))
                dec = jnp.exp2(jnp.where(causal, col - cs_t[ln:ln + 1, :], NEG_INF))
                gs.append((cb * dec * dt_t[ln:ln + 1, :]).astype(BF16))
            ch = slice(h0 * SSM_HEAD_DIM, (h0 + 2) * SSM_HEAD_DIM)
            xpair = xs_b[:, ch]
            zero = jnp.zeros_like(xpair)
            rhs = jnp.concatenate([jnp.where(low, xpair, zero), jnp.where(low, zero, xpair)], axis=0)
            y_diag = _dot(jnp.concatenate(gs, axis=1), rhs)
            off = slice(2 * pr * SSM_HEAD_DIM, (2 * pr + 2) * SSM_HEAD_DIM)
            ys.append(y_diag + y_off[:, off] + xs[:, ch] * dskip_ref[:, ch])
        st = _dot(bg.T.astype(BF16), xw_b[:, g * gw:(g + 1) * gw])
        h_ref[:, g * gw:(g + 1) * gw] = h_g * state_decay[:, g * gw:(g + 1) * gw] + st

    y = jnp.concatenate(ys, axis=1) * _silu(z_ref[...])
    outs = []
    for g in range(SSM_GROUPS):
        outs.append(_rms(y[:, g * gw:(g + 1) * gw], gain_ref[:, g * gw:(g + 1) * gw]))
    o_ref[...] = jnp.concatenate(outs, axis=1).astype(BF16)


def _ssd_consts():
    lane = np.arange(LANES)
    tri = (np.arange(SSD_CHUNK)[:, None] >= np.arange(SSD_CHUNK)[None, :])
    head1 = np.arange(SSM_INNER) // SSM_HEAD_DIM
    e1 = (lane[:, None] - DT_LANE0) == head1[None, :]
    return jnp.asarray(tri, BF16), jnp.asarray(e1, BF16)


def _ssd(xbc, z, misc, prm, consts, layer, batch, seq):
    step_rows = SSD_STEP_CHUNKS * SSD_CHUNK
    nch = seq // step_rows
    tok = lambda n: pl.BlockSpec((step_rows, n), lambda b, c: (b * nch + c, 0))
    lay = lambda a: pl.BlockSpec((None,) + a.shape[1:], lambda b, c: (layer,) + (0,) * (a.ndim - 1))
    full = lambda a: pl.BlockSpec(a.shape, lambda b, c: (0,) * a.ndim)
    params = (prm["conv_w"], prm["conv_b"], prm["dt_bias"], prm["a_log"], prm["d_skip"], prm["ssm_gain"])
    return pl.pallas_call(
        _ssd_kernel,
        grid=(batch, nch),
        in_specs=[tok(CONV_CH), tok(SSM_INNER), tok(LANES)] + [lay(a) for a in params] + [full(a) for a in consts],
        out_specs=tok(SSM_INNER),
        out_shape=jax.ShapeDtypeStruct((batch * seq, SSM_INNER), BF16),
        scratch_shapes=[pltpu.VMEM((CONV_TAIL, CONV_CH), F32), pltpu.VMEM((SSM_STATE, SSM_INNER), F32)],
        compiler_params=_cparams("parallel", "arbitrary"),
        name="ssd",
    )(xbc, z, misc, *params, *consts)


def _prep_ssd_params(conv_w, conv_b, dt_bias, a_log, d_skip, ssm_out_norm):
    nl = conv_w.shape[0]

    def dt_lanes(v):
        out = jnp.zeros((nl, 1, LANES), F32)
        return out.at[:, 0, DT_LANE0:DT_LANE0 + SSM_HEADS].set(v)

    return {
        "conv_w": conv_w,
        "conv_b": conv_b[:, None, :],
        "dt_bias": dt_lanes(dt_bias),
        "a_log": dt_lanes(a_log),
        "d_skip": jnp.repeat(d_skip, SSM_HEAD_DIM, axis=-1)[:, None, :],
        "ssm_gain": ssm_out_norm[:, None, :],
    }


def _outproj_kernel(x_ref, oa_ref, os_ref, g_ref, wa_ref, ws_ref, o_ref):
    an = _rms(oa_ref[...], g_ref[...]).astype(BF16)
    o_ref[...] = x_ref[...] + _dot(an, wa_ref[...]) + _dot(os_ref[...], ws_ref[...])


def _outproj(x, o_attn, o_ssm, gain, wa, ws, layer):
    t, d = x.shape
    tm = RESID_TM
    return pl.pallas_call(
        _outproj_kernel,
        grid=(t // tm,),
        in_specs=[
            pl.BlockSpec((tm, d), lambda i: (i, 0)),
            pl.BlockSpec((tm, NSA_WIDTH), lambda i: (i, 0)),
            pl.BlockSpec((tm, SSM_INNER), lambda i: (i, 0)),
            pl.BlockSpec((None, 1, NSA_WIDTH), lambda i: (layer, 0, 0)),
            pl.BlockSpec((None, NSA_WIDTH, d), lambda i: (layer, 0, 0)),
            pl.BlockSpec((None, SSM_INNER, d), lambda i: (layer, 0, 0)),
        ],
        out_specs=pl.BlockSpec((tm, d), lambda i: (i, 0)),
        out_shape=jax.ShapeDtypeStruct((t, d), F32),
        compiler_params=_cparams("parallel"),
        name="outproj",
    )(x, o_attn, o_ssm, gain, wa, ws)


def _perm_heads(a, axis):
    idx = np.concatenate([np.arange(h * HEAD_DIM, (h + 1) * HEAD_DIM) for h in HEAD_ORDER])
    return jnp.take(a, jnp.asarray(idx), axis=axis)


def _ple_kernel(x_ref, p_ref, g_ref, wg_ref, wp_ref, fg_ref, o_ref, *, final):
    x = x_ref[...]
    xn = _rms(x, g_ref[...]).astype(BF16)
    gate = jax.nn.sigmoid(_dot(xn, wg_ref[...]))
    y = x + gate * _dot(p_ref[...].astype(BF16), wp_ref[...])
    if final:
        y = _rms(y, fg_ref[...])
    o_ref[...] = y


def _ple(x, p, gain, wg, wp, final_gain, layer, final):
    t, d = x.shape
    tm = RESID_TM
    return pl.pallas_call(
        functools.partial(_ple_kernel, final=final),
        grid=(t // tm,),
        in_specs=[
            pl.BlockSpec((tm, d), lambda i: (i, 0)),
            pl.BlockSpec((None, tm, PLE_DIM), lambda i: (layer, i, 0)),
            pl.BlockSpec((None, 1, d), lambda i: (layer, 0, 0)),
            pl.BlockSpec((None, d, d), lambda i: (layer, 0, 0)),
            pl.BlockSpec((None, PLE_DIM, d), lambda i: (layer, 0, 0)),
            pl.BlockSpec((1, d), lambda i: (0, 0)),
        ],
        out_specs=pl.BlockSpec((tm, d), lambda i: (i, 0)),
        out_shape=jax.ShapeDtypeStruct((t, d), F32),
        compiler_params=_cparams("parallel"),
        name="ple",
    )(x, p, gain, wg, wp, final_gain)


def kernel(x, p, ffn1_norm, ffn1_w_in, ffn1_w_out, mix_norm, w_mix_in, cmp_pos, cmp_w1, cmp_b1, cmp_w2,
           cmp_b2, rel_table, nsa_out_norm, conv_w, conv_b, dt_bias, a_log, d_skip, ssm_out_norm, w_mix_out,
           ffn2_norm, ffn2_w_in, ffn2_w_out, ple_norm, ple_gate_w, ple_proj_w, final_norm):
    batch, seq, d = x.shape
    depth = p.shape[0]
    t = batch * seq
    bf = lambda a: a.astype(BF16)
    row = lambda a: a[:, None, :]

    ffn1_in, ffn1_out, ffn2_in, ffn2_out = bf(ffn1_w_in), bf(ffn1_w_out), bf(ffn2_w_in), bf(ffn2_w_out)
    w_proj = _prep_inproj_weight(w_mix_in)
    wo_attn = bf(_perm_heads(w_mix_out[:, :NSA_WIDTH], axis=1))
    wo_ssm = bf(w_mix_out[:, NSA_WIDTH:])
    nsa_gain = row(_perm_heads(nsa_out_norm, axis=1))
    ssd_prm = _prep_ssd_params(conv_w, conv_b, dt_bias, a_log, d_skip, ssm_out_norm)
    ssd_consts = _ssd_consts()
    tb, biasc = _nsa_tables(rel_table, seq)
    ovt = _nsa_consts(seq)
    cprep = _prep_compress(cmp_pos, cmp_w1, cmp_b1, cmp_w2, cmp_b2)
    wg, wp = bf(ple_gate_w), bf(ple_proj_w)
    p2 = p.reshape(depth, t, PLE_DIM)
    fgain = final_norm[None, :]

    h = x.reshape(t, d)
    for i in range(depth):
        h = _ffn(h, row(ffn1_norm), ffn1_in, ffn1_out, i)
        q, kc, vc, kv, misc, z, xbc = _inproj(h, row(mix_norm), w_proj, i)
        kcmp, vcmp = _compress(kc, vc, cprep, i, batch, seq)
        o_attn = _nsa_attention(q, kcmp, vcmp, kv, misc, tb, biasc, ovt, batch, seq)
        o_ssm = _ssd(xbc, z, misc, ssd_prm, ssd_consts, i, batch, seq)
        h = _outproj(h, o_attn, o_ssm, nsa_gain, wo_attn, wo_ssm, i)
        h = _ffn(h, row(ffn2_norm), ffn2_in, ffn2_out, i)
        h = _ple(h, p2, row(ple_norm), wg, wp, fgain, i, final=(i == depth - 1))
    return h.reshape(batch, seq, d)
```

```python
import functools
import math

import numpy as np
import jax
import jax.numpy as jnp
from jax import lax
from jax.experimental import pallas as pl
from jax.experimental.pallas import tpu as pltpu

F32 = jnp.float32
BF16 = jnp.bfloat16

D_MODEL = 1024
DEPTH = 4
PLE_DIM = 256
D_FF = 2816
EPS = 1e-6
NEG_INF = -1e30
FORCE_SCORE = 1e4

NSA_HEADS = 8
NSA_KV_GROUPS = 2
NSA_REP = NSA_HEADS // NSA_KV_GROUPS
HEAD_DIM = 64
NSA_WIDTH = NSA_HEADS * HEAD_DIM
KV_WIDTH = NSA_KV_GROUPS * HEAD_DIM
CMP_BLOCK = 32
CMP_STRIDE = 16
CMP_HIDDEN = 256
SEL_BLOCK = 64
SEL_TOPK = 8
WINDOW = 512
REL_BUCKETS = 32
REL_MAX_DIST = 128

SSM_HEADS = 16
SSM_HEAD_DIM = 64
SSM_INNER = SSM_HEADS * SSM_HEAD_DIM
SSM_GROUPS = 2
SSM_STATE = 128
CONV_WIDTH = 4
SSD_CHUNK = 128
CONV_CH = SSM_INNER + 2 * SSM_GROUPS * SSM_STATE

LANES = 128
VMEM_LIMIT_BYTES = 48 * 1024 * 1024

FFN_TM = 1024
FFN_TF = 256
PROJ_TM = 512
RESID_TM = 1024
Q_TILE = 128
K_TILE = 256
HEAD_ORDER = (0, 4, 1, 5, 2, 6, 3, 7)
GATE_LANE0 = 0
DT_LANE0 = 3 * NSA_HEADS
N_BIAS_TABLES = 6
FAR_TABLE = 3
LOG2E = math.log2(math.e)
BF16_SUBLANES = 16
CONV_TAIL = 8
SSD_STEP_CHUNKS = 8


def _dot(a, b):
    return jnp.dot(a, b, preferred_element_type=F32)


def _dot_nt(a, b):
    return lax.dot_general(a, b, (((1,), (1,)), ((), ())), preferred_element_type=F32)


def _split3(v):
    hi = v.astype(BF16)
    r = v - hi.astype(F32)
    mid = r.astype(BF16)
    lo = (r - mid.astype(F32)).astype(BF16)
    return hi, mid, lo


def _dot_f32x2_lhs(v, e):
    hi = v.astype(BF16)
    lo = (v - hi.astype(F32)).astype(BF16)
    return _dot(hi, e) + _dot(lo, e)


def _dot_f32_rhs(e, v):
    hi, mid, lo = _split3(v)
    return _dot(e, hi) + _dot(e, mid) + _dot(e, lo)


def _rms(x, g):
    ms = jnp.mean(x * x, axis=-1, keepdims=True)
    return x * lax.rsqrt(ms + EPS) * g


def _silu(x):
    return x * jax.nn.sigmoid(x)


def _cparams(*sem):
    return pltpu.CompilerParams(dimension_semantics=sem, vmem_limit_bytes=VMEM_LIMIT_BYTES)


def _ffn_kernel(x_ref, g_ref, wg_ref, wu_ref, wo_ref, o_ref, xn_ref, acc_ref):
    j = pl.program_id(1)

    @pl.when(j == 0)
    def _():
        xn_ref[...] = _rms(x_ref[...], g_ref[...]).astype(BF16)
        acc_ref[...] = jnp.zeros_like(acc_ref)

    xn = xn_ref[...]
    gate = _dot(xn, wg_ref[...])
    up = _dot(xn, wu_ref[...])
    h = (_silu(gate) * up).astype(BF16)
    acc_ref[...] += _dot(h, wo_ref[...])

    @pl.when(j == pl.num_programs(1) - 1)
    def _():
        o_ref[...] = x_ref[...] + 0.5 * acc_ref[...]


def _ffn(x, gain, w_in, w_out, layer):
    t, d = x.shape
    nf = D_FF // FFN_TF
    return pl.pallas_call(
        _ffn_kernel,
        grid=(t // FFN_TM, nf),
        in_specs=[
            pl.BlockSpec((FFN_TM, d), lambda i, j: (i, 0)),
            pl.BlockSpec((None, 1, d), lambda i, j: (layer, 0, 0)),
            pl.BlockSpec((None, d, FFN_TF), lambda i, j: (layer, 0, j)),
            pl.BlockSpec((None, d, FFN_TF), lambda i, j: (layer, 0, j + nf)),
            pl.BlockSpec((None, FFN_TF, d), lambda i, j: (layer, j, 0)),
        ],
        out_specs=pl.BlockSpec((FFN_TM, d), lambda i, j: (i, 0)),
        out_shape=jax.ShapeDtypeStruct((t, d), F32),
        scratch_shapes=[pltpu.VMEM((FFN_TM, d), BF16), pltpu.VMEM((FFN_TM, d), F32)],
        compiler_params=_cparams("parallel", "arbitrary"),
        name="ffn",
    )(x, gain, w_in, w_in, w_out)


_C_Q = (0, NSA_WIDTH)
_C_KC = (_C_Q[1], _C_Q[1] + KV_WIDTH)
_C_VC = (_C_KC[1], _C_KC[1] + KV_WIDTH)
_C_KV = (_C_VC[1], _C_VC[1] + 4 * KV_WIDTH)
_C_MISC = (_C_KV[1], _C_KV[1] + LANES)
_C_Z = (_C_MISC[1], _C_MISC[1] + SSM_INNER)
_C_XBC = (_C_Z[1], _C_Z[1] + CONV_CH)
PROJ_COLS = _C_XBC[1]


def _inproj_kernel(x_ref, g_ref, w_ref, q_ref, kc_ref, vc_ref, kv_ref, misc_ref, z_ref, xbc_ref, rows_ref):
    xn = _rms(x_ref[...], g_ref[...]).astype(BF16)
    q_ref[...] = _dot(xn, w_ref[:, _C_Q[0]:_C_Q[1]])
    nrow = rows_ref.shape[0] // CMP_STRIDE
    for cols, dst in ((_C_KC, kc_ref), (_C_VC, vc_ref)):
        rows_ref[...] = _dot(xn, w_ref[:, cols[0]:cols[1]])
        for l in range(CMP_STRIDE):
            dst[:, l * KV_WIDTH:(l + 1) * KV_WIDTH] = rows_ref[pl.ds(l, nrow, stride=CMP_STRIDE), :]
    kv_ref[...] = _dot(xn, w_ref[:, _C_KV[0]:_C_KV[1]]).astype(BF16)
    misc_ref[...] = _dot(xn, w_ref[:, _C_MISC[0]:_C_MISC[1]])
    z_ref[...] = _dot(xn, w_ref[:, _C_Z[0]:_C_Z[1]])
    xbc_ref[...] = _dot(xn, w_ref[:, _C_XBC[0]:_C_XBC[1]])


def _inproj(x, gain, w, layer):
    t, d = x.shape
    tm = PROJ_TM
    grouped = (tm // CMP_STRIDE, t // CMP_STRIDE, CMP_STRIDE * KV_WIDTH, F32)
    outs = ((tm, t, NSA_WIDTH, F32), grouped, grouped, (tm, t, 4 * KV_WIDTH, BF16), (tm, t, LANES, F32),
            (tm, t, SSM_INNER, F32), (tm, t, CONV_CH, F32))
    return pl.pallas_call(
        _inproj_kernel,
        grid=(t // tm,),
        in_specs=[
            pl.BlockSpec((tm, d), lambda i: (i, 0)),
            pl.BlockSpec((None, 1, d), lambda i: (layer, 0, 0)),
            pl.BlockSpec((None, d, PROJ_COLS), lambda i: (layer, 0, 0)),
        ],
        out_specs=[pl.BlockSpec((rows, n), lambda i: (i, 0)) for rows, _, n, _ in outs],
        out_shape=[jax.ShapeDtypeStruct((total, n), dt) for _, total, n, dt in outs],
        scratch_shapes=[pltpu.VMEM((tm, KV_WIDTH), F32)],
        compiler_params=_cparams("parallel"),
        name="inproj",
    )(x, gain, w)


_IN_OFFS = tuple(int(v) for v in np.cumsum(
    (0, NSA_WIDTH) + (KV_WIDTH,) * 6 + (3 * NSA_HEADS, SSM_INNER, CONV_CH, SSM_HEADS)))
PREP_ROWS = 256


def _prep_inproj_kernel(w_ref, o_ref):
    o = _IN_OFFS
    o_ref[:, _C_KC[0]:_C_KV[1]] = w_ref[:, o[1]:o[7]].astype(BF16)
    o_ref[:, _C_Z[0]:_C_XBC[1]] = w_ref[:, o[8]:o[10]].astype(BF16)
    for pos, h in enumerate(HEAD_ORDER):
        o_ref[:, pos * HEAD_DIM:(pos + 1) * HEAD_DIM] = w_ref[:, h * HEAD_DIM:(h + 1) * HEAD_DIM].astype(BF16)
    ngate, ndt = o[8] - o[7], o[11] - o[10]
    m0 = _C_MISC[0]
    o_ref[:, m0:m0 + ngate] = w_ref[:, o[7]:o[8]].astype(BF16)
    o_ref[:, m0 + ngate:m0 + ngate + ndt] = w_ref[:, o[10]:o[11]].astype(BF16)
    o_ref[:, m0 + ngate + ndt:_C_MISC[1]] = jnp.zeros((o_ref.shape[0], LANES - ngate - ndt), BF16)


def _prep_inproj_weight(w_mix_in):
    nl, d, win = w_mix_in.shape
    return pl.pallas_call(
        _prep_inproj_kernel,
        grid=(nl, d // PREP_ROWS),
        in_specs=[pl.BlockSpec((None, PREP_ROWS, win), lambda l, r: (l, r, 0))],
        out_specs=pl.BlockSpec((None, PREP_ROWS, PROJ_COLS), lambda l, r: (l, r, 0)),
        out_shape=jax.ShapeDtypeStruct((nl, d, PROJ_COLS), BF16),
        compiler_params=_cparams("parallel", "parallel"),
        name="inproj_weight_layout",
    )(w_mix_in)


def _compress_kernel(kc_ref, vc_ref, pos_ref, w1a_ref, w1b_ref, b1_ref, w2_ref, b2_ref,
                     kcmp_ref, vcmpt_ref):
    nrow = kc_ref.shape[0]
    for which, src in enumerate((kc_ref, vc_ref)):
        r = src[...]
        ra = (r + pos_ref[which, 0]).astype(BF16)
        rb = (r + pos_ref[which, 1]).astype(BF16)
        ha = _dot(ra, w1a_ref[which])
        hb = _dot(rb, w1b_ref[which])
        h = ha + pltpu.roll(hb, nrow - 1, 0) + b1_ref[which]
        out = _dot(_silu(h).astype(BF16), w2_ref[which]) + b2_ref[which]
        if which == 0:
            kcmp_ref[...] = out.astype(BF16)
        else:
            vcmpt_ref[...] = out.T.astype(BF16)


def _compress(kc, vc, prep, layer, batch, seq):
    nrow = seq // CMP_STRIDE
    wide = CMP_STRIDE * KV_WIDTH
    kcr = kc.reshape(batch, nrow, wide)
    vcr = vc.reshape(batch, nrow, wide)
    lay = lambda a: pl.BlockSpec((None,) + a.shape[1:], lambda b: (layer,) + (0,) * (a.ndim - 1))
    consts = (prep["pos"], prep["w1a"], prep["w1b"], prep["b1"], prep["w2"], prep["b2"])
    return pl.pallas_call(
        _compress_kernel,
        grid=(batch,),
        in_specs=[pl.BlockSpec((None, nrow, wide), lambda b: (b, 0, 0))] * 2 + [lay(a) for a in consts],
        out_specs=[pl.BlockSpec((None, nrow, KV_WIDTH), lambda b: (b, 0, 0)),
                   pl.BlockSpec((None, KV_WIDTH, nrow), lambda b: (b, 0, 0))],
        out_shape=[jax.ShapeDtypeStruct((batch, nrow, KV_WIDTH), BF16),
                   jax.ShapeDtypeStruct((batch, KV_WIDTH, nrow), BF16)],
        compiler_params=_cparams("parallel"),
        name="nsa_compress",
    )(kcr, vcr, *consts)


def _prep_compress(cmp_pos, cmp_w1, cmp_b1, cmp_w2, cmp_b2):
    assert NSA_KV_GROUPS == 2
    nl = cmp_w1.shape[0]
    half = CMP_BLOCK // 2
    w1 = cmp_w1.astype(BF16).reshape(nl, 2, CMP_BLOCK, HEAD_DIM, CMP_HIDDEN)

    def block_diag(w, axis):
        z = jnp.zeros_like(w)
        return jnp.stack([jnp.concatenate([w, z], axis=-1), jnp.concatenate([z, w], axis=-1)], axis=axis)

    def expand(w):
        return block_diag(w, 3).reshape(nl, 2, half * 2 * HEAD_DIM, 2 * CMP_HIDDEN)

    def pos_rows(p):
        return jnp.broadcast_to(p[:, :, :, None, :], (nl, 2, half, 2, HEAD_DIM)).reshape(nl, 2, 1, half * 2 * HEAD_DIM)

    w2 = block_diag(cmp_w2.astype(BF16), 2).reshape(nl, 2, 2 * CMP_HIDDEN, 2 * HEAD_DIM)
    return {
        "pos": jnp.stack([pos_rows(cmp_pos[:, :, :half]), pos_rows(cmp_pos[:, :, half:])], axis=2),
        "w1a": expand(w1[:, :, :half]),
        "w1b": expand(w1[:, :, half:]),
        "b1": jnp.tile(cmp_b1, (1, 1, 2))[:, :, None, :],
        "w2": w2,
        "b2": jnp.tile(cmp_b2, (1, 1, 2))[:, :, None, :],
    }


def _t5_bucket_np(dist):
    n = np.maximum(dist, 0)
    exact = REL_BUCKETS // 2
    nf = np.maximum(n, exact).astype(np.float64)
    large = exact + (np.log(nf / exact) / math.log(REL_MAX_DIST / exact) * (REL_BUCKETS - exact)).astype(np.int64)
    return np.where(n < exact, n, np.minimum(large, REL_BUCKETS - 1)).astype(np.int32)


MASKED_BUCKET = REL_BUCKETS


def _bucket_maps(seq):
    j = np.arange(K_TILE)[:, None]
    i = np.arange(Q_TILE)[None, :]
    tiles = []
    for delta in range(N_BIAS_TABLES):
        d = Q_TILE * delta + i - j
        valid = (d >= 0) & ((d < WINDOW) if delta >= 4 else True)
        tiles.append(np.where(valid, _t5_bucket_np(d), MASKED_BUCKET))
    c = np.arange(seq // CMP_STRIDE)[:, None]
    t = np.arange(seq)[None, :]
    dc = t - (c * CMP_STRIDE + CMP_BLOCK - 1)
    cmp_map = np.where(dc >= 0, _t5_bucket_np(dc), MASKED_BUCKET)
    return np.stack(tiles).astype(np.int32), cmp_map.astype(np.int32)


def _tables_kernel(tab_ref, bkt_ref, bkc_ref, tb_ref, bc_ref):
    h = pl.program_id(0)
    for src, dst in ((bkt_ref, tb_ref), (bkc_ref, bc_ref)):
        bk = src[...]
        out = jnp.zeros(bk.shape, F32)
        for b in range(REL_BUCKETS + 1):
            out = jnp.where(bk == b, tab_ref[b, h], out)
        dst[...] = (out * LOG2E).astype(dst.dtype)


def _nsa_tables(rel_table, seq):
    bkt, bkc = _bucket_maps(seq)
    tab = jnp.concatenate([rel_table[:, np.asarray(HEAD_ORDER)],
                           jnp.full((1, NSA_HEADS), NEG_INF, F32)], axis=0)
    nc = seq // CMP_STRIDE
    return pl.pallas_call(
        _tables_kernel,
        grid=(NSA_HEADS,),
        in_specs=[
            pl.BlockSpec(memory_space=pltpu.SMEM),
            pl.BlockSpec(bkt.shape, lambda h: (0, 0, 0)),
            pl.BlockSpec(bkc.shape, lambda h: (0, 0)),
        ],
        out_specs=[pl.BlockSpec((N_BIAS_TABLES, None, K_TILE, Q_TILE), lambda h: (0, h, 0, 0)),
                   pl.BlockSpec((None, nc, seq), lambda h: (h, 0, 0))],
        out_shape=[jax.ShapeDtypeStruct((N_BIAS_TABLES, NSA_HEADS, K_TILE, Q_TILE), BF16),
                   jax.ShapeDtypeStruct((NSA_HEADS, nc, seq), F32)],
        compiler_params=_cparams("parallel"),
        name="nsa_bias_tables",
    )(tab, jnp.asarray(bkt), jnp.asarray(bkc))


def _nsa_consts(seq):
    nc = seq // CMP_STRIDE
    nb = seq // SEL_BLOCK
    c = np.arange(nc)
    blk = np.arange(nb)
    c_lo, c_hi = c * CMP_STRIDE, c * CMP_STRIDE + CMP_BLOCK - 1
    s_lo, s_hi = blk * SEL_BLOCK, blk * SEL_BLOCK + SEL_BLOCK - 1
    ovt = (c_lo[None, :] <= s_hi[:, None]) & (c_hi[None, :] >= s_lo[:, None])
    ovt[:, nc - 1] = False
    return jnp.asarray(ovt, BF16)


def _nsa_kernel(q_ref, kcmp_ref, vcmpt_ref, ks_ref, vs_ref, kw_ref, vw_ref, misc_ref, biasc_ref, tb_ref,
                ovt_ref, o_ref, qpt_ref, vst_ref, vwt_ref, madd_ref, m_ref, acc_ref, mw_ref, accw_ref):
    n = pl.program_id(1)
    nkt = ks_ref.shape[0] // K_TILE
    nsel_blocks = ovt_ref.shape[0]
    nhb = NSA_HEADS
    cols = lambda hb: slice(hb * Q_TILE, (hb + 1) * Q_TILE)
    frow = lax.broadcasted_iota(jnp.int32, (KV_WIDTH, Q_TILE), 0)
    low = frow < HEAD_DIM

    @pl.when(n == 0)
    def _():
        arow = lax.broadcasted_iota(jnp.int32, (BF16_SUBLANES, K_TILE), 0)
        ones_row = jnp.where(arow == 0, 1.0, 0.0).astype(BF16)
        for t2 in range(nkt):
            rows = slice(t2 * K_TILE, (t2 + 1) * K_TILE)
            for src, dst in ((vs_ref, vst_ref), (vw_ref, vwt_ref)):
                dst[t2, 0:KV_WIDTH, :] = src[rows, :].astype(F32).T.astype(BF16)
                dst[t2, KV_WIDTH:, :] = ones_row

    scale = HEAD_DIM ** -0.5 * LOG2E
    for j in range(nhb // 2):
        slab = (q_ref[:, j * LANES:(j + 1) * LANES].T * scale).astype(BF16)
        zero = jnp.zeros_like(slab)
        qpt_ref[:, cols(2 * j)] = jnp.where(low, slab, zero)
        qpt_ref[:, cols(2 * j + 1)] = jnp.where(low, zero, slab)
    qpt = qpt_ref[...]

    pair_cols = lambda pair: slice(2 * pair * Q_TILE, (2 * pair + 2) * Q_TILE)

    def branch(k_ref, vt_ref, mx_ref, ac_ref, selected):
        def scores(t2):
            start = pl.multiple_of(t2 * K_TILE, K_TILE)
            k = k_ref[pl.ds(start, K_TILE), :]
            return [_dot(k, qpt_ref[:, pair_cols(pair)]) for pair in range(nhb // 2)]

        def softmax_pv(t2, s_pairs):
            vt = vt_ref[t2]
            delta = n - 2 * t2
            ti = jnp.minimum(delta, FAR_TABLE) if selected else delta
            blk0 = t2 * (K_TILE // SEL_BLOCK)
            for pair in range(nhb // 2):
                c2 = pair_cols(pair)
                s2 = s_pairs[pair]
                ps, alphas = [], []
                for half in range(2):
                    hb = 2 * pair + half
                    s = s2[:, half * Q_TILE:(half + 1) * Q_TILE].astype(BF16) + tb_ref[ti, hb]
                    if selected:
                        s = jnp.concatenate(
                            [s[j * SEL_BLOCK:(j + 1) * SEL_BLOCK]
                             + madd_ref[half, pl.ds(blk0 + j, 1), :].astype(BF16)
                             for j in range(K_TILE // SEL_BLOCK)], axis=0)
                    m_prev = mx_ref[:, cols(hb)]
                    m_new = jnp.maximum(m_prev, jnp.max(s, axis=0, keepdims=True).astype(F32))
                    mx_ref[:, cols(hb)] = m_new
                    alphas.append(jnp.exp2(m_prev - m_new))
                    ps.append(jnp.exp2(s - m_new.astype(BF16)))
                pv = _dot(vt, jnp.concatenate(ps, axis=1))
                ac_ref[:, c2] = ac_ref[:, c2] * jnp.concatenate(alphas, axis=1) + pv

        def issue(first, k):
            return [scores(first + i) for i in range(k)]

        def finish(first, ss):
            for i, s_pairs in enumerate(ss):
                softmax_pv(first + i, s_pairs)

        def init():
            mx_ref[...] = jnp.full_like(mx_ref, -jnp.inf)
            ac_ref[...] = jnp.zeros_like(ac_ref)

        def result():
            acc = ac_ref[...]
            return acc[0:KV_WIDTH] / acc[KV_WIDTH:KV_WIDTH + 1]

        return init, issue, finish, result

    sel_init, sel_issue, sel_finish, sel_result = branch(ks_ref, vst_ref, m_ref, acc_ref, True)
    win_init, win_issue, win_finish, win_result = branch(kw_ref, vwt_ref, mw_ref, accw_ref, False)

    tq = n * Q_TILE + lax.broadcasted_iota(jnp.int32, (1, Q_TILE), 1)
    t2_diag = n // 2

    def q_tile(sel_k, win_k):
        win_first = t2_diag + 1 - win_k
        win_init()
        sel_init()
        has_block = (tq >= CMP_BLOCK - 1).astype(F32)
        sc = _dot(kcmp_ref[...], qpt)
        win_scores = win_issue(win_first, win_k)
        psum = [None, None]
        pcs = []
        for hb in range(nhb):
            s = sc[:, cols(hb)] + biasc_ref[hb]
            m = jnp.max(s, axis=0, keepdims=True)
            e = jnp.exp2(s - m)
            p = e / jnp.sum(e, axis=0, keepdims=True) * has_block
            pcs.append(p.astype(BF16))
            g = hb % 2
            psum[g] = p if psum[g] is None else psum[g] + p
        o_cmp = _dot(vcmpt_ref[...], jnp.concatenate(pcs, axis=1))

        blk = lax.broadcasted_iota(jnp.int32, (nsel_blocks, Q_TILE), 0)
        blk_f = blk.astype(F32)
        cur = tq // SEL_BLOCK
        forced = ((blk == 0) | (blk == cur) | (blk == cur - 1)).astype(F32)
        imps = [_dot_f32_rhs(ovt_ref[...], psum[g]) for g in range(NSA_KV_GROUPS)]
        sel_scores = sel_issue(0, sel_k)
        for g in range(NSA_KV_GROUPS):
            imp = imps[g]
            score = jnp.where(blk <= cur, imp + FORCE_SCORE * forced, -FORCE_SCORE)
            sel = jnp.zeros(score.shape, F32)
            for _ in range(SEL_TOPK):
                mx = jnp.max(score, axis=0, keepdims=True)
                first = jnp.min(jnp.where(score == mx, blk_f, float(nsel_blocks)), axis=0, keepdims=True)
                hit = blk_f == first
                sel = jnp.where(hit, 1.0, sel)
                score = jnp.where(hit, -jnp.inf, score)
            madd_ref[g] = (sel - 1.0) * (-NEG_INF)
        win_finish(win_first, win_scores)
        sel_finish(0, sel_scores)
        o_sel = sel_result()
        o_win = win_result()

        gates = jax.nn.sigmoid(misc_ref[...]).T

        def gate(branch, hb):
            col = GATE_LANE0 + branch * NSA_HEADS + HEAD_ORDER[hb]
            return gates[col:col + 1, :]

        for j in range(nhb // 2):
            outs = []
            for hb in (2 * j, 2 * j + 1):
                outs.append(gate(0, hb) * o_cmp[:, cols(hb)] + gate(1, hb) * o_sel[:, cols(hb)]
                            + gate(2, hb) * o_win[:, cols(hb)])
            o_ref[:, j * LANES:(j + 1) * LANES] = jnp.where(low, outs[0], outs[1]).T

    win_full = WINDOW // K_TILE + 1
    for sel_k in range(1, nkt + 1):
        @pl.when(t2_diag + 1 == sel_k)
        def _(sel_k=sel_k):
            q_tile(sel_k, min(sel_k, win_full))


def _nsa_attention(q, kcmp, vcmpt, kv, misc, tb, biasc, ovt, batch, seq):
    nq = seq // Q_TILE
    nc = seq // CMP_STRIDE
    nkt = seq // K_TILE
    cols_all = NSA_HEADS * Q_TILE
    vrows = KV_WIDTH + BF16_SUBLANES
    kvspec = lambda c: pl.BlockSpec((seq, KV_WIDTH), lambda b, n: (b, c))
    return pl.pallas_call(
        _nsa_kernel,
        grid=(batch, nq),
        in_specs=[
            pl.BlockSpec((Q_TILE, NSA_WIDTH), lambda b, n: (b * nq + n, 0)),
            pl.BlockSpec((None, nc, KV_WIDTH), lambda b, n: (b, 0, 0)),
            pl.BlockSpec((None, KV_WIDTH, nc), lambda b, n: (b, 0, 0)),
            kvspec(0), kvspec(1), kvspec(2), kvspec(3),
            pl.BlockSpec((Q_TILE, LANES), lambda b, n: (b * nq + n, 0)),
            pl.BlockSpec((NSA_HEADS, nc, Q_TILE), lambda b, n: (0, 0, n)),
            pl.BlockSpec(tb.shape, lambda b, n: (0, 0, 0, 0)),
            pl.BlockSpec(ovt.shape, lambda b, n: (0, 0)),
        ],
        out_specs=pl.BlockSpec((Q_TILE, NSA_WIDTH), lambda b, n: (b * nq + n, 0)),
        out_shape=jax.ShapeDtypeStruct((batch * seq, NSA_WIDTH), F32),
        scratch_shapes=[
            pltpu.VMEM((KV_WIDTH, cols_all), BF16),
            pltpu.VMEM((nkt, vrows, K_TILE), BF16),
            pltpu.VMEM((nkt, vrows, K_TILE), BF16),
            pltpu.VMEM((NSA_KV_GROUPS, seq // SEL_BLOCK, Q_TILE), F32),
            pltpu.VMEM((1, cols_all), F32),
            pltpu.VMEM((vrows, cols_all), F32),
            pltpu.VMEM((1, cols_all), F32),
            pltpu.VMEM((vrows, cols_all), F32),
        ],
        compiler_params=_cparams("arbitrary", "arbitrary"),
        name="nsa_attention",
    )(q, kcmp, vcmpt, kv, kv, kv, kv, misc, biasc, tb, ovt)


def _softplus(x):
    return jnp.maximum(x, 0.0) + jnp.log1p(jnp.exp(-jnp.abs(x)))


def _ssd_kernel(xbc_ref, z_ref, misc_ref, convw_ref, convb_ref, dtb_ref, alog_ref, dskip_ref, gain_ref,
                tri_ref, e1_ref, o_ref, prev_ref, h_ref):
    L = SSD_CHUNK

    @pl.when(pl.program_id(1) == 0)
    def _():
        prev_ref[...] = jnp.zeros_like(prev_ref)
        h_ref[...] = jnp.zeros_like(h_ref)

    for s in range(SSD_STEP_CHUNKS):
        rows = pl.ds(s * L, L)
        tail = prev_ref[...] if s == 0 else xbc_ref[s * L - CONV_TAIL:s * L, :]
        _ssd_chunk(xbc_ref.at[rows], z_ref.at[rows], misc_ref.at[rows], convw_ref, convb_ref, dtb_ref, alog_ref,
                   dskip_ref, gain_ref, tri_ref, e1_ref, o_ref.at[rows], tail, h_ref)
    prev_ref[...] = xbc_ref[SSD_STEP_CHUNKS * L - CONV_TAIL:SSD_STEP_CHUNKS * L, :]


def _ssd_chunk(xbc_ref, z_ref, misc_ref, convw_ref, convb_ref, dtb_ref, alog_ref, dskip_ref, gain_ref,
               tri_ref, e1_ref, o_ref, tail, h_ref):
    L = SSD_CHUNK
    gw = SSM_INNER // SSM_GROUPS
    hpg = SSM_HEADS // SSM_GROUPS

    x = xbc_ref[...]
    ng = L // CONV_TAIL
    x3 = x.reshape(ng, CONV_TAIL, CONV_CH)
    sub = lax.broadcasted_iota(jnp.int32, (1, CONV_TAIL, 1), 1)
    acc3 = convb_ref[...] + x3 * convw_ref[CONV_WIDTH - 1:CONV_WIDTH, :]
    for k in range(1, CONV_WIDTH):
        rot = pltpu.roll(x3, k, 1)
        before = jnp.concatenate([pltpu.roll(tail, k, 0)[None], rot[:ng - 1]], axis=0)
        acc3 = acc3 + jnp.where(sub >= k, rot, before) * convw_ref[CONV_WIDTH - 1 - k:CONV_WIDTH - k, :]
    xa = _silu(acc3).reshape(L, CONV_CH)
    xs = xa[:, :SSM_INNER]
    bm = xa[:, SSM_INNER:SSM_INNER + SSM_GROUPS * SSM_STATE]
    cm = xa[:, SSM_INNER + SSM_GROUPS * SSM_STATE:]

    dt = _softplus(misc_ref[...] + dtb_ref[...])
    da = dt * (-jnp.exp(alog_ref[...]))
    cs = _dot_f32_rhs(tri_ref[...], da) * LOG2E
    cs_t = cs.T
    dt_t = dt.T
    ecs = _dot_f32x2_lhs(jnp.exp2(cs), e1_ref[...])
    to_end = dt * jnp.exp2(cs[L - 1:L, :] - cs)
    xw_b = (xs * _dot_f32x2_lhs(to_end, e1_ref[...])).astype(BF16)
    xs_b = xs.astype(BF16)
    state_decay = ecs[L - 1:L, :]

    li = lax.broadcasted_iota(jnp.int32, (L, L), 0)
    si = lax.broadcasted_iota(jnp.int32, (L, L), 1)
    causal = li >= si
    low = si < SSM_HEAD_DIM

    ys = []
    for g in range(SSM_GROUPS):
        bg = bm[:, g * SSM_STATE:(g + 1) * SSM_STATE]
        cg = cm[:, g * SSM_STATE:(g + 1) * SSM_STATE].astype(BF16)
        cb = _dot_nt(cg, bg.astype(BF16))
        h_g = h_ref[:, g * gw:(g + 1) * gw]
        y_off = _dot(cg, h_g.astype(BF16)) * ecs[:, g * gw:(g + 1) * gw]
        for pr in range(hpg // 2):
            h0 = g * hpg + 2 * pr
            gs = []
            for hh in (h0, h0 + 1):
                ln = DT_LANE0 + hh
                col = jnp.broadcast_to(cs[:, ln:ln + 1], (L, L))
                dec = jnp.exp2(jnp.where(causal, col - cs_t[ln:ln + 1, :], NEG_INF))
                gs.append((cb * dec * dt_t[ln:ln + 1, :]).astype(BF16))
            ch = slice(h0 * SSM_HEAD_DIM, (h0 + 2) * SSM_HEAD_DIM)
            xpair = xs_b[:, ch]
            zero = jnp.zeros_like(xpair)
            rhs = jnp.concatenate([jnp.where(low, xpair, zero), jnp.where(low, zero, xpair)], axis=0)
            y_diag = _dot(jnp.concatenate(gs, axis=1), rhs)
            off = slice(2 * pr * SSM_HEAD_DIM, (2 * pr + 2) * SSM_HEAD_DIM)
            ys.append(y_diag + y_off[:, off] + xs[:, ch] * dskip_ref[:, ch])
        st = _dot(bg.T.astype(BF16), xw_b[:, g * gw:(g + 1) * gw])
        h_ref[:, g * gw:(g + 1) * gw] = h_g * state_decay[:, g * gw:(g + 1) * gw] + st

    y = jnp.concatenate(ys, axis=1) * _silu(z_ref[...])
    outs = []
    for g in range(SSM_GROUPS):
        outs.append(_rms(y[:, g * gw:(g + 1) * gw], gain_ref[:, g * gw:(g + 1) * gw]))
    o_ref[...] = jnp.concatenate(outs, axis=1).astype(BF16)


def _ssd_consts():
    lane = np.arange(LANES)
    tri = (np.arange(SSD_CHUNK)[:, None] >= np.arange(SSD_CHUNK)[None, :])
    head1 = np.arange(SSM_INNER) // SSM_HEAD_DIM
    e1 = (lane[:, None] - DT_LANE0) == head1[None, :]
    return jnp.asarray(tri, BF16), jnp.asarray(e1, BF16)


def _ssd(xbc, z, misc, prm, consts, layer, batch, seq):
    step_rows = SSD_STEP_CHUNKS * SSD_CHUNK
    nch = seq // step_rows
    tok = lambda n: pl.BlockSpec((step_rows, n), lambda b, c: (b * nch + c, 0))
    lay = lambda a: pl.BlockSpec((None,) + a.shape[1:], lambda b, c: (layer,) + (0,) * (a.ndim - 1))
    full = lambda a: pl.BlockSpec(a.shape, lambda b, c: (0,) * a.ndim)
    params = (prm["conv_w"], prm["conv_b"], prm["dt_bias"], prm["a_log"], prm["d_skip"], prm["ssm_gain"])
    return pl.pallas_call(
        _ssd_kernel,
        grid=(batch, nch),
        in_specs=[tok(CONV_CH), tok(SSM_INNER), tok(LANES)] + [lay(a) for a in params] + [full(a) for a in consts],
        out_specs=tok(SSM_INNER),
        out_shape=jax.ShapeDtypeStruct((batch * seq, SSM_INNER), BF16),
        scratch_shapes=[pltpu.VMEM((CONV_TAIL, CONV_CH), F32), pltpu.VMEM((SSM_STATE, SSM_INNER), F32)],
        compiler_params=_cparams("parallel", "arbitrary"),
        name="ssd",
    )(xbc, z, misc, *params, *consts)


def _prep_ssd_params(conv_w, conv_b, dt_bias, a_log, d_skip, ssm_out_norm):
    nl = conv_w.shape[0]

    def dt_lanes(v):
        out = jnp.zeros((nl, 1, LANES), F32)
        return out.at[:, 0, DT_LANE0:DT_LANE0 + SSM_HEADS].set(v)

    return {
        "conv_w": conv_w,
        "conv_b": conv_b[:, None, :],
        "dt_bias": dt_lanes(dt_bias),
        "a_log": dt_lanes(a_log),
        "d_skip": jnp.repeat(d_skip, SSM_HEAD_DIM, axis=-1)[:, None, :],
        "ssm_gain": ssm_out_norm[:, None, :],
    }


def _outproj_kernel(x_ref, oa_ref, os_ref, g_ref, wa_ref, ws_ref, o_ref):
    an = _rms(oa_ref[...], g_ref[...]).astype(BF16)
    o_ref[...] = x_ref[...] + _dot(an, wa_ref[...]) + _dot(os_ref[...], ws_ref[...])


def _outproj(x, o_attn, o_ssm, gain, wa, ws, layer):
    t, d = x.shape
    tm = RESID_TM
    return pl.pallas_call(
        _outproj_kernel,
        grid=(t // tm,),
        in_specs=[
            pl.BlockSpec((tm, d), lambda i: (i, 0)),
            pl.BlockSpec((tm, NSA_WIDTH), lambda i: (i, 0)),
            pl.BlockSpec((tm, SSM_INNER), lambda i: (i, 0)),
            pl.BlockSpec((None, 1, NSA_WIDTH), lambda i: (layer, 0, 0)),
            pl.BlockSpec((None, NSA_WIDTH, d), lambda i: (layer, 0, 0)),
            pl.BlockSpec((None, SSM_INNER, d), lambda i: (layer, 0, 0)),
        ],
        out_specs=pl.BlockSpec((tm, d), lambda i: (i, 0)),
        out_shape=jax.ShapeDtypeStruct((t, d), F32),
        compiler_params=_cparams("parallel"),
        name="outproj",
    )(x, o_attn, o_ssm, gain, wa, ws)


def _perm_heads(a, axis):
    idx = np.concatenate([np.arange(h * HEAD_DIM, (h + 1) * HEAD_DIM) for h in HEAD_ORDER])
    return jnp.take(a, jnp.asarray(idx), axis=axis)


def _ple_kernel(x_ref, p_ref, g_ref, wg_ref, wp_ref, fg_ref, o_ref, *, final):
    x = x_ref[...]
    xn = _rms(x, g_ref[...]).astype(BF16)
    gate = jax.nn.sigmoid(_dot(xn, wg_ref[...]))
    y = x + gate * _dot(p_ref[...].astype(BF16), wp_ref[...])
    if final:
        y = _rms(y, fg_ref[...])
    o_ref[...] = y


def _ple(x, p, gain, wg, wp, final_gain, layer, final):
    t, d = x.shape
    tm = RESID_TM
    return pl.pallas_call(
        functools.partial(_ple_kernel, final=final),
        grid=(t // tm,),
        in_specs=[
            pl.BlockSpec((tm, d), lambda i: (i, 0)),
            pl.BlockSpec((None, tm, PLE_DIM), lambda i: (layer, i, 0)),
            pl.BlockSpec((None, 1, d), lambda i: (layer, 0, 0)),
            pl.BlockSpec((None, d, d), lambda i: (layer, 0, 0)),
            pl.BlockSpec((None, PLE_DIM, d), lambda i: (layer, 0, 0)),
            pl.BlockSpec((1, d), lambda i: (0, 0)),
        ],
        out_specs=pl.BlockSpec((tm, d), lambda i: (i, 0)),
        out_shape=jax.ShapeDtypeStruct((t, d), F32),
        compiler_params=_cparams("parallel"),
        name="ple",
    )(x, p, gain, wg, wp, final_gain)


def kernel(x, p, ffn1_norm, ffn1_w_in, ffn1_w_out, mix_norm, w_mix_in, cmp_pos, cmp_w1, cmp_b1, cmp_w2,
           cmp_b2, rel_table, nsa_out_norm, conv_w, conv_b, dt_bias, a_log, d_skip, ssm_out_norm, w_mix_out,
           ffn2_norm, ffn2_w_in, ffn2_w_out, ple_norm, ple_gate_w, ple_proj_w, final_norm):
    batch, seq, d = x.shape
    depth = p.shape[0]
    t = batch * seq
    bf = lambda a: a.astype(BF16)
    row = lambda a: a[:, None, :]

    ffn1_in, ffn1_out, ffn2_in, ffn2_out = bf(ffn1_w_in), bf(ffn1_w_out), bf(ffn2_w_in), bf(ffn2_w_out)
    w_proj = _prep_inproj_weight(w_mix_in)
    wo_attn = bf(_perm_heads(w_mix_out[:, :NSA_WIDTH], axis=1))
    wo_ssm = bf(w_mix_out[:, NSA_WIDTH:])
    nsa_gain = row(_perm_heads(nsa_out_norm, axis=1))
    ssd_prm = _prep_ssd_params(conv_w, conv_b, dt_bias, a_log, d_skip, ssm_out_norm)
    ssd_consts = _ssd_consts()
    tb, biasc = _nsa_tables(rel_table, seq)
    ovt = _nsa_consts(seq)
    cprep = _prep_compress(cmp_pos, cmp_w1, cmp_b1, cmp_w2, cmp_b2)
    wg, wp = bf(ple_gate_w), bf(ple_proj_w)
    p2 = p.reshape(depth, t, PLE_DIM)
    fgain = final_norm[None, :]

    h = x.reshape(t, d)
    for i in range(depth):
        h = _ffn(h, row(ffn1_norm), ffn1_in, ffn1_out, i)
        q, kc, vc, kv, misc, z, xbc = _inproj(h, row(mix_norm), w_proj, i)
        kcmp, vcmp = _compress(kc, vc, cprep, i, batch, seq)
        o_attn = _nsa_attention(q, kcmp, vcmp, kv, misc, tb, biasc, ovt, batch, seq)
        o_ssm = _ssd(xbc, z, misc, ssd_prm, ssd_consts, i, batch, seq)
        h = _outproj(h, o_attn, o_ssm, nsa_gain, wo_attn, wo_ssm, i)
        h = _ffn(h, row(ffn2_norm), ffn2_in, ffn2_out, i)
        h = _ple(h, p2, row(ple_norm), wg, wp, fgain, i, final=(i == depth - 1))
    return h.reshape(batch, seq, d)
```
